```python
import jax, jax.numpy as jnp
from jax import lax
import numpy as np

D_MODEL = 1024
BATCH = 8
SEQ = 2048
DEPTH = 4
DEC_BATCH = 128
DEC_SEQ = 4
PAST_LEN = 16384
PAGE_SIZE = 128

HEAD_DIM = 64
N_HEADS = D_MODEL // HEAD_DIM
D_FF = 4 * D_MODEL
D_DECAY_LORA = max(32, int(round(1.8 * D_MODEL ** 0.5 / 32)) * 32)
D_AAA_LORA = max(32, int(round(1.8 * D_MODEL ** 0.5 / 32)) * 32)
D_MV_LORA = max(32, int(round(1.3 * D_MODEL ** 0.5 / 32)) * 32)
D_GATE_LORA = max(32, int(round(0.6 * D_MODEL ** 0.8 / 32)) * 32)
N_MIXERS = 2
N_RWKV = (DEPTH + 1) // 2
N_POOL = DEPTH // 2
POOL_WINDOWS = (2, 4, 8, 16)
N_POOL_GROUPS = len(POOL_WINDOWS)
POOL_GROUP = D_MODEL // N_POOL_GROUPS
POOL_BUF = max(POOL_WINDOWS) - 1
NORM_EPS = 1e-6
GN_EPS = 64e-5
L2_EPS = 1e-12

kernel_name = 'rwkv7_pool_hybrid_step'


def rmsnorm(x, g):
    xf = x.astype(jnp.float32)
    y = xf * lax.rsqrt(jnp.mean(xf * xf, axis=-1, keepdims=True) + NORM_EPS)
    return (y * g).astype(x.dtype)


def wkv7_scan(S0, r, decay, k, v, a_vec, b_vec):
    def step(S, inp):
        r_t, w_t, k_t, v_t, a_t, b_t = inp
        sa = jnp.einsum('bhvk,bhk->bhv', S, a_t)
        S = S * w_t[:, :, None, :] + sa[..., None] * b_t[:, :, None, :] + v_t[..., None] * k_t[:, :, None, :]
        return S, jnp.einsum('bhvk,bhk->bhv', S, r_t)
    seq = tuple(jnp.moveaxis(t, 1, 0) for t in (r, decay, k, v, a_vec, b_vec))
    S, y = lax.scan(step, S0, seq)
    return S, jnp.moveaxis(y, 0, 1)


def rwkv7_time_mix(x, shift_prev, wkv_prev, v_first, vres, mix, w_rkv, w_o, w0, w1, w2,
                   a0, a1, a2, g1, g2, k_k, k_a, r_k, gn_w, gn_b):
    B, T, D = x.shape
    f32 = jnp.float32
    x_prev = jnp.concatenate([shift_prev[:, None, :].astype(x.dtype), x[:, :-1]], axis=1)
    xs = x[:, :, None, :] + (x_prev - x)[:, :, None, :] * mix
    rkv = jnp.einsum('btjc,jcd->btjd', xs[:, :, :3], w_rkv)
    r, k, v = rkv[:, :, 0], rkv[:, :, 1], rkv[:, :, 2]
    xv, xw, xa, xg = xs[:, :, 2], xs[:, :, 3], xs[:, :, 4], xs[:, :, 5]
    w_pre = (w0 + jnp.tanh(xw @ w1) @ w2).astype(f32)
    decay = jnp.exp(-jnp.exp(-jax.nn.softplus(-w_pre) - 0.5))
    if vres is None:
        v_first = v
    else:
        v0, v1, v2 = vres
        v = v + (v_first - v) * jax.nn.sigmoid(v0 + (xv @ v1) @ v2)
    a = jax.nn.sigmoid(a0 + (xa @ a1) @ a2)
    g = jax.nn.sigmoid(xg @ g1) @ g2
    hd = lambda t: t.reshape(B, T, N_HEADS, HEAD_DIM).astype(f32)
    kk = hd(k * k_k)
    kk = kk / jnp.maximum(jnp.sqrt(jnp.sum(kk * kk, axis=-1, keepdims=True)), L2_EPS)
    k = k * (1 + (a - 1) * k_a)
    r_h, k_h, v_h, a_h = hd(r), hd(k), hd(v), hd(a)
    S, y = wkv7_scan(wkv_prev.astype(f32), r_h, hd(decay), k_h, v_h, -kk, kk * a_h)
    mu = jnp.mean(y, axis=-1, keepdims=True)
    var = jnp.mean(jnp.square(y - mu), axis=-1, keepdims=True)
    y = ((y - mu) * lax.rsqrt(var + GN_EPS)).reshape(B, T, D) * gn_w + gn_b
    bonus = jnp.sum(r_h * k_h * r_k, axis=-1, keepdims=True) * v_h
    y = (y + bonus.reshape(B, T, D)).astype(x.dtype)
    out = (y * g) @ w_o
    return out, x[:, -1], S.astype(wkv_prev.dtype), v_first


def pool_mix(x, buf, start_pos, w_pool, pool_scale):
    B, T, D = x.shape
    f32 = jnp.float32
    ext = jnp.concatenate([buf.astype(x.dtype), x], axis=1)
    cs = jnp.cumsum(ext.astype(f32), axis=1)
    cs = jnp.concatenate([jnp.zeros((B, 1, D), f32), cs], axis=1)
    pos = start_pos + jnp.arange(T)
    means = []
    for gi, w in enumerate(POOL_WINDOWS):
        sl = slice(gi * POOL_GROUP, (gi + 1) * POOL_GROUP)
        s = cs[:, POOL_BUF + 1:, sl] - cs[:, POOL_BUF + 1 - w:POOL_BUF + 1 - w + T, sl]
        cnt = jnp.minimum(pos + 1, w).astype(f32)
        means.append(s / cnt[None, :, None])
    pooled = jnp.concatenate(means, axis=-1)
    diff = (pooled - x.astype(f32)).astype(x.dtype).reshape(B, T, N_POOL_GROUPS, POOL_GROUP)
    y = jnp.einsum('btgc,gcd->btgd', diff, w_pool).reshape(B, T, D) * pool_scale
    return y, ext[:, -POOL_BUF:]


def trunk(x, start_pos, shift_states, wkv_states, pool_bufs, p):
    new_shift, new_wkv, new_pool = [], [], []
    v_first = None
    for i in range(DEPTH):
        h = rmsnorm(x, p['norm_mix'][i])
        j = i // N_MIXERS
        if i % N_MIXERS == 0:
            vres = None if j == 0 else (p['v0'][j - 1], p['v1'][j - 1], p['v2'][j - 1])
            out, s_shift, s_wkv, v_first = rwkv7_time_mix(
                h, shift_states[j], wkv_states[j], v_first, vres,
                p['mix'][j], p['w_rkv'][j], p['w_o'][j], p['w0'][j], p['w1'][j], p['w2'][j],
                p['a0'][j], p['a1'][j], p['a2'][j], p['g1'][j], p['g2'][j],
                p['k_k'][j], p['k_a'][j], p['r_k'][j], p['gn_w'][j], p['gn_b'][j])
            new_shift.append(s_shift)
            new_wkv.append(s_wkv)
        else:
            out, s_buf = pool_mix(h, pool_bufs[j], start_pos, p['pool_w'][j], p['pool_scale'][j])
            new_pool.append(s_buf)
        x = x + out
        h = rmsnorm(x, p['norm_ffn'][i])
        x = x + jnp.square(jax.nn.relu(h @ p['ffn_w1'][i])) @ p['ffn_w2'][i]
    y = rmsnorm(x, p['norm_final'])
    return y, jnp.stack(new_wkv), jnp.stack(new_shift), jnp.stack(new_pool)


def setup_inputs(seed: int = 0) -> dict:
    key = jax.random.key(seed)
    ks = jax.random.split(key, 32)
    nrm = lambda k, shape, s: jax.random.normal(k, shape, jnp.float32) * s
    D, H, K = D_MODEL, N_HEADS, HEAD_DIM
    nv = max(N_RWKV - 1, 0)
    return {
        'x_prompt': nrm(ks[0], (BATCH, SEQ, D), 1.0),
        'x_sample': nrm(ks[1], (DEC_BATCH, DEC_SEQ, D), 1.0),
        'state_wkv': nrm(ks[2], (N_RWKV, DEC_BATCH, H, K, K), 0.3),
        'state_shift': nrm(ks[3], (N_RWKV, DEC_BATCH, D), 1.0),
        'state_pool': nrm(ks[4], (N_POOL, DEC_BATCH, POOL_BUF, D), 1.0),
        'norm_mix': 1.0 + nrm(ks[5], (DEPTH, D), 0.05),
        'norm_ffn': 1.0 + nrm(ks[6], (DEPTH, D), 0.05),
        'norm_final': 1.0 + nrm(ks[7], (D,), 0.05),
        'rwkv_mix': jax.random.uniform(ks[8], (N_RWKV, 6, D), jnp.float32),
        'rwkv_w_rkv': nrm(ks[9], (N_RWKV, 3, D, D), D ** -0.5),
        'rwkv_w_o': nrm(ks[10], (N_RWKV, D, D), D ** -0.5),
        'rwkv_w0': jax.random.uniform(ks[11], (N_RWKV, D), jnp.float32, -6.0, -1.0),
        'rwkv_w1': nrm(ks[12], (N_RWKV, D, D_DECAY_LORA), D ** -0.5),
        'rwkv_w2': nrm(ks[13], (N_RWKV, D_DECAY_LORA, D), 0.5 * D_DECAY_LORA ** -0.5),
        'rwkv_a0': nrm(ks[14], (N_RWKV, D), 0.1),
        'rwkv_a1': nrm(ks[15], (N_RWKV, D, D_AAA_LORA), D ** -0.5),
        'rwkv_a2': nrm(ks[16], (N_RWKV, D_AAA_LORA, D), D_AAA_LORA ** -0.5),
        'rwkv_v0': nrm(ks[17], (nv, D), 0.5),
        'rwkv_v1': nrm(ks[18], (nv, D, D_MV_LORA), D ** -0.5),
        'rwkv_v2': nrm(ks[19], (nv, D_MV_LORA, D), D_MV_LORA ** -0.5),
        'rwkv_g1': nrm(ks[20], (N_RWKV, D, D_GATE_LORA), D ** -0.5),
        'rwkv_g2': nrm(ks[21], (N_RWKV, D_GATE_LORA, D), D_GATE_LORA ** -0.5),
        'rwkv_k_k': 0.85 + nrm(ks[22], (N_RWKV, D), 0.1),
        'rwkv_k_a': 1.0 + nrm(ks[23], (N_RWKV, D), 0.1),
        'rwkv_r_k': nrm(ks[24], (N_RWKV, H, K), 0.1),
        'rwkv_gn_w': 1.0 + nrm(ks[25], (N_RWKV, D), 0.05),
        'rwkv_gn_b': nrm(ks[26], (N_RWKV, D), 0.02),
        'pool_w': nrm(ks[27], (N_POOL, N_POOL_GROUPS, POOL_GROUP, POOL_GROUP), POOL_GROUP ** -0.5),
        'pool_scale': 1.0 + nrm(ks[28], (N_POOL, D), 0.1),
        'ffn_w1': nrm(ks[29], (DEPTH, D, D_FF), D ** -0.5),
        'ffn_w2': nrm(ks[30], (DEPTH, D_FF, D), 0.5 * D_FF ** -0.5),
    }


def reference(x_prompt, x_sample, state_wkv, state_shift, state_pool, norm_mix, norm_ffn, norm_final,
              rwkv_mix, rwkv_w_rkv, rwkv_w_o, rwkv_w0, rwkv_w1, rwkv_w2, rwkv_a0, rwkv_a1, rwkv_a2,
              rwkv_v0, rwkv_v1, rwkv_v2, rwkv_g1, rwkv_g2, rwkv_k_k, rwkv_k_a, rwkv_r_k,
              rwkv_gn_w, rwkv_gn_b, pool_w, pool_scale, ffn_w1, ffn_w2):
    p = {'norm_mix': norm_mix, 'norm_ffn': norm_ffn, 'norm_final': norm_final,
         'mix': rwkv_mix, 'w_rkv': rwkv_w_rkv, 'w_o': rwkv_w_o, 'w0': rwkv_w0, 'w1': rwkv_w1,
         'w2': rwkv_w2, 'a0': rwkv_a0, 'a1': rwkv_a1, 'a2': rwkv_a2, 'v0': rwkv_v0, 'v1': rwkv_v1,
         'v2': rwkv_v2, 'g1': rwkv_g1, 'g2': rwkv_g2, 'k_k': rwkv_k_k, 'k_a': rwkv_k_a,
         'r_k': rwkv_r_k, 'gn_w': rwkv_gn_w, 'gn_b': rwkv_gn_b, 'pool_w': pool_w,
         'pool_scale': pool_scale, 'ffn_w1': ffn_w1, 'ffn_w2': ffn_w2}
    dt = x_prompt.dtype
    z_shift = jnp.zeros((N_RWKV, BATCH, D_MODEL), dt)
    z_wkv = jnp.zeros((N_RWKV, BATCH, N_HEADS, HEAD_DIM, HEAD_DIM), dt)
    z_pool = jnp.zeros((N_POOL, BATCH, POOL_BUF, D_MODEL), dt)
    y_prompt, wkv_prompt, shift_prompt, pool_prompt = trunk(x_prompt, 0, z_shift, z_wkv, z_pool, p)
    y_sample, wkv_sample, shift_sample, pool_sample = trunk(
        x_sample, PAST_LEN, state_shift, state_wkv, state_pool, p)
    return (y_prompt, y_sample, wkv_prompt, shift_prompt, pool_prompt, wkv_sample, shift_sample, pool_sample)
```

```python
import functools
import math

import jax
import jax.numpy as jnp
from jax import lax
from jax.experimental import pallas as pl
from jax.experimental.pallas import tpu as pltpu

HEAD_DIM = 64
LANES = 128
SUBLANES = 8
POOL_WINDOWS = (2, 4, 8, 16)
POOL_BUF = max(POOL_WINDOWS) - 1
PAST_LEN = 16384
NORM_EPS = 1e-6
GN_EPS = 64e-5
L2_EPS = 1e-12
WKV_CHUNK = 16
VMEM_LIMIT = 56 * 1024 * 1024

BF16 = jnp.bfloat16
F32 = jnp.float32


def _dot(a, b):
    return jnp.dot(a, b, preferred_element_type=F32)


def _dot_nt(a, b):
    return lax.dot_general(a, b, (((1,), (1,)), ((), ())), preferred_element_type=F32)


def _sigmoid(x):
    return 1.0 / (1.0 + jnp.exp(-x))


def _rmsnorm(x, g):
    ms = jnp.mean(x * x, axis=-1, keepdims=True)
    return x * lax.rsqrt(ms + NORM_EPS) * g


def _head_ones():
    r = lax.broadcasted_iota(jnp.int32, (LANES, LANES), 0) // HEAD_DIM
    c = lax.broadcasted_iota(jnp.int32, (LANES, LANES), 1) // HEAD_DIM
    return jnp.where(r == c, 1.0, 0.0).astype(BF16)


def _head_sum_bf16(p, ones):
    return _dot(p.astype(BF16), ones)


def _head_sum(p, ones):
    hi = p.astype(BF16)
    lo = (p - hi.astype(F32)).astype(BF16)
    return _dot(hi, ones) + _dot(lo, ones)


def _const_spec(shape):
    n = len(shape)
    return pl.BlockSpec(shape, lambda *_: (0,) * n)


def _pre_kernel(has_vres, nb, *refs):
    it = iter(refs)
    x_ref = next(it); shift_ref = next(it); gmix_ref = next(it); mix_ref = next(it)
    wr_ref = next(it); wk_ref = next(it); wv_ref = next(it)
    w0_ref = next(it); w1_ref = next(it); w2_ref = next(it)
    a0_ref = next(it); a1_ref = next(it); a2_ref = next(it)
    if has_vres:
        v0_ref = next(it); v1_ref = next(it); v2_ref = next(it); vfirst_ref = next(it)
    g1_ref = next(it); g2_ref = next(it); kk_ref = next(it); ka_ref = next(it)
    r_out = next(it); lw_out = next(it); k_out = next(it); v_out = next(it)
    kk_out = next(it); b_out = next(it); g_out = next(it); shift_out = next(it)
    carry = next(it)

    tm, d = x_ref.shape
    npair = d // LANES

    @pl.when(pl.program_id(0) == 0)
    def _():
        carry[...] = shift_ref[...]

    h = _rmsnorm(x_ref[...], gmix_ref[...])
    if tm > nb:
        hp = jnp.concatenate([carry[...], h[: tm - nb]], axis=0)
    else:
        hp = carry[...]
    carry[...] = h[tm - nb:]
    shift_out[...] = h[tm - nb:]
    dx = hp - h

    def mixed(i):
        return (h + dx * mix_ref[i:i + 1, :]).astype(BF16)

    xv = mixed(2)
    r = _dot(mixed(0), wr_ref[...])
    k = _dot(mixed(1), wk_ref[...])
    v = _dot(xv, wv_ref[...])
    wpre = w0_ref[...] + _dot(jnp.tanh(_dot(mixed(3), w1_ref[...])).astype(BF16), w2_ref[...])
    lw = -math.exp(-0.5) * _sigmoid(wpre)
    if has_vres:
        vfirst = jnp.concatenate([vfirst_ref[p] for p in range(npair)], axis=1)
        gate = _sigmoid(v0_ref[...] + _dot(_dot(xv, v1_ref[...]).astype(BF16), v2_ref[...]))
        v = v + (vfirst - v) * gate
    a = _sigmoid(a0_ref[...] + _dot(_dot(mixed(4), a1_ref[...]).astype(BF16), a2_ref[...]))
    g = _dot(_sigmoid(_dot(mixed(5), g1_ref[...])).astype(BF16), g2_ref[...])
    g_out[...] = g

    ones = _head_ones()
    kk = k * kk_ref[...]
    k2 = k * (1.0 + (a - 1.0) * ka_ref[...])
    for p in range(npair):
        sl = slice(p * LANES, (p + 1) * LANES)
        kkp = kk[:, sl]
        nrm = jnp.sqrt(_head_sum(kkp * kkp, ones))
        kkp = kkp / jnp.maximum(nrm, L2_EPS)
        r_out[p] = r[:, sl]
        lw_out[p] = lw[:, sl]
        k_out[p] = k2[:, sl]
        v_out[p] = v[:, sl]
        kk_out[p] = kkp
        b_out[p] = kkp * a[:, sl]


def _pre_call(x, shift_in, gmix, mix, wr, wk, wv, w0, w1, w2, a0, a1, a2, vres, vfirst,
              g1, g2, k_k, k_a, nb, tm):
    rows, d = x.shape
    npair = d // LANES
    has_vres = vres is not None
    row = lambda a: a.reshape(1, -1)
    args = [x, shift_in, row(gmix), mix, wr, wk, wv, row(w0), w1, w2, row(a0), a1, a2]
    specs = [pl.BlockSpec((tm, d), lambda i: (i, 0))] + [_const_spec(a.shape) for a in args[1:]]
    if has_vres:
        v0, v1, v2 = vres
        extra = [row(v0), v1, v2]
        args += extra + [vfirst]
        specs += [_const_spec(a.shape) for a in extra]
        specs += [pl.BlockSpec((npair, tm, LANES), lambda i: (0, i, 0))]
    tail = [g1, g2, row(k_k), row(k_a)]
    args += tail
    specs += [_const_spec(a.shape) for a in tail]
    pm = jax.ShapeDtypeStruct((npair, rows, LANES), F32)
    pm_spec = pl.BlockSpec((npair, tm, LANES), lambda i: (0, i, 0))
    out_shape = [pm] * 6 + [jax.ShapeDtypeStruct((rows, d), F32), jax.ShapeDtypeStruct((nb, d), F32)]
    out_specs = [pm_spec] * 6 + [pl.BlockSpec((tm, d), lambda i: (i, 0)), _const_spec((nb, d))]
    return pl.pallas_call(
        functools.partial(_pre_kernel, has_vres, nb),
        grid=(rows // tm,),
        in_specs=specs,
        out_specs=out_specs,
        out_shape=out_shape,
        scratch_shapes=[pltpu.VMEM((nb, d), F32)],
        compiler_params=pltpu.CompilerParams(
            dimension_semantics=("arbitrary",), vmem_limit_bytes=VMEM_LIMIT),
        name="rwkv_pre",
    )(*args)


def _wkv_kernel(tchunk, nchunks, r_ref, lw_ref, k_ref, v_ref, kk_ref, b_ref, s0_ref,
                rk_ref, gnw_ref, gnb_ref, y_ref, sout_ref,
                s_scr, u_scr, y_scr):
    nseq = SUBLANES
    cp = max(tchunk, 8)
    n = cp * nseq
    nv = tchunk * nseq

    @pl.when(pl.program_id(2) == 0)
    def _():
        s_scr[...] = s0_ref[...]

    ones = _head_ones()
    rowid = lax.broadcasted_iota(jnp.int32, (2 * n, LANES), 0) % nseq
    colid = lax.broadcasted_iota(jnp.int32, (LANES, 2 * n), 1) % nseq
    hr = lax.broadcasted_iota(jnp.int32, (LANES, LANES), 0) // HEAD_DIM
    hc = lax.broadcasted_iota(jnp.int32, (LANES, LANES), 1) // HEAD_DIM
    same_head = hr == hc

    def load(ref, c):
        val = ref[pl.ds(c * tchunk, tchunk)].reshape(nv, LANES)
        if n > nv:
            val = jnp.concatenate([val, jnp.zeros((n - nv, LANES), F32)], axis=0)
        return val

    def tile_rows(slab, reps):
        return slab if reps == 1 else jnp.concatenate([slab] * reps, axis=0)

    def chunk(c, carry):
        r = load(r_ref, c); lw = load(lw_ref, c); k = load(k_ref, c)
        v = load(v_ref, c); kk = load(kk_ref, c); b = load(b_ref, c)

        acc = jnp.zeros((nseq, LANES), F32)
        cums = []
        for t in range(cp):
            acc = acc + lw[t * nseq:(t + 1) * nseq]
            cums.append(acc)
        cum = jnp.concatenate(cums, axis=0)
        w_inc = jnp.exp(cum)
        w_inv = jnp.exp(-cum)
        w_exc = jnp.concatenate([jnp.ones((nseq, LANES), F32), w_inc[: n - nseq]], axis=0)
        w_tot = w_inc[n - nseq:]
        at = -kk * w_exc
        rt = r * w_inc
        kt = k * w_inv
        bt = b * w_inv
        w_tot_rows = tile_rows(w_tot, cp)
        khat = kt * w_tot_rows
        bhat = bt * w_tot_rows

        lhs = jnp.concatenate([at, rt], axis=0).astype(BF16)
        xy0 = jnp.zeros((2 * n, LANES), F32)
        for s in range(nseq):
            st = s_scr[s]
            st_hi = st.astype(BF16)
            st_lo = (st - st_hi.astype(F32)).astype(BF16)
            lm = jnp.where(rowid == s, lhs, jnp.zeros_like(lhs))
            xy0 = xy0 + _dot_nt(lm, st_hi) + _dot_nt(lm, st_lo)
        x0 = xy0[:n]
        y0 = xy0[n:]

        for t in range(cp):
            lo, hi = t * nseq, (t + 1) * nseq
            a_t = at[lo:hi]
            r_t = rt[lo:hi]
            parts = []
            if t > 0:
                parts += [tile_rows(a_t, t) * kt[:lo], tile_rows(a_t, t) * bt[:lo]]
            parts += [tile_rows(r_t, t + 1) * kt[:hi], tile_rows(r_t, t + 1) * bt[:hi]]
            coef = _head_sum_bf16(jnp.concatenate(parts, axis=0), ones)

            def red(cf, vals, m):
                prod = cf * vals
                out = prod[:nseq]
                for j in range(1, m):
                    out = out + prod[j * nseq:(j + 1) * nseq]
                return out

            u_t = x0[lo:hi]
            off = 0
            if t > 0:
                u_t = u_t + red(coef[:lo], v[:lo], t)
                u_t = u_t + red(coef[lo:2 * lo], u_scr[:lo], t)
                off = 2 * lo
            u_scr[lo:hi] = u_t
            y_t = y0[lo:hi] + red(coef[off:off + hi], v[:hi], t + 1)
            y_t = y_t + red(coef[off + hi:off + 2 * hi], u_scr[:hi], t + 1)
            y_scr[lo:hi] = y_t

        z_t = jnp.concatenate([u_scr[...], v], axis=0).T.astype(BF16)
        gk = jnp.concatenate([bhat, khat], axis=0).astype(BF16)
        for s in range(nseq):
            zm = jnp.where(colid == s, z_t, jnp.zeros_like(z_t))
            delta = _dot(zm, gk)
            s_scr[s] = s_scr[s] * w_tot[s:s + 1, :] + jnp.where(same_head, delta, 0.0)

        y = y_scr[:nv]
        rv, kv, vv = r[:nv], k[:nv], v[:nv]
        mu = _head_sum(y, ones) * (1.0 / HEAD_DIM)
        dlt = y - mu
        var = _head_sum(dlt * dlt, ones) * (1.0 / HEAD_DIM)
        yn = dlt * lax.rsqrt(var + GN_EPS) * gnw_ref[...] + gnb_ref[...]
        bonus = _head_sum(rv * kv * rk_ref[...], ones) * vv
        y_ref[pl.ds(c * tchunk, tchunk)] = (yn + bonus).reshape(tchunk, nseq, LANES)
        return carry

    lax.fori_loop(0, nchunks, chunk, 0)

    @pl.when(pl.program_id(2) == pl.num_programs(2) - 1)
    def _():
        sout_ref[...] = s_scr[...]


def _wkv_call(r, lw, k, v, kk, b, s0, r_k, gn_w, gn_b, nb, nt):
    npair = r.shape[0]
    ngroup = nb // SUBLANES
    tchunk = min(WKV_CHUNK, nt)
    tb = min(256, nt)
    nchunks = tb // tchunk
    assert nt % tb == 0 and tb % tchunk == 0 and nb % SUBLANES == 0
    view = lambda a: a.reshape(npair, nt, ngroup, SUBLANES, LANES)
    act_spec = pl.BlockSpec((None, tb, None, SUBLANES, LANES), lambda g, p, t: (p, t, g, 0, 0))
    st_spec = pl.BlockSpec((SUBLANES, None, LANES, LANES), lambda g, p, t: (g, p, 0, 0))
    par_spec = pl.BlockSpec((None, 1, LANES), lambda g, p, t: (p, 0, 0))
    par = lambda a: a.reshape(npair, 1, LANES)
    cp = max(tchunk, 8)
    y, s_out = pl.pallas_call(
        functools.partial(_wkv_kernel, tchunk, nchunks),
        grid=(ngroup, npair, nt // tb),
        in_specs=[act_spec] * 6 + [st_spec] + [par_spec] * 3,
        out_specs=[act_spec, st_spec],
        out_shape=[jax.ShapeDtypeStruct((npair, nt, ngroup, SUBLANES, LANES), F32),
                   jax.ShapeDtypeStruct(s0.shape, F32)],
        scratch_shapes=[pltpu.VMEM((SUBLANES, LANES, LANES), F32),
                        pltpu.VMEM((cp * SUBLANES, LANES), F32),
                        pltpu.VMEM((cp * SUBLANES, LANES), F32)],
        compiler_params=pltpu.CompilerParams(
            dimension_semantics=("arbitrary", "arbitrary", "arbitrary"),
            vmem_limit_bytes=VMEM_LIMIT),
        name="wkv7",
    )(view(r), view(lw), view(k), view(v), view(kk), view(b), s0, par(r_k), par(gn_w), par(gn_b))
    return y.reshape(npair, nt * nb, LANES), s_out


def _ffn(x, gffn, w1_ref, w2_ref):
    h = _rmsnorm(x, gffn).astype(BF16)
    dff = w1_ref.shape[1]
    step = min(dff, 1024)
    acc = x
    for c in range(dff // step):
        hh = _dot(h, w1_ref[:, c * step:(c + 1) * step])
        hh = jnp.square(jnp.maximum(hh, 0.0)).astype(BF16)
        acc = acc + _dot(hh, w2_ref[c * step:(c + 1) * step, :])
    return acc


def _attn_ffn_kernel(final_norm, x_ref, y_ref, g_ref, wo_ref, gffn_ref, w1_ref, w2_ref, gfin_ref, o_ref):
    npair = y_ref.shape[0]
    y = jnp.concatenate([y_ref[p] for p in range(npair)], axis=1)
    x = x_ref[...] + _dot((y * g_ref[...]).astype(BF16), wo_ref[...])
    out = _ffn(x, gffn_ref[...], w1_ref, w2_ref)
    if final_norm:
        out = _rmsnorm(out, gfin_ref[...])
    o_ref[...] = out


def _attn_ffn_call(x, y, g, wo, gffn, w1, w2, gfin, final_norm, tm):
    rows, d = x.shape
    npair = d // LANES
    row = lambda a: a.reshape(1, -1)
    x_spec = pl.BlockSpec((tm, d), lambda i: (i, 0))
    return pl.pallas_call(
        functools.partial(_attn_ffn_kernel, final_norm),
        grid=(rows // tm,),
        in_specs=[x_spec, pl.BlockSpec((npair, tm, LANES), lambda i: (0, i, 0)), x_spec,
                  _const_spec(wo.shape), _const_spec((1, d)), _const_spec(w1.shape),
                  _const_spec(w2.shape), _const_spec((1, d))],
        out_specs=x_spec,
        out_shape=jax.ShapeDtypeStruct((rows, d), F32),
        compiler_params=pltpu.CompilerParams(
            dimension_semantics=("arbitrary",), vmem_limit_bytes=VMEM_LIMIT),
        name="attn_out_ffn",
    )(x, y, g, wo, row(gffn), w1, w2, row(gfin))


def _pool_ffn_kernel(final_norm, nb, start_pos, x_ref, buf_ref, gmix_ref, pw_ref, ps_ref,
                     gffn_ref, w1_ref, w2_ref, gfin_ref, o_ref, buf_out, carry):
    tm, d = x_ref.shape
    ngrp = len(POOL_WINDOWS)
    gw = d // ngrp
    nbuf = POOL_BUF * nb

    @pl.when(pl.program_id(0) == 0)
    def _():
        carry[...] = buf_ref[...]

    x = x_ref[...]
    h = _rmsnorm(x, gmix_ref[...])
    ext = jnp.concatenate([carry[...], h], axis=0)
    carry[...] = ext[tm:]
    buf_out[...] = ext[tm:]

    rowi = lax.broadcasted_iota(jnp.int32, (tm, gw), 0) + pl.program_id(0) * tm
    pos = start_pos + jnp.right_shift(rowi, int(math.log2(nb)))
    ys = []
    for gi, w in enumerate(POOL_WINDOWS):
        sl = slice(gi * gw, (gi + 1) * gw)
        s = ext[:, sl]
        span = 1
        while span < w:
            s = s[span * nb:] + s[: s.shape[0] - span * nb]
            span *= 2
        s = s[s.shape[0] - tm:]
        cnt = jnp.minimum(pos + 1, w).astype(F32)
        diff = (s / cnt - h[:, sl]).astype(BF16)
        ys.append(_dot(diff, pw_ref[gi]))
    x = x + jnp.concatenate(ys, axis=1) * ps_ref[...]
    out = _ffn(x, gffn_ref[...], w1_ref, w2_ref)
    if final_norm:
        out = _rmsnorm(out, gfin_ref[...])
    o_ref[...] = out


def _pool_ffn_call(x, buf, gmix, pw, ps, gffn, w1, w2, gfin, final_norm, nb, start_pos, tm):
    rows, d = x.shape
    assert nb & (nb - 1) == 0
    row = lambda a: a.reshape(1, -1)
    x_spec = pl.BlockSpec((tm, d), lambda i: (i, 0))
    return pl.pallas_call(
        functools.partial(_pool_ffn_kernel, final_norm, nb, start_pos),
        grid=(rows // tm,),
        in_specs=[x_spec, _const_spec(buf.shape), _const_spec((1, d)), _const_spec(pw.shape),
                  _const_spec((1, d)), _const_spec((1, d)), _const_spec(w1.shape),
                  _const_spec(w2.shape), _const_spec((1, d))],
        out_specs=[x_spec, _const_spec(buf.shape)],
        out_shape=[jax.ShapeDtypeStruct((rows, d), F32), jax.ShapeDtypeStruct(buf.shape, F32)],
        scratch_shapes=[pltpu.VMEM(buf.shape, F32)],
        compiler_params=pltpu.CompilerParams(
            dimension_semantics=("arbitrary",), vmem_limit_bytes=VMEM_LIMIT),
        name="pool_ffn",
    )(x, buf, row(gmix), pw, row(ps), row(gffn), w1, w2, row(gfin))


def _to_blockdiag(s):
    nb, nh = s.shape[:2]
    s = s.reshape(nb, nh // 2, 2, HEAD_DIM, HEAD_DIM)
    z = jnp.zeros_like(s[:, :, 0])
    top = jnp.concatenate([s[:, :, 0], z], axis=-1)
    bot = jnp.concatenate([z, s[:, :, 1]], axis=-1)
    return jnp.concatenate([top, bot], axis=-2)


def _from_blockdiag(s):
    nb, npair = s.shape[:2]
    a = s[:, :, :HEAD_DIM, :HEAD_DIM]
    b = s[:, :, HEAD_DIM:, HEAD_DIM:]
    return jnp.stack([a, b], axis=2).reshape(nb, 2 * npair, HEAD_DIM, HEAD_DIM)


def _trunk(x, start_pos, shift_states, wkv_states, pool_bufs, p):
    nb, nt, d = x.shape
    depth = p['norm_mix'].shape[0]
    rows = nb * nt
    tm = min(512, rows)
    assert rows % tm == 0 and tm % nb == 0
    xt = x.transpose(1, 0, 2).reshape(rows, d)
    new_shift, new_wkv, new_pool = [], [], []
    v_first = None
    for i in range(depth):
        j = i // 2
        last = i == depth - 1
        if i % 2 == 0:
            vres = None if j == 0 else (p['v0'][j - 1], p['v1'][j - 1], p['v2'][j - 1])
            r, lw, k, v, kk, b, g, s_shift = _pre_call(
                xt, shift_states[j], p['norm_mix'][i], p['mix'][j],
                p['w_rkv'][j, 0], p['w_rkv'][j, 1], p['w_rkv'][j, 2],
                p['w0'][j], p['w1'][j], p['w2'][j], p['a0'][j], p['a1'][j], p['a2'][j],
                vres, v_first, p['g1'][j], p['g2'][j], p['k_k'][j], p['k_a'][j], nb, tm)
            if j == 0:
                v_first = v
            y, s_wkv = _wkv_call(r, lw, k, v, kk, b, _to_blockdiag(wkv_states[j]),
                                 p['r_k'][j].reshape(-1), p['gn_w'][j], p['gn_b'][j], nb, nt)
            new_shift.append(s_shift)
            new_wkv.append(_from_blockdiag(s_wkv))
            xt = _attn_ffn_call(xt, y, g, p['w_o'][j], p['norm_ffn'][i], p['ffn_w1'][i],
                                p['ffn_w2'][i], p['norm_final'], last, tm)
        else:
            buf = pool_bufs[j].transpose(1, 0, 2).reshape(POOL_BUF * nb, d)
            xt, s_buf = _pool_ffn_call(xt, buf, p['norm_mix'][i], p['pool_w'][j], p['pool_scale'][j],
                                       p['norm_ffn'][i], p['ffn_w1'][i], p['ffn_w2'][i],
                                       p['norm_final'], last, nb, start_pos, tm)
            new_pool.append(s_buf.reshape(POOL_BUF, nb, d).transpose(1, 0, 2))
    y = xt.reshape(nt, nb, d).transpose(1, 0, 2)
    return y, jnp.stack(new_wkv), jnp.stack(new_shift), jnp.stack(new_pool)


def kernel(x_prompt, x_sample, state_wkv, state_shift, state_pool, norm_mix, norm_ffn, norm_final,
           rwkv_mix, rwkv_w_rkv, rwkv_w_o, rwkv_w0, rwkv_w1, rwkv_w2, rwkv_a0, rwkv_a1, rwkv_a2,
           rwkv_v0, rwkv_v1, rwkv_v2, rwkv_g1, rwkv_g2, rwkv_k_k, rwkv_k_a, rwkv_r_k,
           rwkv_gn_w, rwkv_gn_b, pool_w, pool_scale, ffn_w1, ffn_w2):
    bf = lambda a: a.astype(BF16)
    p = {'norm_mix': norm_mix, 'norm_ffn': norm_ffn, 'norm_final': norm_final,
         'mix': rwkv_mix, 'w_rkv': bf(rwkv_w_rkv), 'w_o': bf(rwkv_w_o), 'w0': rwkv_w0,
         'w1': bf(rwkv_w1), 'w2': bf(rwkv_w2), 'a0': rwkv_a0, 'a1': bf(rwkv_a1),
         'a2': bf(rwkv_a2), 'v0': rwkv_v0, 'v1': bf(rwkv_v1), 'v2': bf(rwkv_v2),
         'g1': bf(rwkv_g1), 'g2': bf(rwkv_g2), 'k_k': rwkv_k_k, 'k_a': rwkv_k_a,
         'r_k': rwkv_r_k, 'gn_w': rwkv_gn_w, 'gn_b': rwkv_gn_b, 'pool_w': bf(pool_w),
         'pool_scale': pool_scale, 'ffn_w1': bf(ffn_w1), 'ffn_w2': bf(ffn_w2)}
    dt = x_prompt.dtype
    nb, _, d = x_prompt.shape
    n_rwkv = state_wkv.shape[0]
    n_pool = state_pool.shape[0]
    nh = d // HEAD_DIM
    z_shift = jnp.zeros((n_rwkv, nb, d), dt)
    z_wkv = jnp.zeros((n_rwkv, nb, nh, HEAD_DIM, HEAD_DIM), dt)
    z_pool = jnp.zeros((n_pool, nb, POOL_BUF, d), dt)
    y_p, wkv_p, shift_p, pool_p = _trunk(x_prompt, 0, z_shift, z_wkv, z_pool, p)
    y_s, wkv_s, shift_s, pool_s = _trunk(x_sample, PAST_LEN, state_shift, state_wkv, state_pool, p)
    return (y_p, y_s, wkv_p, shift_p, pool_p, wkv_s, shift_s, pool_s)
```

```python
import functools
import math

import jax
import jax.numpy as jnp
from jax import lax
from jax.experimental import pallas as pl
from jax.experimental.pallas import tpu as pltpu

HEAD_DIM = 64
LANES = 128
SUBLANES = 8
WKV_LANES = 256
POOL_WINDOWS = (2, 4, 8, 16)
POOL_BUF = max(POOL_WINDOWS) - 1
PAST_LEN = 16384
NORM_EPS = 1e-6
GN_EPS = 64e-5
L2_EPS = 1e-12
WKV_CHUNK = 16
VMEM_LIMIT = 56 * 1024 * 1024

BF16 = jnp.bfloat16
F32 = jnp.float32


def _dot(a, b):
    return jnp.dot(a, b, preferred_element_type=F32)


def _dot_nt(a, b):
    return lax.dot_general(a, b, (((1,), (1,)), ((), ())), preferred_element_type=F32)


def _sigmoid(x):
    return 1.0 / (1.0 + jnp.exp(-x))


def _rmsnorm(x, g):
    ms = jnp.mean(x * x, axis=-1, keepdims=True)
    return x * lax.rsqrt(ms + NORM_EPS) * g


def _head_ones():
    r = lax.broadcasted_iota(jnp.int32, (LANES, LANES), 0) // HEAD_DIM
    c = lax.broadcasted_iota(jnp.int32, (LANES, LANES), 1) // HEAD_DIM
    return jnp.where(r == c, 1.0, 0.0).astype(BF16)


def _head_sum_bf16(p, ones):
    return _dot(p.astype(BF16), ones)


def _head_sum(p, ones):
    hi = p.astype(BF16)
    lo = (p - hi.astype(F32)).astype(BF16)
    return _dot(hi, ones) + _dot(lo, ones)


def _const_spec(shape):
    n = len(shape)
    return pl.BlockSpec(shape, lambda *_: (0,) * n)


def _slab_lanes(d):
    return WKV_LANES if d % WKV_LANES == 0 else LANES


def _pre_kernel(has_vres, nb, *refs):
    it = iter(refs)
    x_ref = next(it); shift_ref = next(it); gmix_ref = next(it); mix_ref = next(it)
    wr_ref = next(it); wk_ref = next(it); wv_ref = next(it)
    w0_ref = next(it); w1_ref = next(it); w2_ref = next(it)
    a0_ref = next(it); a1_ref = next(it); a2_ref = next(it)
    if has_vres:
        v0_ref = next(it); v1_ref = next(it); v2_ref = next(it); vfirst_ref = next(it)
    g1_ref = next(it); g2_ref = next(it); kk_ref = next(it); ka_ref = next(it)
    r_out = next(it); lw_out = next(it); k_out = next(it); v_out = next(it)
    kk_out = next(it); b_out = next(it); g_out = next(it); shift_out = next(it)
    carry = next(it)

    tm, d = x_ref.shape
    nslab, _, slab = r_out.shape

    @pl.when(pl.program_id(0) == 0)
    def _():
        carry[...] = shift_ref[...]

    h = _rmsnorm(x_ref[...], gmix_ref[...])
    if tm > nb:
        hp = jnp.concatenate([carry[...], h[: tm - nb]], axis=0)
    else:
        hp = carry[...]
    carry[...] = h[tm - nb:]
    shift_out[...] = h[tm - nb:]
    dx = hp - h

    def mixed(i):
        return (h + dx * mix_ref[i:i + 1, :]).astype(BF16)

    xv = mixed(2)
    r = _dot(mixed(0), wr_ref[...])
    k = _dot(mixed(1), wk_ref[...])
    v = _dot(xv, wv_ref[...])
    wpre = w0_ref[...] + _dot(jnp.tanh(_dot(mixed(3), w1_ref[...])).astype(BF16), w2_ref[...])
    lw = -math.exp(-0.5) * _sigmoid(wpre)
    if has_vres:
        vfirst = jnp.concatenate([vfirst_ref[p] for p in range(nslab)], axis=1)
        gate = _sigmoid(v0_ref[...] + _dot(_dot(xv, v1_ref[...]).astype(BF16), v2_ref[...]))
        v = v + (vfirst - v) * gate
    a = _sigmoid(a0_ref[...] + _dot(_dot(mixed(4), a1_ref[...]).astype(BF16), a2_ref[...]))
    g = _dot(_sigmoid(_dot(mixed(5), g1_ref[...])).astype(BF16), g2_ref[...])
    g_out[...] = g

    ones = _head_ones()
    kk = k * kk_ref[...]
    k2 = k * (1.0 + (a - 1.0) * ka_ref[...])
    kkn = []
    for p in range(d // LANES):
        kkp = kk[:, p * LANES:(p + 1) * LANES]
        nrm = jnp.sqrt(_head_sum(kkp * kkp, ones))
        kkn.append(kkp / jnp.maximum(nrm, L2_EPS))
    kkn = jnp.concatenate(kkn, axis=1)
    bb = kkn * a
    for p in range(nslab):
        sl = slice(p * slab, (p + 1) * slab)
        r_out[p] = r[:, sl]
        lw_out[p] = lw[:, sl]
        k_out[p] = k2[:, sl]
        v_out[p] = v[:, sl]
        kk_out[p] = kkn[:, sl]
        b_out[p] = bb[:, sl]


def _pre_call(x, shift_in, gmix, mix, wr, wk, wv, w0, w1, w2, a0, a1, a2, vres, vfirst,
              g1, g2, k_k, k_a, nb, tm):
    rows, d = x.shape
    slab = _slab_lanes(d)
    nslab = d // slab
    has_vres = vres is not None
    row = lambda a: a.reshape(1, -1)
    args = [x, shift_in, row(gmix), mix, wr, wk, wv, row(w0), w1, w2, row(a0), a1, a2]
    specs = [pl.BlockSpec((tm, d), lambda i: (i, 0))] + [_const_spec(a.shape) for a in args[1:]]
    sm_spec = pl.BlockSpec((nslab, tm, slab), lambda i: (0, i, 0))
    if has_vres:
        v0, v1, v2 = vres
        extra = [row(v0), v1, v2]
        args += extra + [vfirst]
        specs += [_const_spec(a.shape) for a in extra] + [sm_spec]
    tail = [g1, g2, row(k_k), row(k_a)]
    args += tail
    specs += [_const_spec(a.shape) for a in tail]
    sm = jax.ShapeDtypeStruct((nslab, rows, slab), F32)
    out_shape = [sm] * 6 + [jax.ShapeDtypeStruct((rows, d), F32), jax.ShapeDtypeStruct((nb, d), F32)]
    out_specs = [sm_spec] * 6 + [pl.BlockSpec((tm, d), lambda i: (i, 0)), _const_spec((nb, d))]
    return pl.pallas_call(
        functools.partial(_pre_kernel, has_vres, nb),
        grid=(rows // tm,),
        in_specs=specs,
        out_specs=out_specs,
        out_shape=out_shape,
        scratch_shapes=[pltpu.VMEM((nb, d), F32)],
        compiler_params=pltpu.CompilerParams(
            dimension_semantics=("arbitrary",), vmem_limit_bytes=VMEM_LIMIT),
        name="rwkv_pre",
    )(*args)


_WKV_SLOTS = ("at", "rt", "u", "v", "bh", "kh", "x0", "y0")


def _wkv_kernel(tchunk, nchunks, r_ref, lw_ref, k_ref, v_ref, kk_ref, b_ref, s0_ref,
                rk_ref, gnw_ref, gnb_ref, y_ref, sout_ref, s_scr, *work):
    nseq = SUBLANES
    npl = s_scr.shape[1]
    cp = max(tchunk, 8)
    n = cp * nseq
    nv = tchunk * nseq
    grp = LANES // (2 * cp)
    nslot = len(_WKV_SLOTS)
    scr = [dict(zip(_WKV_SLOTS, work[q * nslot:(q + 1) * nslot])) for q in range(npl)]

    @pl.when(pl.program_id(2) == 0)
    def _():
        s_scr[...] = s0_ref[...]

    ones = _head_ones()
    hr = lax.broadcasted_iota(jnp.int32, (LANES, LANES), 0) // HEAD_DIM
    hc = lax.broadcasted_iota(jnp.int32, (LANES, LANES), 1) // HEAD_DIM
    same_head = hr == hc
    lane_head = lax.broadcasted_iota(jnp.int32, (1, LANES), 1) // HEAD_DIM
    head_f32 = [jnp.where(lane_head == hh, 1.0, 0.0) for hh in range(2)]
    head_bf16 = [m.astype(BF16) for m in head_f32]
    ri = lax.broadcasted_iota(jnp.int32, (n, 2 * n), 0)
    ci = lax.broadcasted_iota(jnp.int32, (n, 2 * n), 1) % n
    same_seq = (ri % nseq) == (ci % nseq)
    tri_incl = jnp.where(same_seq & (ci // nseq <= ri // nseq), 1.0, 0.0)
    tri_strict = jnp.where(same_seq & (ci // nseq < ri // nseq), 1.0, 0.0)[:, :n]

    def load(ref, c, q):
        val = ref[pl.ds(c * tchunk, tchunk), :, q * LANES:(q + 1) * LANES].reshape(nv, LANES)
        if n > nv:
            val = jnp.concatenate([val, jnp.zeros((n - nv, LANES), F32)], axis=0)
        return val

    def tile_rows(slab, reps):
        return slab if reps == 1 else jnp.concatenate([slab] * reps, axis=0)

    def seq_rows(ref, s):
        return ref[pl.ds(s, cp, stride=nseq), :]

    def chunk_pair(c, q):
        w = scr[q]
        r = load(r_ref, c, q); lw = load(lw_ref, c, q); k = load(k_ref, c, q)
        v = load(v_ref, c, q); kk = load(kk_ref, c, q); b = load(b_ref, c, q)

        acc = jnp.zeros((nseq, LANES), F32)
        cums = []
        for t in range(cp):
            acc = acc + lw[t * nseq:(t + 1) * nseq]
            cums.append(acc)
        cum = jnp.concatenate(cums, axis=0)
        w_inc = jnp.exp(cum)
        w_inv = jnp.exp(-cum)
        w_exc = jnp.concatenate([jnp.ones((nseq, LANES), F32), w_inc[: n - nseq]], axis=0)
        w_tot = w_inc[n - nseq:]
        at = -kk * w_exc
        rt = r * w_inc
        kt = k * w_inv
        bt = b * w_inv
        w_tot_rows = tile_rows(w_tot, cp)
        w["at"][...] = at
        w["rt"][...] = rt
        w["v"][...] = v
        w["bh"][...] = bt * w_tot_rows
        w["kh"][...] = kt * w_tot_rows

        for s in range(nseq):
            lhs = jnp.concatenate([seq_rows(w["at"], s), seq_rows(w["rt"], s)], axis=0).astype(BF16)
            out = _dot_nt(lhs, s_scr[s, q].astype(BF16))
            w["x0"][pl.ds(s, cp, stride=nseq), :] = out[:cp]
            w["y0"][pl.ds(s, cp, stride=nseq), :] = out[cp:]

        at16 = at.astype(BF16)
        rt16 = rt.astype(BF16)
        kb16 = jnp.concatenate([kt.astype(BF16), bt.astype(BF16)], axis=0)
        lhs4 = jnp.concatenate([at16 * head_bf16[0], at16 * head_bf16[1],
                                rt16 * head_bf16[0], rt16 * head_bf16[1]], axis=0)
        gram = _dot_nt(lhs4, kb16)

        ga = jnp.concatenate([gram[0:n, :n] * tri_strict, gram[n:2 * n, :n] * tri_strict], axis=1)
        vm = jnp.concatenate([v * head_f32[0], v * head_f32[1]], axis=0)
        x = w["x0"][...] + _dot(ga.astype(BF16), vm.astype(BF16))

        prods = [tile_rows(at[t * nseq:(t + 1) * nseq], t) * bt[:t * nseq] for t in range(1, cp)]
        coef = _head_sum_bf16(jnp.concatenate(prods, axis=0), ones)
        us = [x[0:nseq]]
        off = 0
        for t in range(1, cp):
            u_t = x[t * nseq:(t + 1) * nseq]
            for j in range(t):
                u_t = u_t + coef[off + j * nseq:off + (j + 1) * nseq] * us[j]
            off += t * nseq
            us.append(u_t)
        u = jnp.concatenate(us, axis=0)
        w["u"][...] = u

        gr = jnp.concatenate([gram[2 * n:3 * n] * tri_incl, gram[3 * n:] * tri_incl], axis=1)
        um = jnp.concatenate([u * head_f32[0], u * head_f32[1]], axis=0)
        vum = jnp.concatenate([vm[:n], um[:n], vm[n:], um[n:]], axis=0)
        y = w["y0"][...] + _dot(gr.astype(BF16), vum.astype(BF16))

        zp = []
        for s in range(nseq):
            zp += [seq_rows(w["u"], s), seq_rows(w["v"], s)]
        z_t = jnp.concatenate(zp, axis=0).T.astype(BF16)
        for g0 in range(0, nseq, grp):
            cols = []
            for e in range(grp):
                gs = jnp.concatenate([seq_rows(w["bh"], g0 + e), seq_rows(w["kh"], g0 + e)],
                                     axis=0).astype(BF16)
                blk = [gs]
                if e > 0:
                    blk = [jnp.zeros((2 * cp * e, LANES), BF16)] + blk
                if e < grp - 1:
                    blk = blk + [jnp.zeros((2 * cp * (grp - 1 - e), LANES), BF16)]
                cols.append(jnp.concatenate(blk, axis=0))
            lhs = z_t[:, (g0 // grp) * LANES:(g0 // grp + 1) * LANES]
            delta = _dot(lhs, jnp.concatenate(cols, axis=1))
            for e in range(grp):
                s = g0 + e
                dl = delta[:, e * LANES:(e + 1) * LANES]
                s_scr[s, q] = s_scr[s, q] * w_tot[s:s + 1, :] + jnp.where(same_head, dl, 0.0)

        ql = slice(q * LANES, (q + 1) * LANES)
        yv, rv, kv, vv = y[:nv], r[:nv], k[:nv], v[:nv]
        ysq = yv * yv
        y_hi = yv.astype(BF16)
        q_hi = ysq.astype(BF16)
        stack = jnp.concatenate([y_hi, (yv - y_hi.astype(F32)).astype(BF16),
                                 q_hi, (ysq - q_hi.astype(F32)).astype(BF16),
                                 (rv * kv * rk_ref[:, ql]).astype(BF16)], axis=0)
        hs = _dot(stack, ones) * (1.0 / HEAD_DIM)
        mu = hs[:nv] + hs[nv:2 * nv]
        var = hs[2 * nv:3 * nv] + hs[3 * nv:4 * nv] - mu * mu
        yn = (yv - mu) * lax.rsqrt(var + GN_EPS) * gnw_ref[:, ql] + gnb_ref[:, ql]
        bonus = hs[4 * nv:] * float(HEAD_DIM) * vv
        y_ref[pl.ds(c * tchunk, tchunk), :, ql] = (yn + bonus).reshape(tchunk, nseq, LANES)

    def chunk(c, carry):
        for q in range(npl):
            chunk_pair(c, q)
        return carry

    lax.fori_loop(0, nchunks, chunk, 0)

    @pl.when(pl.program_id(2) == pl.num_programs(2) - 1)
    def _():
        sout_ref[...] = s_scr[...]


def _wkv_call(r, lw, k, v, kk, b, s0, r_k, gn_w, gn_b, nb, nt):
    nslab, _, slab = r.shape
    npl = slab // LANES
    ngroup = nb // SUBLANES
    tchunk = min(WKV_CHUNK, nt)
    tb = min(256, nt)
    nchunks = tb // tchunk
    assert nt % tb == 0 and tb % tchunk == 0 and nb % SUBLANES == 0
    view = lambda a: a.reshape(nslab, nt, ngroup, SUBLANES, slab)
    act_spec = pl.BlockSpec((None, tb, None, SUBLANES, slab), lambda g, p, t: (p, t, g, 0, 0))
    st_spec = pl.BlockSpec((SUBLANES, npl, LANES, LANES), lambda g, p, t: (g, p, 0, 0))
    par_spec = pl.BlockSpec((None, 1, slab), lambda g, p, t: (p, 0, 0))
    par = lambda a: a.reshape(nslab, 1, slab)
    n = max(tchunk, 8) * SUBLANES
    y, s_out = pl.pallas_call(
        functools.partial(_wkv_kernel, tchunk, nchunks),
        grid=(ngroup, nslab, nt // tb),
        in_specs=[act_spec] * 6 + [st_spec] + [par_spec] * 3,
        out_specs=[act_spec, st_spec],
        out_shape=[jax.ShapeDtypeStruct((nslab, nt, ngroup, SUBLANES, slab), F32),
                   jax.ShapeDtypeStruct(s0.shape, F32)],
        scratch_shapes=[pltpu.VMEM((SUBLANES, npl, LANES, LANES), F32)]
        + [pltpu.VMEM((n, LANES), F32)] * (len(_WKV_SLOTS) * npl),
        compiler_params=pltpu.CompilerParams(
            dimension_semantics=("arbitrary", "arbitrary", "arbitrary"),
            vmem_limit_bytes=VMEM_LIMIT),
        name="wkv7",
    )(view(r), view(lw), view(k), view(v), view(kk), view(b), s0, par(r_k), par(gn_w), par(gn_b))
    return y.reshape(nslab, nt * nb, slab), s_out


def _ffn(x, gffn, w1_ref, w2_ref):
    h = _rmsnorm(x, gffn).astype(BF16)
    dff = w1_ref.shape[1]
    step = min(dff, 1024)
    acc = x
    for c in range(dff // step):
        hh = _dot(h, w1_ref[:, c * step:(c + 1) * step])
        hh = jnp.square(jnp.maximum(hh, 0.0)).astype(BF16)
        acc = acc + _dot(hh, w2_ref[c * step:(c + 1) * step, :])
    return acc


def _attn_ffn_kernel(final_norm, x_ref, y_ref, g_ref, wo_ref, gffn_ref, w1_ref, w2_ref, gfin_ref, o_ref):
    nslab = y_ref.shape[0]
    y = jnp.concatenate([y_ref[p] for p in range(nslab)], axis=1)
    x = x_ref[...] + _dot((y * g_ref[...]).astype(BF16), wo_ref[...])
    out = _ffn(x, gffn_ref[...], w1_ref, w2_ref)
    if final_norm:
        out = _rmsnorm(out, gfin_ref[...])
    o_ref[...] = out


def _attn_ffn_call(x, y, g, wo, gffn, w1, w2, gfin, final_norm, tm):
    rows, d = x.shape
    nslab, _, slab = y.shape
    row = lambda a: a.reshape(1, -1)
    x_spec = pl.BlockSpec((tm, d), lambda i: (i, 0))
    return pl.pallas_call(
        functools.partial(_attn_ffn_kernel, final_norm),
        grid=(rows // tm,),
        in_specs=[x_spec, pl.BlockSpec((nslab, tm, slab), lambda i: (0, i, 0)), x_spec,
                  _const_spec(wo.shape), _const_spec((1, d)), _const_spec(w1.shape),
                  _const_spec(w2.shape), _const_spec((1, d))],
        out_specs=x_spec,
        out_shape=jax.ShapeDtypeStruct((rows, d), F32),
        compiler_params=pltpu.CompilerParams(
            dimension_semantics=("arbitrary",), vmem_limit_bytes=VMEM_LIMIT),
        name="attn_out_ffn",
    )(x, y, g, wo, row(gffn), w1, w2, row(gfin))


def _pool_ffn_kernel(final_norm, nb, start_pos, x_ref, buf_ref, gmix_ref, pw_ref, ps_ref,
                     gffn_ref, w1_ref, w2_ref, gfin_ref, o_ref, buf_out, carry):
    tm, d = x_ref.shape
    ngrp = len(POOL_WINDOWS)
    gw = d // ngrp

    @pl.when(pl.program_id(0) == 0)
    def _():
        carry[...] = buf_ref[...]

    x = x_ref[...]
    h = _rmsnorm(x, gmix_ref[...])
    ext = jnp.concatenate([carry[...], h], axis=0)
    carry[...] = ext[tm:]
    buf_out[...] = ext[tm:]

    rowi = lax.broadcasted_iota(jnp.int32, (tm, gw), 0) + pl.program_id(0) * tm
    pos = start_pos + jnp.right_shift(rowi, int(math.log2(nb)))
    ys = []
    for gi, w in enumerate(POOL_WINDOWS):
        sl = slice(gi * gw, (gi + 1) * gw)
        s = ext[:, sl]
        span = 1
        while span < w:
            s = s[span * nb:] + s[: s.shape[0] - span * nb]
            span *= 2
        s = s[s.shape[0] - tm:]
        cnt = jnp.minimum(pos + 1, w).astype(F32)
        diff = (s / cnt - h[:, sl]).astype(BF16)
        ys.append(_dot(diff, pw_ref[gi]))
    x = x + jnp.concatenate(ys, axis=1) * ps_ref[...]
    out = _ffn(x, gffn_ref[...], w1_ref, w2_ref)
    if final_norm:
        out = _rmsnorm(out, gfin_ref[...])
    o_ref[...] = out


def _pool_ffn_call(x, buf, gmix, pw, ps, gffn, w1, w2, gfin, final_norm, nb, start_pos, tm):
    rows, d = x.shape
    assert nb & (nb - 1) == 0
    row = lambda a: a.reshape(1, -1)
    x_spec = pl.BlockSpec((tm, d), lambda i: (i, 0))
    return pl.pallas_call(
        functools.partial(_pool_ffn_kernel, final_norm, nb, start_pos),
        grid=(rows // tm,),
        in_specs=[x_spec, _const_spec(buf.shape), _const_spec((1, d)), _const_spec(pw.shape),
                  _const_spec((1, d)), _const_spec((1, d)), _const_spec(w1.shape),
                  _const_spec(w2.shape), _const_spec((1, d))],
        out_specs=[x_spec, _const_spec(buf.shape)],
        out_shape=[jax.ShapeDtypeStruct((rows, d), F32), jax.ShapeDtypeStruct(buf.shape, F32)],
        scratch_shapes=[pltpu.VMEM(buf.shape, F32)],
        compiler_params=pltpu.CompilerParams(
            dimension_semantics=("arbitrary",), vmem_limit_bytes=VMEM_LIMIT),
        name="pool_ffn",
    )(x, buf, row(gmix), pw, row(ps), row(gffn), w1, w2, row(gfin))


def _to_blockdiag(s):
    nb, nh = s.shape[:2]
    s = s.reshape(nb, nh // 2, 2, HEAD_DIM, HEAD_DIM)
    z = jnp.zeros_like(s[:, :, 0])
    top = jnp.concatenate([s[:, :, 0], z], axis=-1)
    bot = jnp.concatenate([z, s[:, :, 1]], axis=-1)
    return jnp.concatenate([top, bot], axis=-2)


def _from_blockdiag(s):
    nb, npair = s.shape[:2]
    a = s[:, :, :HEAD_DIM, :HEAD_DIM]
    b = s[:, :, HEAD_DIM:, HEAD_DIM:]
    return jnp.stack([a, b], axis=2).reshape(nb, 2 * npair, HEAD_DIM, HEAD_DIM)


def _trunk(x, start_pos, shift_states, wkv_states, pool_bufs, p):
    nb, nt, d = x.shape
    depth = p['norm_mix'].shape[0]
    rows = nb * nt
    tm = min(512, rows)
    assert rows % tm == 0 and tm % nb == 0
    xt = x.transpose(1, 0, 2).reshape(rows, d)
    new_shift, new_wkv, new_pool = [], [], []
    v_first = None
    for i in range(depth):
        j = i // 2
        last = i == depth - 1
        if i % 2 == 0:
            vres = None if j == 0 else (p['v0'][j - 1], p['v1'][j - 1], p['v2'][j - 1])
            r, lw, k, v, kk, b, g, s_shift = _pre_call(
                xt, shift_states[j], p['norm_mix'][i], p['mix'][j],
                p['w_rkv'][j, 0], p['w_rkv'][j, 1], p['w_rkv'][j, 2],
                p['w0'][j], p['w1'][j], p['w2'][j], p['a0'][j], p['a1'][j], p['a2'][j],
                vres, v_first, p['g1'][j], p['g2'][j], p['k_k'][j], p['k_a'][j], nb, tm)
            if j == 0:
                v_first = v
            y, s_wkv = _wkv_call(r, lw, k, v, kk, b, _to_blockdiag(wkv_states[j]),
                                 p['r_k'][j].reshape(-1), p['gn_w'][j], p['gn_b'][j], nb, nt)
            new_shift.append(s_shift)
            new_wkv.append(_from_blockdiag(s_wkv))
            xt = _attn_ffn_call(xt, y, g, p['w_o'][j], p['norm_ffn'][i], p['ffn_w1'][i],
                                p['ffn_w2'][i], p['norm_final'], last, tm)
        else:
            buf = pool_bufs[j].transpose(1, 0, 2).reshape(POOL_BUF * nb, d)
            xt, s_buf = _pool_ffn_call(xt, buf, p['norm_mix'][i], p['pool_w'][j], p['pool_scale'][j],
                                       p['norm_ffn'][i], p['ffn_w1'][i], p['ffn_w2'][i],
                                       p['norm_final'], last, nb, start_pos, tm)
            new_pool.append(s_buf.reshape(POOL_BUF, nb, d).transpose(1, 0, 2))
    y = xt.reshape(nt, nb, d).transpose(1, 0, 2)
    return y, jnp.stack(new_wkv), jnp.stack(new_shift), jnp.stack(new_pool)


def kernel(x_prompt, x_sample, state_wkv, state_shift, state_pool, norm_mix, norm_ffn, norm_final,
           rwkv_mix, rwkv_w_rkv, rwkv_w_o, rwkv_w0, rwkv_w1, rwkv_w2, rwkv_a0, rwkv_a1, rwkv_a2,
           rwkv_v0, rwkv_v1, rwkv_v2, rwkv_g1, rwkv_g2, rwkv_k_k, rwkv_k_a, rwkv_r_k,
           rwkv_gn_w, rwkv_gn_b, pool_w, pool_scale, ffn_w1, ffn_w2):
    bf = lambda a: a.astype(BF16)
    p = {'norm_mix': norm_mix, 'norm_ffn': norm_ffn, 'norm_final': norm_final,
         'mix': rwkv_mix, 'w_rkv': bf(rwkv_w_rkv), 'w_o': bf(rwkv_w_o), 'w0': rwkv_w0,
         'w1': bf(rwkv_w1), 'w2': bf(rwkv_w2), 'a0': rwkv_a0, 'a1': bf(rwkv_a1),
         'a2': bf(rwkv_a2), 'v0': rwkv_v0, 'v1': bf(rwkv_v1), 'v2': bf(rwkv_v2),
         'g1': bf(rwkv_g1), 'g2': bf(rwkv_g2), 'k_k': rwkv_k_k, 'k_a': rwkv_k_a,
         'r_k': rwkv_r_k, 'gn_w': rwkv_gn_w, 'gn_b': rwkv_gn_b, 'pool_w': bf(pool_w),
         'pool_scale': pool_scale, 'ffn_w1': bf(ffn_w1), 'ffn_w2': bf(ffn_w2)}
    dt = x_prompt.dtype
    nb, _, d = x_prompt.shape
    n_rwkv = state_wkv.shape[0]
    n_pool = state_pool.shape[0]
    nh = d // HEAD_DIM
    z_shift = jnp.zeros((n_rwkv, nb, d), dt)
    z_wkv = jnp.zeros((n_rwkv, nb, nh, HEAD_DIM, HEAD_DIM), dt)
    z_pool = jnp.zeros((n_pool, nb, POOL_BUF, d), dt)
    y_p, wkv_p, shift_p, pool_p = _trunk(x_prompt, 0, z_shift, z_wkv, z_pool, p)
    y_s, wkv_s, shift_s, pool_s = _trunk(x_sample, PAST_LEN, state_shift, state_wkv, state_pool, p)
    return (y_p, y_s, wkv_p, shift_p, pool_p, wkv_s, shift_s, pool_s)
```

```python
import functools
import math

import jax
import jax.numpy as jnp
from jax import lax
from jax.experimental import pallas as pl
from jax.experimental.pallas import tpu as pltpu

HEAD_DIM = 64
LANES = 128
SUBLANES = 8
WKV_LANES = 512
WKV_BLOCK_ELEMS = 8 * 128 * 512
POOL_WINDOWS = (2, 4, 8, 16)
POOL_BUF = max(POOL_WINDOWS) - 1
PAST_LEN = 16384
NORM_EPS = 1e-6
GN_EPS = 64e-5
L2_EPS = 1e-12
WKV_CHUNK = 16
VMEM_LIMIT = 56 * 1024 * 1024

BF16 = jnp.bfloat16
F32 = jnp.float32


def _dot(a, b):
    return jnp.dot(a, b, preferred_element_type=F32)


def _dot_nt(a, b):
    return lax.dot_general(a, b, (((1,), (1,)), ((), ())), preferred_element_type=F32)


def _sigmoid(x):
    return 1.0 / (1.0 + jnp.exp(-x))


def _rmsnorm(x, g):
    ms = jnp.mean(x * x, axis=-1, keepdims=True)
    return x * lax.rsqrt(ms + NORM_EPS) * g


def _head_ones():
    r = lax.broadcasted_iota(jnp.int32, (LANES, LANES), 0) // HEAD_DIM
    c = lax.broadcasted_iota(jnp.int32, (LANES, LANES), 1) // HEAD_DIM
    return jnp.where(r == c, 1.0, 0.0).astype(BF16)


def _head_sum_bf16(p, ones):
    return _dot(p.astype(BF16), ones)


def _head_sum(p, ones):
    hi = p.astype(BF16)
    lo = (p - hi.astype(F32)).astype(BF16)
    return _dot(hi, ones) + _dot(lo, ones)


def _const_spec(shape):
    n = len(shape)
    return pl.BlockSpec(shape, lambda *_: (0,) * n)


def _slab_lanes(d):
    return WKV_LANES if d % WKV_LANES == 0 else LANES


def _pre_kernel(has_vres, nb, *refs):
    it = iter(refs)
    x_ref = next(it); shift_ref = next(it); gmix_ref = next(it); mix_ref = next(it)
    wr_ref = next(it); wk_ref = next(it); wv_ref = next(it)
    w0_ref = next(it); w1_ref = next(it); w2_ref = next(it)
    a0_ref = next(it); a1_ref = next(it); a2_ref = next(it)
    if has_vres:
        v0_ref = next(it); v1_ref = next(it); v2_ref = next(it); vfirst_ref = next(it)
    g1_ref = next(it); g2_ref = next(it); kk_ref = next(it); ka_ref = next(it)
    r_out = next(it); lw_out = next(it); k_out = next(it); v_out = next(it)
    kk_out = next(it); b_out = next(it); g_out = next(it); shift_out = next(it)
    carry = next(it)

    tm, d = x_ref.shape
    nslab, _, slab = r_out.shape

    @pl.when(pl.program_id(0) == 0)
    def _():
        carry[...] = shift_ref[...]

    h = _rmsnorm(x_ref[...], gmix_ref[...])
    if tm > nb:
        hp = jnp.concatenate([carry[...], h[: tm - nb]], axis=0)
    else:
        hp = carry[...]
    carry[...] = h[tm - nb:]
    shift_out[...] = h[tm - nb:]
    dx = hp - h

    def mixed(i):
        return (h + dx * mix_ref[i:i + 1, :]).astype(BF16)

    xv = mixed(2)
    r = _dot(mixed(0), wr_ref[...])
    k = _dot(mixed(1), wk_ref[...])
    v = _dot(xv, wv_ref[...])
    wpre = w0_ref[...] + _dot(jnp.tanh(_dot(mixed(3), w1_ref[...])).astype(BF16), w2_ref[...])
    lw = -math.exp(-0.5) * _sigmoid(wpre)
    if has_vres:
        vfirst = jnp.concatenate([vfirst_ref[p] for p in range(nslab)], axis=1)
        gate = _sigmoid(v0_ref[...] + _dot(_dot(xv, v1_ref[...]).astype(BF16), v2_ref[...]))
        v = v + (vfirst - v) * gate
    a = _sigmoid(a0_ref[...] + _dot(_dot(mixed(4), a1_ref[...]).astype(BF16), a2_ref[...]))
    g = _dot(_sigmoid(_dot(mixed(5), g1_ref[...])).astype(BF16), g2_ref[...])
    g_out[...] = g

    ones = _head_ones()
    kk = k * kk_ref[...]
    k2 = k * (1.0 + (a - 1.0) * ka_ref[...])
    kkn = []
    for p in range(d // LANES):
        kkp = kk[:, p * LANES:(p + 1) * LANES]
        nrm = jnp.sqrt(_head_sum(kkp * kkp, ones))
        kkn.append(kkp / jnp.maximum(nrm, L2_EPS))
    kkn = jnp.concatenate(kkn, axis=1)
    bb = kkn * a
    for p in range(nslab):
        sl = slice(p * slab, (p + 1) * slab)
        r_out[p] = r[:, sl]
        lw_out[p] = lw[:, sl]
        k_out[p] = k2[:, sl]
        v_out[p] = v[:, sl]
        kk_out[p] = kkn[:, sl]
        b_out[p] = bb[:, sl]


def _pre_call(x, shift_in, gmix, mix, wr, wk, wv, w0, w1, w2, a0, a1, a2, vres, vfirst,
              g1, g2, k_k, k_a, nb, tm):
    rows, d = x.shape
    slab = _slab_lanes(d)
    nslab = d // slab
    has_vres = vres is not None
    row = lambda a: a.reshape(1, -1)
    args = [x, shift_in, row(gmix), mix, wr, wk, wv, row(w0), w1, w2, row(a0), a1, a2]
    specs = [pl.BlockSpec((tm, d), lambda i: (i, 0))] + [_const_spec(a.shape) for a in args[1:]]
    sm_spec = pl.BlockSpec((nslab, tm, slab), lambda i: (0, i, 0))
    if has_vres:
        v0, v1, v2 = vres
        extra = [row(v0), v1, v2]
        args += extra + [vfirst]
        specs += [_const_spec(a.shape) for a in extra] + [sm_spec]
    tail = [g1, g2, row(k_k), row(k_a)]
    args += tail
    specs += [_const_spec(a.shape) for a in tail]
    sm = jax.ShapeDtypeStruct((nslab, rows, slab), F32)
    out_shape = [sm] * 6 + [jax.ShapeDtypeStruct((rows, d), F32), jax.ShapeDtypeStruct((nb, d), F32)]
    out_specs = [sm_spec] * 6 + [pl.BlockSpec((tm, d), lambda i: (i, 0)), _const_spec((nb, d))]
    return pl.pallas_call(
        functools.partial(_pre_kernel, has_vres, nb),
        grid=(rows // tm,),
        in_specs=specs,
        out_specs=out_specs,
        out_shape=out_shape,
        scratch_shapes=[pltpu.VMEM((nb, d), F32)],
        compiler_params=pltpu.CompilerParams(
            dimension_semantics=("arbitrary",), vmem_limit_bytes=VMEM_LIMIT),
        name="rwkv_pre",
    )(*args)


_WKV_SLOTS = ("at", "rt", "u", "v", "bh", "kh", "x0", "y0")


def _wkv_kernel(tchunk, nchunks, r_ref, lw_ref, k_ref, v_ref, kk_ref, b_ref, s0_ref,
                rk_ref, gnw_ref, gnb_ref, y_ref, sout_ref, s_scr, *work):
    nseq = SUBLANES
    npl = s_scr.shape[1]
    cp = max(tchunk, 8)
    n = cp * nseq
    nv = tchunk * nseq
    grp = LANES // (2 * cp)
    nslot = len(_WKV_SLOTS)
    scr = [dict(zip(_WKV_SLOTS, work[q * nslot:(q + 1) * nslot])) for q in range(npl)]

    @pl.when(pl.program_id(2) == 0)
    def _():
        s_scr[...] = s0_ref[...]

    ones = _head_ones()
    hr = lax.broadcasted_iota(jnp.int32, (LANES, LANES), 0) // HEAD_DIM
    hc = lax.broadcasted_iota(jnp.int32, (LANES, LANES), 1) // HEAD_DIM
    same_head = hr == hc
    lane_head = lax.broadcasted_iota(jnp.int32, (1, LANES), 1) // HEAD_DIM
    head_f32 = [jnp.where(lane_head == hh, 1.0, 0.0) for hh in range(2)]
    head_bf16 = [m.astype(BF16) for m in head_f32]
    ri = lax.broadcasted_iota(jnp.int32, (n, 2 * n), 0)
    ci = lax.broadcasted_iota(jnp.int32, (n, 2 * n), 1) % n
    same_seq = (ri % nseq) == (ci % nseq)
    tri_incl = jnp.where(same_seq & (ci // nseq <= ri // nseq), 1.0, 0.0)
    tri_strict = jnp.where(same_seq & (ci // nseq < ri // nseq), 1.0, 0.0)[:, :n]

    def load(ref, c, q):
        val = ref[pl.ds(c * tchunk, tchunk), :, q * LANES:(q + 1) * LANES].reshape(nv, LANES)
        if n > nv:
            val = jnp.concatenate([val, jnp.zeros((n - nv, LANES), F32)], axis=0)
        return val

    def tile_rows(slab, reps):
        return slab if reps == 1 else jnp.concatenate([slab] * reps, axis=0)

    def seq_rows(ref, s):
        return ref[pl.ds(s, cp, stride=nseq), :]

    def chunk_pair(c, q):
        w = scr[q]
        r = load(r_ref, c, q); lw = load(lw_ref, c, q); k = load(k_ref, c, q)
        v = load(v_ref, c, q); kk = load(kk_ref, c, q); b = load(b_ref, c, q)

        acc = jnp.zeros((nseq, LANES), F32)
        cums = []
        for t in range(cp):
            acc = acc + lw[t * nseq:(t + 1) * nseq]
            cums.append(acc)
        cum = jnp.concatenate(cums, axis=0)
        w_inc = jnp.exp(cum)
        w_inv = jnp.exp(-cum)
        w_exc = jnp.concatenate([jnp.ones((nseq, LANES), F32), w_inc[: n - nseq]], axis=0)
        w_tot = w_inc[n - nseq:]
        at = -kk * w_exc
        rt = r * w_inc
        kt = k * w_inv
        bt = b * w_inv
        w_tot_rows = tile_rows(w_tot, cp)
        w["at"][...] = at
        w["rt"][...] = rt
        w["v"][...] = v
        w["bh"][...] = bt * w_tot_rows
        w["kh"][...] = kt * w_tot_rows

        at16 = at.astype(BF16)
        rt16 = rt.astype(BF16)
        kt16 = kt.astype(BF16)
        kb16 = jnp.concatenate([kt16, bt.astype(BF16)], axis=0)
        gram_a = _dot_nt(jnp.concatenate([at16 * head_bf16[0], at16 * head_bf16[1]], axis=0), kt16)
        gram_r = _dot_nt(jnp.concatenate([rt16 * head_bf16[0], rt16 * head_bf16[1]], axis=0), kb16)
        ga = jnp.concatenate([gram_a[:n] * tri_strict, gram_a[n:] * tri_strict], axis=1).astype(BF16)
        gr = jnp.concatenate([gram_r[:n] * tri_incl, gram_r[n:] * tri_incl], axis=1).astype(BF16)
        vm = jnp.concatenate([v * head_f32[0], v * head_f32[1]], axis=0)
        yield

        prods = [tile_rows(at[t * nseq:(t + 1) * nseq], t) * bt[:t * nseq] for t in range(1, cp)]
        coef = _head_sum_bf16(jnp.concatenate(prods, axis=0), ones)
        yield

        for s in range(nseq):
            lhs = jnp.concatenate([seq_rows(w["at"], s), seq_rows(w["rt"], s)], axis=0).astype(BF16)
            out = _dot_nt(lhs, s_scr[s, q].astype(BF16))
            w["x0"][pl.ds(s, cp, stride=nseq), :] = out[:cp]
            w["y0"][pl.ds(s, cp, stride=nseq), :] = out[cp:]
        yield

        x = w["x0"][...] + _dot(ga, vm.astype(BF16))
        yield

        us = [x[0:nseq]]
        off = 0
        for t in range(1, cp):
            u_t = x[t * nseq:(t + 1) * nseq]
            for j in range(t):
                u_t = u_t + coef[off + j * nseq:off + (j + 1) * nseq] * us[j]
            off += t * nseq
            us.append(u_t)
        u = jnp.concatenate(us, axis=0)
        w["u"][...] = u
        yield

        um = jnp.concatenate([u * head_f32[0], u * head_f32[1]], axis=0)
        vum = jnp.concatenate([vm[:n], um[:n], vm[n:], um[n:]], axis=0)
        y = w["y0"][...] + _dot(gr, vum.astype(BF16))
        yield

        zp = []
        for s in range(nseq):
            zp += [seq_rows(w["u"], s), seq_rows(w["v"], s)]
        z_t = jnp.concatenate(zp, axis=0).T.astype(BF16)
        for g0 in range(0, nseq, grp):
            cols = []
            for e in range(grp):
                gs = jnp.concatenate([seq_rows(w["bh"], g0 + e), seq_rows(w["kh"], g0 + e)],
                                     axis=0).astype(BF16)
                blk = [gs]
                if e > 0:
                    blk = [jnp.zeros((2 * cp * e, LANES), BF16)] + blk
                if e < grp - 1:
                    blk = blk + [jnp.zeros((2 * cp * (grp - 1 - e), LANES), BF16)]
                cols.append(jnp.concatenate(blk, axis=0))
            lhs = z_t[:, (g0 // grp) * LANES:(g0 // grp + 1) * LANES]
            delta = _dot(lhs, jnp.concatenate(cols, axis=1))
            for e in range(grp):
                s = g0 + e
                dl = delta[:, e * LANES:(e + 1) * LANES]
                s_scr[s, q] = s_scr[s, q] * w_tot[s:s + 1, :] + jnp.where(same_head, dl, 0.0)
        yield

        ql = slice(q * LANES, (q + 1) * LANES)
        yv, rv, kv, vv = y[:nv], r[:nv], k[:nv], v[:nv]
        ysq = yv * yv
        y_hi = yv.astype(BF16)
        q_hi = ysq.astype(BF16)
        stack = jnp.concatenate([y_hi, (yv - y_hi.astype(F32)).astype(BF16),
                                 q_hi, (ysq - q_hi.astype(F32)).astype(BF16),
                                 (rv * kv * rk_ref[:, ql]).astype(BF16)], axis=0)
        hs = _dot(stack, ones) * (1.0 / HEAD_DIM)
        mu = hs[:nv] + hs[nv:2 * nv]
        var = hs[2 * nv:3 * nv] + hs[3 * nv:4 * nv] - mu * mu
        yn = (yv - mu) * lax.rsqrt(var + GN_EPS) * gnw_ref[:, ql] + gnb_ref[:, ql]
        bonus = hs[4 * nv:] * float(HEAD_DIM) * vv
        y_ref[pl.ds(c * tchunk, tchunk), :, ql] = (yn + bonus).reshape(tchunk, nseq, LANES)

    def chunk(c, carry):
        stages = [chunk_pair(c, q) for q in range(npl)]
        while stages:
            alive = []
            for g in stages:
                if next(g, "done") is None:
                    alive.append(g)
            stages = alive
        return carry

    lax.fori_loop(0, nchunks, chunk, 0)

    @pl.when(pl.program_id(2) == pl.num_programs(2) - 1)
    def _():
        sout_ref[...] = s_scr[...]


def _wkv_call(r, lw, k, v, kk, b, s0, r_k, gn_w, gn_b, nb, nt):
    nslab, _, slab = r.shape
    npl = slab // LANES
    ngroup = nb // SUBLANES
    tchunk = min(WKV_CHUNK, nt)
    tb = min(WKV_BLOCK_ELEMS // (SUBLANES * slab), nt)
    nchunks = tb // tchunk
    assert nt % tb == 0 and tb % tchunk == 0 and nb % SUBLANES == 0
    view = lambda a: a.reshape(nslab, nt, ngroup, SUBLANES, slab)
    act_spec = pl.BlockSpec((None, tb, None, SUBLANES, slab), lambda g, p, t: (p, t, g, 0, 0))
    st_spec = pl.BlockSpec((SUBLANES, npl, LANES, LANES), lambda g, p, t: (g, p, 0, 0))
    par_spec = pl.BlockSpec((None, 1, slab), lambda g, p, t: (p, 0, 0))
    par = lambda a: a.reshape(nslab, 1, slab)
    n = max(tchunk, 8) * SUBLANES
    y, s_out = pl.pallas_call(
        functools.partial(_wkv_kernel, tchunk, nchunks),
        grid=(ngroup, nslab, nt // tb),
        in_specs=[act_spec] * 6 + [st_spec] + [par_spec] * 3,
        out_specs=[act_spec, st_spec],
        out_shape=[jax.ShapeDtypeStruct((nslab, nt, ngroup, SUBLANES, slab), F32),
                   jax.ShapeDtypeStruct(s0.shape, F32)],
        scratch_shapes=[pltpu.VMEM((SUBLANES, npl, LANES, LANES), F32)]
        + [pltpu.VMEM((n, LANES), F32)] * (len(_WKV_SLOTS) * npl),
        compiler_params=pltpu.CompilerParams(
            dimension_semantics=("arbitrary", "arbitrary", "arbitrary"),
            vmem_limit_bytes=VMEM_LIMIT),
        name="wkv7",
    )(view(r), view(lw), view(k), view(v), view(kk), view(b), s0, par(r_k), par(gn_w), par(gn_b))
    return y.reshape(nslab, nt * nb, slab), s_out


def _ffn(x, gffn, w1_ref, w2_ref):
    h = _rmsnorm(x, gffn).astype(BF16)
    dff = w1_ref.shape[1]
    step = min(dff, 1024)
    acc = x
    for c in range(dff // step):
        hh = _dot(h, w1_ref[:, c * step:(c + 1) * step])
        hh = jnp.square(jnp.maximum(hh, 0.0)).astype(BF16)
        acc = acc + _dot(hh, w2_ref[c * step:(c + 1) * step, :])
    return acc


def _attn_ffn_kernel(final_norm, x_ref, y_ref, g_ref, wo_ref, gffn_ref, w1_ref, w2_ref, gfin_ref, o_ref):
    nslab = y_ref.shape[0]
    y = jnp.concatenate([y_ref[p] for p in range(nslab)], axis=1)
    x = x_ref[...] + _dot((y * g_ref[...]).astype(BF16), wo_ref[...])
    out = _ffn(x, gffn_ref[...], w1_ref, w2_ref)
    if final_norm:
        out = _rmsnorm(out, gfin_ref[...])
    o_ref[...] = out


def _attn_ffn_call(x, y, g, wo, gffn, w1, w2, gfin, final_norm, tm):
    rows, d = x.shape
    nslab, _, slab = y.shape
    row = lambda a: a.reshape(1, -1)
    x_spec = pl.BlockSpec((tm, d), lambda i: (i, 0))
    return pl.pallas_call(
        functools.partial(_attn_ffn_kernel, final_norm),
        grid=(rows // tm,),
        in_specs=[x_spec, pl.BlockSpec((nslab, tm, slab), lambda i: (0, i, 0)), x_spec,
                  _const_spec(wo.shape), _const_spec((1, d)), _const_spec(w1.shape),
                  _const_spec(w2.shape), _const_spec((1, d))],
        out_specs=x_spec,
        out_shape=jax.ShapeDtypeStruct((rows, d), F32),
        compiler_params=pltpu.CompilerParams(
            dimension_semantics=("arbitrary",), vmem_limit_bytes=VMEM_LIMIT),
        name="attn_out_ffn",
    )(x, y, g, wo, row(gffn), w1, w2, row(gfin))


def _pool_ffn_kernel(final_norm, nb, start_pos, x_ref, buf_ref, gmix_ref, pw_ref, ps_ref,
                     gffn_ref, w1_ref, w2_ref, gfin_ref, o_ref, buf_out, carry):
    tm, d = x_ref.shape
    ngrp = len(POOL_WINDOWS)
    gw = d // ngrp

    @pl.when(pl.program_id(0) == 0)
    def _():
        carry[...] = buf_ref[...]

    x = x_ref[...]
    h = _rmsnorm(x, gmix_ref[...])
    ext = jnp.concatenate([carry[...], h], axis=0)
    carry[...] = ext[tm:]
    buf_out[...] = ext[tm:]

    rowi = lax.broadcasted_iota(jnp.int32, (tm, gw), 0) + pl.program_id(0) * tm
    pos = start_pos + jnp.right_shift(rowi, int(math.log2(nb)))
    ys = []
    for gi, w in enumerate(POOL_WINDOWS):
        sl = slice(gi * gw, (gi + 1) * gw)
        s = ext[:, sl]
        span = 1
        while span < w:
            s = s[span * nb:] + s[: s.shape[0] - span * nb]
            span *= 2
        s = s[s.shape[0] - tm:]
        cnt = jnp.minimum(pos + 1, w).astype(F32)
        diff = (s / cnt - h[:, sl]).astype(BF16)
        ys.append(_dot(diff, pw_ref[gi]))
    x = x + jnp.concatenate(ys, axis=1) * ps_ref[...]
    out = _ffn(x, gffn_ref[...], w1_ref, w2_ref)
    if final_norm:
        out = _rmsnorm(out, gfin_ref[...])
    o_ref[...] = out


def _pool_ffn_call(x, buf, gmix, pw, ps, gffn, w1, w2, gfin, final_norm, nb, start_pos, tm):
    rows, d = x.shape
    assert nb & (nb - 1) == 0
    row = lambda a: a.reshape(1, -1)
    x_spec = pl.BlockSpec((tm, d), lambda i: (i, 0))
    return pl.pallas_call(
        functools.partial(_pool_ffn_kernel, final_norm, nb, start_pos),
        grid=(rows // tm,),
        in_specs=[x_spec, _const_spec(buf.shape), _const_spec((1, d)), _const_spec(pw.shape),
                  _const_spec((1, d)), _const_spec((1, d)), _const_spec(w1.shape),
                  _const_spec(w2.shape), _const_spec((1, d))],
        out_specs=[x_spec, _const_spec(buf.shape)],
        out_shape=[jax.ShapeDtypeStruct((rows, d), F32), jax.ShapeDtypeStruct(buf.shape, F32)],
        scratch_shapes=[pltpu.VMEM(buf.shape, F32)],
        compiler_params=pltpu.CompilerParams(
            dimension_semantics=("arbitrary",), vmem_limit_bytes=VMEM_LIMIT),
        name="pool_ffn",
    )(x, buf, row(gmix), pw, row(ps), row(gffn), w1, w2, row(gfin))


def _to_blockdiag(s):
    nb, nh = s.shape[:2]
    s = s.reshape(nb, nh // 2, 2, HEAD_DIM, HEAD_DIM)
    z = jnp.zeros_like(s[:, :, 0])
    top = jnp.concatenate([s[:, :, 0], z], axis=-1)
    bot = jnp.concatenate([z, s[:, :, 1]], axis=-1)
    return jnp.concatenate([top, bot], axis=-2)


def _from_blockdiag(s):
    nb, npair = s.shape[:2]
    a = s[:, :, :HEAD_DIM, :HEAD_DIM]
    b = s[:, :, HEAD_DIM:, HEAD_DIM:]
    return jnp.stack([a, b], axis=2).reshape(nb, 2 * npair, HEAD_DIM, HEAD_DIM)


def _trunk(x, start_pos, shift_states, wkv_states, pool_bufs, p):
    nb, nt, d = x.shape
    depth = p['norm_mix'].shape[0]
    rows = nb * nt
    tm = min(512, rows)
    assert rows % tm == 0 and tm % nb == 0
    xt = x.transpose(1, 0, 2).reshape(rows, d)
    new_shift, new_wkv, new_pool = [], [], []
    v_first = None
    for i in range(depth):
        j = i // 2
        last = i == depth - 1
        if i % 2 == 0:
            vres = None if j == 0 else (p['v0'][j - 1], p['v1'][j - 1], p['v2'][j - 1])
            r, lw, k, v, kk, b, g, s_shift = _pre_call(
                xt, shift_states[j], p['norm_mix'][i], p['mix'][j],
                p['w_rkv'][j, 0], p['w_rkv'][j, 1], p['w_rkv'][j, 2],
                p['w0'][j], p['w1'][j], p['w2'][j], p['a0'][j], p['a1'][j], p['a2'][j],
                vres, v_first, p['g1'][j], p['g2'][j], p['k_k'][j], p['k_a'][j], nb, tm)
            if j == 0:
                v_first = v
            y, s_wkv = _wkv_call(r, lw, k, v, kk, b, _to_blockdiag(wkv_states[j]),
                                 p['r_k'][j].reshape(-1), p['gn_w'][j], p['gn_b'][j], nb, nt)
            new_shift.append(s_shift)
            new_wkv.append(_from_blockdiag(s_wkv))
            xt = _attn_ffn_call(xt, y, g, p['w_o'][j], p['norm_ffn'][i], p['ffn_w1'][i],
                                p['ffn_w2'][i], p['norm_final'], last, tm)
        else:
            buf = pool_bufs[j].transpose(1, 0, 2).reshape(POOL_BUF * nb, d)
            xt, s_buf = _pool_ffn_call(xt, buf, p['norm_mix'][i], p['pool_w'][j], p['pool_scale'][j],
                                       p['norm_ffn'][i], p['ffn_w1'][i], p['ffn_w2'][i],
                                       p['norm_final'], last, nb, start_pos, tm)
            new_pool.append(s_buf.reshape(POOL_BUF, nb, d).transpose(1, 0, 2))
    y = xt.reshape(nt, nb, d).transpose(1, 0, 2)
    return y, jnp.stack(new_wkv), jnp.stack(new_shift), jnp.stack(new_pool)


def kernel(x_prompt, x_sample, state_wkv, state_shift, state_pool, norm_mix, norm_ffn, norm_final,
           rwkv_mix, rwkv_w_rkv, rwkv_w_o, rwkv_w0, rwkv_w1, rwkv_w2, rwkv_a0, rwkv_a1, rwkv_a2,
           rwkv_v0, rwkv_v1, rwkv_v2, rwkv_g1, rwkv_g2, rwkv_k_k, rwkv_k_a, rwkv_r_k,
           rwkv_gn_w, rwkv_gn_b, pool_w, pool_scale, ffn_w1, ffn_w2):
    bf = lambda a: a.astype(BF16)
    p = {'norm_mix': norm_mix, 'norm_ffn': norm_ffn, 'norm_final': norm_final,
         'mix': rwkv_mix, 'w_rkv': bf(rwkv_w_rkv), 'w_o': bf(rwkv_w_o), 'w0': rwkv_w0,
         'w1': bf(rwkv_w1), 'w2': bf(rwkv_w2), 'a0': rwkv_a0, 'a1': bf(rwkv_a1),
         'a2': bf(rwkv_a2), 'v0': rwkv_v0, 'v1': bf(rwkv_v1), 'v2': bf(rwkv_v2),
         'g1': bf(rwkv_g1), 'g2': bf(rwkv_g2), 'k_k': rwkv_k_k, 'k_a': rwkv_k_a,
         'r_k': rwkv_r_k, 'gn_w': rwkv_gn_w, 'gn_b': rwkv_gn_b, 'pool_w': bf(pool_w),
         'pool_scale': pool_scale, 'ffn_w1': bf(ffn_w1), 'ffn_w2': bf(ffn_w2)}
    dt = x_prompt.dtype
    nb, _, d = x_prompt.shape
    n_rwkv = state_wkv.shape[0]
    n_pool = state_pool.shape[0]
    nh = d // HEAD_DIM
    z_shift = jnp.zeros((n_rwkv, nb, d), dt)
    z_wkv = jnp.zeros((n_rwkv, nb, nh, HEAD_DIM, HEAD_DIM), dt)
    z_pool = jnp.zeros((n_pool, nb, POOL_BUF, d), dt)
    y_p, wkv_p, shift_p, pool_p = _trunk(x_prompt, 0, z_shift, z_wkv, z_pool, p)
    y_s, wkv_s, shift_s, pool_s = _trunk(x_sample, PAST_LEN, state_shift, state_wkv, state_pool, p)
    return (y_p, y_s, wkv_p, shift_p, pool_p, wkv_s, shift_s, pool_s)
```

```python
import functools
import math

import jax
import jax.numpy as jnp
from jax import lax
from jax.experimental import pallas as pl
from jax.experimental.pallas import tpu as pltpu

HEAD_DIM = 64
LANES = 128
SUBLANES = 8
WKV_LANES = 512
WKV_BLOCK_ELEMS = 8 * 128 * 512
POOL_WINDOWS = (2, 4, 8, 16)
POOL_BUF = max(POOL_WINDOWS) - 1
PAST_LEN = 16384
NORM_EPS = 1e-6
GN_EPS = 64e-5
L2_EPS = 1e-12
WKV_CHUNK = 16
VMEM_LIMIT = 56 * 1024 * 1024

BF16 = jnp.bfloat16
F32 = jnp.float32


def _dot(a, b):
    return jnp.dot(a, b, preferred_element_type=F32)


def _dot_nt(a, b):
    return lax.dot_general(a, b, (((1,), (1,)), ((), ())), preferred_element_type=F32)


def _sigmoid(x):
    return 1.0 / (1.0 + jnp.exp(-x))


def _rmsnorm(x, g):
    ms = jnp.mean(x * x, axis=-1, keepdims=True)
    return x * lax.rsqrt(ms + NORM_EPS) * g


def _head_ones():
    r = lax.broadcasted_iota(jnp.int32, (LANES, LANES), 0) // HEAD_DIM
    c = lax.broadcasted_iota(jnp.int32, (LANES, LANES), 1) // HEAD_DIM
    return jnp.where(r == c, 1.0, 0.0).astype(BF16)


def _head_sum_bf16(p, ones):
    return _dot(p.astype(BF16), ones)


def _head_sum(p, ones):
    hi = p.astype(BF16)
    lo = (p - hi.astype(F32)).astype(BF16)
    return _dot(hi, ones) + _dot(lo, ones)


def _const_spec(shape):
    n = len(shape)
    return pl.BlockSpec(shape, lambda *_: (0,) * n)


def _pick(arr, *idx):
    shape = (None,) * len(idx) + arr.shape[len(idx):]
    at = tuple(idx) + (0,) * (arr.ndim - len(idx))
    return arr, pl.BlockSpec(shape, lambda *_: at)


def _rows_in(x_ref, xs_scr, nb):
    if len(x_ref.shape) == 2:
        return x_ref[...]
    tt = x_ref.shape[1]
    nblk = xs_scr.shape[0]
    for c in range(nblk):
        for b in range(nb):
            xs_scr[c, pl.ds(b, tt, stride=nb), :] = x_ref[b, :, c * LANES:(c + 1) * LANES]
    return jnp.concatenate([xs_scr[c] for c in range(nblk)], axis=1)


def _rows_out(o_ref, os_scr, val, nb):
    if len(o_ref.shape) == 2:
        o_ref[...] = val
        return
    tt = o_ref.shape[1]
    for c in range(os_scr.shape[0]):
        os_scr[c] = val[:, c * LANES:(c + 1) * LANES]
        for b in range(nb):
            o_ref[b, :, c * LANES:(c + 1) * LANES] = os_scr[c, pl.ds(b, tt, stride=nb), :]


def _x_spec(x, tm, nb):
    if x.ndim == 2:
        return pl.BlockSpec((tm, x.shape[1]), lambda i: (i, 0))
    return pl.BlockSpec((nb, tm // nb, x.shape[2]), lambda i: (0, i, 0))


def _slab_lanes(d):
    return WKV_LANES if d % WKV_LANES == 0 else LANES


def _pre_kernel(has_vres, nb, *refs):
    it = iter(refs)
    x_ref = next(it); shift_ref = next(it); gmix_ref = next(it); mix_ref = next(it)
    wr_ref = next(it); wk_ref = next(it); wv_ref = next(it)
    w0_ref = next(it); w1_ref = next(it); w2_ref = next(it)
    a0_ref = next(it); a1_ref = next(it); a2_ref = next(it)
    if has_vres:
        v0_ref = next(it); v1_ref = next(it); v2_ref = next(it); vfirst_ref = next(it)
    g1_ref = next(it); g2_ref = next(it); kk_ref = next(it); ka_ref = next(it)
    r_out = next(it); lw_out = next(it); k_out = next(it); v_out = next(it)
    kk_out = next(it); b_out = next(it); g_out = next(it); shift_out = next(it)
    carry = next(it); xs_scr = next(it)

    tm, d = g_out.shape
    nslab, _, slab = r_out.shape

    @pl.when(pl.program_id(0) == 0)
    def _():
        carry[...] = shift_ref[...]

    h = _rmsnorm(_rows_in(x_ref, xs_scr, nb), gmix_ref[...])
    if tm > nb:
        hp = jnp.concatenate([carry[...], h[: tm - nb]], axis=0)
    else:
        hp = carry[...]
    carry[...] = h[tm - nb:]
    shift_out[...] = h[tm - nb:]
    dx = hp - h

    def mixed(i):
        return (h + dx * mix_ref[i:i + 1, :]).astype(BF16)

    xv = mixed(2)
    r = _dot(mixed(0), wr_ref[...])
    k = _dot(mixed(1), wk_ref[...])
    v = _dot(xv, wv_ref[...])
    wpre = w0_ref[...] + _dot(jnp.tanh(_dot(mixed(3), w1_ref[...])).astype(BF16), w2_ref[...])
    lw = -math.exp(-0.5) * _sigmoid(wpre)
    if has_vres:
        vfirst = jnp.concatenate([vfirst_ref[p] for p in range(nslab)], axis=1)
        gate = _sigmoid(v0_ref[...] + _dot(_dot(xv, v1_ref[...]).astype(BF16), v2_ref[...]))
        v = v + (vfirst - v) * gate
    a = _sigmoid(a0_ref[...] + _dot(_dot(mixed(4), a1_ref[...]).astype(BF16), a2_ref[...]))
    g = _dot(_sigmoid(_dot(mixed(5), g1_ref[...])).astype(BF16), g2_ref[...])
    g_out[...] = g

    ones = _head_ones()
    kk = k * kk_ref[...]
    k2 = k * (1.0 + (a - 1.0) * ka_ref[...])
    kkn = []
    for p in range(d // LANES):
        kkp = kk[:, p * LANES:(p + 1) * LANES]
        nrm = jnp.sqrt(_head_sum(kkp * kkp, ones))
        kkn.append(kkp / jnp.maximum(nrm, L2_EPS))
    kkn = jnp.concatenate(kkn, axis=1)
    bb = kkn * a
    for p in range(nslab):
        sl = slice(p * slab, (p + 1) * slab)
        r_out[p] = r[:, sl]
        lw_out[p] = lw[:, sl]
        k_out[p] = k2[:, sl]
        v_out[p] = v[:, sl]
        kk_out[p] = kkn[:, sl]
        b_out[p] = bb[:, sl]


def _pre_call(x, p, i, j, shift_states, vfirst, nb, tm):
    d = x.shape[-1]
    rows = x.shape[0] if x.ndim == 2 else x.shape[0] * x.shape[1]
    slab = _slab_lanes(d)
    nslab = d // slab
    has_vres = j > 0
    sm_spec = pl.BlockSpec((nslab, tm, slab), lambda i_: (0, i_, 0))
    ops = [(x, _x_spec(x, tm, nb)), _pick(shift_states, j), _pick(p['norm_mix'], i), _pick(p['mix'], j),
           _pick(p['w_rkv'], j, 0), _pick(p['w_rkv'], j, 1), _pick(p['w_rkv'], j, 2),
           _pick(p['w0'], j), _pick(p['w1'], j), _pick(p['w2'], j),
           _pick(p['a0'], j), _pick(p['a1'], j), _pick(p['a2'], j)]
    if has_vres:
        ops += [_pick(p['v0'], j - 1), _pick(p['v1'], j - 1), _pick(p['v2'], j - 1), (vfirst, sm_spec)]
    ops += [_pick(p['g1'], j), _pick(p['g2'], j), _pick(p['k_k'], j), _pick(p['k_a'], j)]
    sm = jax.ShapeDtypeStruct((nslab, rows, slab), F32)
    out_shape = [sm] * 6 + [jax.ShapeDtypeStruct((rows, d), F32), jax.ShapeDtypeStruct((nb, d), F32)]
    out_specs = [sm_spec] * 6 + [pl.BlockSpec((tm, d), lambda i_: (i_, 0)), _const_spec((nb, d))]
    return pl.pallas_call(
        functools.partial(_pre_kernel, has_vres, nb),
        grid=(rows // tm,),
        in_specs=[o[1] for o in ops],
        out_specs=out_specs,
        out_shape=out_shape,
        scratch_shapes=[pltpu.VMEM((nb, d), F32), pltpu.VMEM((d // LANES, tm, LANES), F32)],
        compiler_params=pltpu.CompilerParams(
            dimension_semantics=("arbitrary",), vmem_limit_bytes=VMEM_LIMIT),
        name="rwkv_pre",
    )(*[o[0] for o in ops])


_WKV_SLOTS = ("at", "rt", "u", "v", "bh", "kh", "x0", "y0")


def _wkv_kernel(tchunk, nchunks, r_ref, lw_ref, k_ref, v_ref, kk_ref, b_ref, s0_ref,
                rk_ref, gnw_ref, gnb_ref, y_ref, sout_ref, s_scr, *work):
    nseq = SUBLANES
    npl = s_scr.shape[1]
    cp = max(tchunk, 8)
    n = cp * nseq
    nv = tchunk * nseq
    grp = LANES // (2 * cp)
    nslot = len(_WKV_SLOTS)
    scr = [dict(zip(_WKV_SLOTS, work[q * nslot:(q + 1) * nslot])) for q in range(npl)]

    @pl.when(pl.program_id(2) == 0)
    def _():
        zero = jnp.zeros((HEAD_DIM, HEAD_DIM), F32)
        for s in range(nseq):
            for q in range(npl):
                top = jnp.concatenate([s0_ref[s, 2 * q], zero], axis=1)
                bot = jnp.concatenate([zero, s0_ref[s, 2 * q + 1]], axis=1)
                s_scr[s, q] = jnp.concatenate([top, bot], axis=0)

    ones = _head_ones()
    hr = lax.broadcasted_iota(jnp.int32, (LANES, LANES), 0) // HEAD_DIM
    hc = lax.broadcasted_iota(jnp.int32, (LANES, LANES), 1) // HEAD_DIM
    same_head = hr == hc
    lane_head = lax.broadcasted_iota(jnp.int32, (1, LANES), 1) // HEAD_DIM
    head_f32 = [jnp.where(lane_head == hh, 1.0, 0.0) for hh in range(2)]
    head_bf16 = [m.astype(BF16) for m in head_f32]
    ri = lax.broadcasted_iota(jnp.int32, (n, 2 * n), 0)
    ci = lax.broadcasted_iota(jnp.int32, (n, 2 * n), 1) % n
    same_seq = (ri % nseq) == (ci % nseq)
    tri_incl = jnp.where(same_seq & (ci // nseq <= ri // nseq), 1.0, 0.0)
    tri_strict = jnp.where(same_seq & (ci // nseq < ri // nseq), 1.0, 0.0)[:, :n]

    def load(ref, c, q):
        val = ref[pl.ds(c * tchunk, tchunk), :, q * LANES:(q + 1) * LANES].reshape(nv, LANES)
        if n > nv:
            val = jnp.concatenate([val, jnp.zeros((n - nv, LANES), F32)], axis=0)
        return val

    def tile_rows(slab, reps):
        return slab if reps == 1 else jnp.concatenate([slab] * reps, axis=0)

    def seq_rows(ref, s):
        return ref[pl.ds(s, cp, stride=nseq), :]

    def chunk_pair(c, q):
        w = scr[q]
        r = load(r_ref, c, q); lw = load(lw_ref, c, q); k = load(k_ref, c, q)
        v = load(v_ref, c, q); kk = load(kk_ref, c, q); b = load(b_ref, c, q)

        acc = jnp.zeros((nseq, LANES), F32)
        cums = []
        for t in range(cp):
            acc = acc + lw[t * nseq:(t + 1) * nseq]
            cums.append(acc)
        cum = jnp.concatenate(cums, axis=0)
        w_inc = jnp.exp(cum)
        w_inv = jnp.exp(-cum)
        w_exc = jnp.concatenate([jnp.ones((nseq, LANES), F32), w_inc[: n - nseq]], axis=0)
        w_tot = w_inc[n - nseq:]
        at = -kk * w_exc
        rt = r * w_inc
        kt = k * w_inv
        bt = b * w_inv
        w_tot_rows = tile_rows(w_tot, cp)
        w["at"][...] = at
        w["rt"][...] = rt
        w["v"][...] = v
        w["bh"][...] = bt * w_tot_rows
        w["kh"][...] = kt * w_tot_rows

        at16 = at.astype(BF16)
        rt16 = rt.astype(BF16)
        kt16 = kt.astype(BF16)
        kb16 = jnp.concatenate([kt16, bt.astype(BF16)], axis=0)
        gram_a = _dot_nt(jnp.concatenate([at16 * head_bf16[0], at16 * head_bf16[1]], axis=0), kt16)
        gram_r = _dot_nt(jnp.concatenate([rt16 * head_bf16[0], rt16 * head_bf16[1]], axis=0), kb16)
        ga = jnp.concatenate([gram_a[:n] * tri_strict, gram_a[n:] * tri_strict], axis=1).astype(BF16)
        gr = jnp.concatenate([gram_r[:n] * tri_incl, gram_r[n:] * tri_incl], axis=1).astype(BF16)
        vm = jnp.concatenate([v * head_f32[0], v * head_f32[1]], axis=0)
        yield

        prods = [tile_rows(at[t * nseq:(t + 1) * nseq], t) * bt[:t * nseq] for t in range(1, cp)]
        coef = _head_sum_bf16(jnp.concatenate(prods, axis=0), ones)
        yield

        for s in range(nseq):
            lhs = jnp.concatenate([seq_rows(w["at"], s), seq_rows(w["rt"], s)], axis=0).astype(BF16)
            out = _dot_nt(lhs, s_scr[s, q].astype(BF16))
            w["x0"][pl.ds(s, cp, stride=nseq), :] = out[:cp]
            w["y0"][pl.ds(s, cp, stride=nseq), :] = out[cp:]
        yield

        x = w["x0"][...] + _dot(ga, vm.astype(BF16))
        yield

        us = [x[0:nseq]]
        off = 0
        for t in range(1, cp):
            u_t = x[t * nseq:(t + 1) * nseq]
            for j in range(t):
                u_t = u_t + coef[off + j * nseq:off + (j + 1) * nseq] * us[j]
            off += t * nseq
            us.append(u_t)
        u = jnp.concatenate(us, axis=0)
        w["u"][...] = u
        yield

        um = jnp.concatenate([u * head_f32[0], u * head_f32[1]], axis=0)
        vum = jnp.concatenate([vm[:n], um[:n], vm[n:], um[n:]], axis=0)
        y = w["y0"][...] + _dot(gr, vum.astype(BF16))
        yield

        zp = []
        for s in range(nseq):
            zp += [seq_rows(w["u"], s), seq_rows(w["v"], s)]
        z_t = jnp.concatenate(zp, axis=0).T.astype(BF16)
        for g0 in range(0, nseq, grp):
            cols = []
            for e in range(grp):
                gs = jnp.concatenate([seq_rows(w["bh"], g0 + e), seq_rows(w["kh"], g0 + e)],
                                     axis=0).astype(BF16)
                blk = [gs]
                if e > 0:
                    blk = [jnp.zeros((2 * cp * e, LANES), BF16)] + blk
                if e < grp - 1:
                    blk = blk + [jnp.zeros((2 * cp * (grp - 1 - e), LANES), BF16)]
                cols.append(jnp.concatenate(blk, axis=0))
            lhs = z_t[:, (g0 // grp) * LANES:(g0 // grp + 1) * LANES]
            delta = _dot(lhs, jnp.concatenate(cols, axis=1))
            for e in range(grp):
                s = g0 + e
                dl = delta[:, e * LANES:(e + 1) * LANES]
                s_scr[s, q] = s_scr[s, q] * w_tot[s:s + 1, :] + jnp.where(same_head, dl, 0.0)
        yield

        ql = slice(q * LANES, (q + 1) * LANES)
        yv, rv, kv, vv = y[:nv], r[:nv], k[:nv], v[:nv]
        ysq = yv * yv
        y_hi = yv.astype(BF16)
        q_hi = ysq.astype(BF16)
        stack = jnp.concatenate([y_hi, (yv - y_hi.astype(F32)).astype(BF16),
                                 q_hi, (ysq - q_hi.astype(F32)).astype(BF16),
                                 (rv * kv * rk_ref[:, ql]).astype(BF16)], axis=0)
        hs = _dot(stack, ones) * (1.0 / HEAD_DIM)
        mu = hs[:nv] + hs[nv:2 * nv]
        var = hs[2 * nv:3 * nv] + hs[3 * nv:4 * nv] - mu * mu
        yn = (yv - mu) * lax.rsqrt(var + GN_EPS) * gnw_ref[:, ql] + gnb_ref[:, ql]
        bonus = hs[4 * nv:] * float(HEAD_DIM) * vv
        y_ref[pl.ds(c * tchunk, tchunk), :, ql] = (yn + bonus).reshape(tchunk, nseq, LANES)

    def chunk(c, carry):
        stages = [chunk_pair(c, q) for q in range(npl)]
        while stages:
            alive = []
            for g in stages:
                if next(g, "done") is None:
                    alive.append(g)
            stages = alive
        return carry

    lax.fori_loop(0, nchunks, chunk, 0)

    @pl.when(pl.program_id(2) == pl.num_programs(2) - 1)
    def _():
        for s in range(nseq):
            for q in range(npl):
                sout_ref[s, 2 * q] = s_scr[s, q, :HEAD_DIM, :HEAD_DIM]
                sout_ref[s, 2 * q + 1] = s_scr[s, q, HEAD_DIM:, HEAD_DIM:]


def _wkv_call(r, lw, k, v, kk, b, states, layer, p, nb, nt):
    nslab, _, slab = r.shape
    npl = slab // LANES
    ngroup = nb // SUBLANES
    tchunk = min(WKV_CHUNK, nt)
    tb = min(WKV_BLOCK_ELEMS // (SUBLANES * slab), nt)
    nchunks = tb // tchunk
    assert nt % tb == 0 and tb % tchunk == 0 and nb % SUBLANES == 0
    view = lambda a: a.reshape(nslab, nt, ngroup, SUBLANES, slab)
    act_spec = pl.BlockSpec((None, tb, None, SUBLANES, slab), lambda g, p, t: (p, t, g, 0, 0))
    hps = 2 * npl
    st_in_spec = pl.BlockSpec((None, SUBLANES, hps, HEAD_DIM, HEAD_DIM),
                              lambda g, p, t: (layer, g, p, 0, 0))
    st_out_spec = pl.BlockSpec((SUBLANES, hps, HEAD_DIM, HEAD_DIM), lambda g, p, t: (g, p, 0, 0))
    par_spec = pl.BlockSpec((None, 1, slab), lambda g, p, t: (layer, 0, p))
    n = max(tchunk, 8) * SUBLANES
    y, s_out = pl.pallas_call(
        functools.partial(_wkv_kernel, tchunk, nchunks),
        grid=(ngroup, nslab, nt // tb),
        in_specs=[act_spec] * 6 + [st_in_spec] + [par_spec] * 3,
        out_specs=[act_spec, st_out_spec],
        out_shape=[jax.ShapeDtypeStruct((nslab, nt, ngroup, SUBLANES, slab), F32),
                   jax.ShapeDtypeStruct(states.shape[1:], F32)],
        scratch_shapes=[pltpu.VMEM((SUBLANES, npl, LANES, LANES), F32)]
        + [pltpu.VMEM((n, LANES), F32)] * (len(_WKV_SLOTS) * npl),
        compiler_params=pltpu.CompilerParams(
            dimension_semantics=("arbitrary", "arbitrary", "arbitrary"),
            vmem_limit_bytes=VMEM_LIMIT),
        name="wkv7",
    )(view(r), view(lw), view(k), view(v), view(kk), view(b), states, p['r_k'], p['gn_w'], p['gn_b'])
    return y.reshape(nslab, nt * nb, slab), s_out


def _ffn(x, gffn, w1_ref, w2_ref):
    h = _rmsnorm(x, gffn).astype(BF16)
    dff = w1_ref.shape[1]
    step = min(dff, 1024)
    acc = x
    for c in range(dff // step):
        hh = _dot(h, w1_ref[:, c * step:(c + 1) * step])
        hh = jnp.square(jnp.maximum(hh, 0.0)).astype(BF16)
        acc = acc + _dot(hh, w2_ref[c * step:(c + 1) * step, :])
    return acc


def _attn_ffn_kernel(final_norm, nb, x_ref, y_ref, g_ref, wo_ref, gffn_ref, w1_ref, w2_ref, gfin_ref,
                     o_ref, xs_scr, os_scr):
    nslab = y_ref.shape[0]
    y = jnp.concatenate([y_ref[q] for q in range(nslab)], axis=1)
    x = _rows_in(x_ref, xs_scr, nb) + _dot((y * g_ref[...]).astype(BF16), wo_ref[...])
    out = _ffn(x, gffn_ref[...], w1_ref, w2_ref)
    if final_norm:
        out = _rmsnorm(out, gfin_ref[...])
    _rows_out(o_ref, os_scr, out, nb)


def _io_scratch(tm, d, x, batch_major_out):
    full, tiny = (d // LANES, tm, LANES), (1, SUBLANES, LANES)
    return [pltpu.VMEM(full if x.ndim == 3 else tiny, F32),
            pltpu.VMEM(full if batch_major_out else tiny, F32)]


def _out_x(rows, d, tm, nb, batch_major_out):
    if batch_major_out:
        return (jax.ShapeDtypeStruct((nb, rows // nb, d), F32),
                pl.BlockSpec((nb, tm // nb, d), lambda i: (0, i, 0)))
    return jax.ShapeDtypeStruct((rows, d), F32), pl.BlockSpec((tm, d), lambda i: (i, 0))


def _attn_ffn_call(x, y, g, p, i, j, final_norm, batch_major_out, nb, tm):
    rows, d = g.shape
    nslab, _, slab = y.shape
    out_sds, out_spec = _out_x(rows, d, tm, nb, batch_major_out)
    ops = [(x, _x_spec(x, tm, nb)), (y, pl.BlockSpec((nslab, tm, slab), lambda i_: (0, i_, 0))),
           (g, pl.BlockSpec((tm, d), lambda i_: (i_, 0))), _pick(p['w_o'], j), _pick(p['norm_ffn'], i),
           _pick(p['ffn_w1'], i), _pick(p['ffn_w2'], i), _pick(p['norm_final'])]
    return pl.pallas_call(
        functools.partial(_attn_ffn_kernel, final_norm, nb),
        grid=(rows // tm,),
        in_specs=[o[1] for o in ops],
        out_specs=out_spec,
        out_shape=out_sds,
        scratch_shapes=_io_scratch(tm, d, x, batch_major_out),
        compiler_params=pltpu.CompilerParams(
            dimension_semantics=("arbitrary",), vmem_limit_bytes=VMEM_LIMIT),
        name="attn_out_ffn",
    )(*[o[0] for o in ops])


def _pool_ffn_kernel(final_norm, nb, start_pos, x_ref, buf_ref, gmix_ref, pw_ref, ps_ref,
                     gffn_ref, w1_ref, w2_ref, gfin_ref, o_ref, buf_out, carry, xs_scr, os_scr):
    d = gmix_ref.shape[-1]
    tm = x_ref.shape[0] if len(x_ref.shape) == 2 else nb * x_ref.shape[1]
    ngrp = len(POOL_WINDOWS)
    gw = d // ngrp

    @pl.when(pl.program_id(0) == 0)
    def _():
        carry[...] = buf_ref[...]

    x = _rows_in(x_ref, xs_scr, nb)
    h = _rmsnorm(x, gmix_ref[...])
    ext = jnp.concatenate([carry[...], h], axis=0)
    carry[...] = ext[tm:]
    buf_out[...] = ext[tm:]

    rowi = lax.broadcasted_iota(jnp.int32, (tm, gw), 0) + pl.program_id(0) * tm
    pos = start_pos + jnp.right_shift(rowi, int(math.log2(nb)))
    ys = []
    for gi, w in enumerate(POOL_WINDOWS):
        sl = slice(gi * gw, (gi + 1) * gw)
        s = ext[:, sl]
        span = 1
        while span < w:
            s = s[span * nb:] + s[: s.shape[0] - span * nb]
            span *= 2
        s = s[s.shape[0] - tm:]
        cnt = jnp.minimum(pos + 1, w).astype(F32)
        diff = (s / cnt - h[:, sl]).astype(BF16)
        ys.append(_dot(diff, pw_ref[gi]))
    x = x + jnp.concatenate(ys, axis=1) * ps_ref[...]
    out = _ffn(x, gffn_ref[...], w1_ref, w2_ref)
    if final_norm:
        out = _rmsnorm(out, gfin_ref[...])
    _rows_out(o_ref, os_scr, out, nb)


def _pool_ffn_call(x, buf, p, i, j, final_norm, batch_major_out, nb, start_pos, tm):
    d = x.shape[-1]
    rows = x.shape[0] if x.ndim == 2 else x.shape[0] * x.shape[1]
    assert nb & (nb - 1) == 0
    out_sds, out_spec = _out_x(rows, d, tm, nb, batch_major_out)
    ops = [(x, _x_spec(x, tm, nb)), (buf, _const_spec(buf.shape)), _pick(p['norm_mix'], i),
           _pick(p['pool_w'], j), _pick(p['pool_scale'], j), _pick(p['norm_ffn'], i),
           _pick(p['ffn_w1'], i), _pick(p['ffn_w2'], i), _pick(p['norm_final'])]
    return pl.pallas_call(
        functools.partial(_pool_ffn_kernel, final_norm, nb, start_pos),
        grid=(rows // tm,),
        in_specs=[o[1] for o in ops],
        out_specs=[out_spec, _const_spec(buf.shape)],
        out_shape=[out_sds, jax.ShapeDtypeStruct(buf.shape, F32)],
        scratch_shapes=[pltpu.VMEM(buf.shape, F32)] + _io_scratch(tm, d, x, batch_major_out),
        compiler_params=pltpu.CompilerParams(
            dimension_semantics=("arbitrary",), vmem_limit_bytes=VMEM_LIMIT),
        name="pool_ffn",
    )(*[o[0] for o in ops])


def _trunk(x, start_pos, shift_states, wkv_states, pool_bufs, p):
    nb, nt, d = x.shape
    depth = p['norm_mix'].shape[0]
    rows = nb * nt
    tm = min(512, rows)
    assert rows % tm == 0 and tm % nb == 0
    batch_major = nb == SUBLANES
    xt = x if batch_major else x.transpose(1, 0, 2).reshape(rows, d)
    new_shift, new_wkv, new_pool = [], [], []
    v_first = None
    for i in range(depth):
        j = i // 2
        last = i == depth - 1
        bm_out = batch_major and last
        if i % 2 == 0:
            r, lw, k, v, kk, b, g, s_shift = _pre_call(xt, p, i, j, shift_states, v_first, nb, tm)
            if j == 0:
                v_first = v
            y, s_wkv = _wkv_call(r, lw, k, v, kk, b, wkv_states, j, p, nb, nt)
            new_shift.append(s_shift)
            new_wkv.append(s_wkv)
            xt = _attn_ffn_call(xt, y, g, p, i, j, last, bm_out, nb, tm)
        else:
            buf = pool_bufs[j].transpose(1, 0, 2).reshape(POOL_BUF * nb, d)
            xt, s_buf = _pool_ffn_call(xt, buf, p, i, j, last, bm_out, nb, start_pos, tm)
            new_pool.append(s_buf.reshape(POOL_BUF, nb, d).transpose(1, 0, 2))
    y = xt if batch_major else xt.reshape(nt, nb, d).transpose(1, 0, 2)
    return y, jnp.stack(new_wkv), jnp.stack(new_shift), jnp.stack(new_pool)


def kernel(x_prompt, x_sample, state_wkv, state_shift, state_pool, norm_mix, norm_ffn, norm_final,
           rwkv_mix, rwkv_w_rkv, rwkv_w_o, rwkv_w0, rwkv_w1, rwkv_w2, rwkv_a0, rwkv_a1, rwkv_a2,
           rwkv_v0, rwkv_v1, rwkv_v2, rwkv_g1, rwkv_g2, rwkv_k_k, rwkv_k_a, rwkv_r_k,
           rwkv_gn_w, rwkv_gn_b, pool_w, pool_scale, ffn_w1, ffn_w2):
    bf = lambda a: a.astype(BF16)
    vec = lambda a: a.reshape(a.shape[0], 1, -1)
    p = {'norm_mix': vec(norm_mix), 'norm_ffn': vec(norm_ffn), 'norm_final': norm_final.reshape(1, -1),
         'mix': rwkv_mix, 'w_rkv': bf(rwkv_w_rkv), 'w_o': bf(rwkv_w_o), 'w0': vec(rwkv_w0),
         'w1': bf(rwkv_w1), 'w2': bf(rwkv_w2), 'a0': vec(rwkv_a0), 'a1': bf(rwkv_a1),
         'a2': bf(rwkv_a2), 'v0': vec(rwkv_v0), 'v1': bf(rwkv_v1), 'v2': bf(rwkv_v2),
         'g1': bf(rwkv_g1), 'g2': bf(rwkv_g2), 'k_k': vec(rwkv_k_k), 'k_a': vec(rwkv_k_a),
         'r_k': vec(rwkv_r_k), 'gn_w': vec(rwkv_gn_w), 'gn_b': vec(rwkv_gn_b), 'pool_w': bf(pool_w),
         'pool_scale': vec(pool_scale), 'ffn_w1': bf(ffn_w1), 'ffn_w2': bf(ffn_w2)}
    dt = x_prompt.dtype
    nb, _, d = x_prompt.shape
    n_rwkv = state_wkv.shape[0]
    n_pool = state_pool.shape[0]
    nh = d // HEAD_DIM
    z_shift = jnp.zeros((n_rwkv, nb, d), dt)
    z_wkv = jnp.zeros((n_rwkv, nb, nh, HEAD_DIM, HEAD_DIM), dt)
    z_pool = jnp.zeros((n_pool, nb, POOL_BUF, d), dt)
    y_p, wkv_p, shift_p, pool_p = _trunk(x_prompt, 0, z_shift, z_wkv, z_pool, p)
    y_s, wkv_s, shift_s, pool_s = _trunk(x_sample, PAST_LEN, state_shift, state_wkv, state_pool, p)
    return (y_p, y_s, wkv_p, shift_p, pool_p, wkv_s, shift_s, pool_s)
```

```python
import functools
import math

import jax
import jax.numpy as jnp
from jax import lax
from jax.experimental import pallas as pl
from jax.experimental.pallas import tpu as pltpu

HEAD_DIM = 64
LANES = 128
SUBLANES = 8
WKV_LANES = 512
WKV_BLOCK_ELEMS = 8 * 128 * 512
POOL_WINDOWS = (2, 4, 8, 16)
POOL_BUF = max(POOL_WINDOWS) - 1
PAST_LEN = 16384
NORM_EPS = 1e-6
GN_EPS = 64e-5
L2_EPS = 1e-12
WKV_CHUNK = 16
VMEM_LIMIT = 56 * 1024 * 1024

BF16 = jnp.bfloat16
F32 = jnp.float32


def _dot(a, b):
    return jnp.dot(a, b, preferred_element_type=F32)


def _dot_nt(a, b):
    return lax.dot_general(a, b, (((1,), (1,)), ((), ())), preferred_element_type=F32)


def _sigmoid(x):
    return 1.0 / (1.0 + jnp.exp(-x))


def _rmsnorm(x, g):
    ms = jnp.mean(x * x, axis=-1, keepdims=True)
    return x * lax.rsqrt(ms + NORM_EPS) * g


def _head_ones():
    r = lax.broadcasted_iota(jnp.int32, (LANES, LANES), 0) // HEAD_DIM
    c = lax.broadcasted_iota(jnp.int32, (LANES, LANES), 1) // HEAD_DIM
    return jnp.where(r == c, 1.0, 0.0).astype(BF16)


def _head_sum_bf16(p, ones):
    return _dot(p.astype(BF16), ones)


def _head_sum(p, ones):
    hi = p.astype(BF16)
    lo = (p - hi.astype(F32)).astype(BF16)
    return _dot(hi, ones) + _dot(lo, ones)


def _const_spec(shape):
    n = len(shape)
    return pl.BlockSpec(shape, lambda *_: (0,) * n)


def _pick(arr, *idx):
    shape = (None,) * len(idx) + arr.shape[len(idx):]
    at = tuple(idx) + (0,) * (arr.ndim - len(idx))
    return arr, pl.BlockSpec(shape, lambda *_: at)


def _rows_in(x_ref, xs_scr, nb):
    if len(x_ref.shape) == 2:
        return x_ref[...]
    tt = x_ref.shape[1]
    nblk = xs_scr.shape[0]
    for c in range(nblk):
        for b in range(nb):
            xs_scr[c, pl.ds(b, tt, stride=nb), :] = x_ref[b, :, c * LANES:(c + 1) * LANES]
    return jnp.concatenate([xs_scr[c] for c in range(nblk)], axis=1)


def _rows_out(o_ref, os_scr, val, nb):
    if len(o_ref.shape) == 2:
        o_ref[...] = val
        return
    tt = o_ref.shape[1]
    for c in range(os_scr.shape[0]):
        os_scr[c] = val[:, c * LANES:(c + 1) * LANES]
        for b in range(nb):
            o_ref[b, :, c * LANES:(c + 1) * LANES] = os_scr[c, pl.ds(b, tt, stride=nb), :]


def _x_spec(x, tm, nb):
    if x.ndim == 2:
        return pl.BlockSpec((tm, x.shape[1]), lambda i: (i, 0))
    return pl.BlockSpec((nb, tm // nb, x.shape[2]), lambda i: (0, i, 0))


def _slab_lanes(d):
    return WKV_LANES if d % WKV_LANES == 0 else LANES


def _pre_kernel(has_vres, nb, *refs):
    it = iter(refs)
    x_ref = next(it); shift_ref = next(it); gmix_ref = next(it); mix_ref = next(it)
    wr_ref = next(it); wk_ref = next(it); wv_ref = next(it)
    w0_ref = next(it); w1_ref = next(it); w2_ref = next(it)
    a0_ref = next(it); a1_ref = next(it); a2_ref = next(it)
    if has_vres:
        v0_ref = next(it); v1_ref = next(it); v2_ref = next(it); vfirst_ref = next(it)
    g1_ref = next(it); g2_ref = next(it); kk_ref = next(it); ka_ref = next(it)
    r_out = next(it); lw_out = next(it); k_out = next(it); v_out = next(it)
    kk_out = next(it); b_out = next(it); g_out = next(it); shift_out = next(it)
    carry = next(it); xs_scr = next(it)

    tm, d = g_out.shape
    nslab, _, slab = r_out.shape

    @pl.when(pl.program_id(0) == 0)
    def _():
        carry[...] = shift_ref[...]

    h = _rmsnorm(_rows_in(x_ref, xs_scr, nb), gmix_ref[...])
    if tm > nb:
        hp = jnp.concatenate([carry[...], h[: tm - nb]], axis=0)
    else:
        hp = carry[...]
    carry[...] = h[tm - nb:]
    shift_out[...] = h[tm - nb:]
    dx = hp - h

    def mixed(i):
        return (h + dx * mix_ref[i:i + 1, :]).astype(BF16)

    xv = mixed(2)
    r = _dot(mixed(0), wr_ref[...])
    k = _dot(mixed(1), wk_ref[...])
    v = _dot(xv, wv_ref[...])
    wpre = w0_ref[...] + _dot(jnp.tanh(_dot(mixed(3), w1_ref[...])).astype(BF16), w2_ref[...])
    lw = -math.exp(-0.5) * _sigmoid(wpre)
    if has_vres:
        vfirst = jnp.concatenate([vfirst_ref[p] for p in range(nslab)], axis=1)
        gate = _sigmoid(v0_ref[...] + _dot(_dot(xv, v1_ref[...]).astype(BF16), v2_ref[...]))
        v = v + (vfirst - v) * gate
    a = _sigmoid(a0_ref[...] + _dot(_dot(mixed(4), a1_ref[...]).astype(BF16), a2_ref[...]))
    g = _dot(_sigmoid(_dot(mixed(5), g1_ref[...])).astype(BF16), g2_ref[...])
    g_out[...] = g

    ones = _head_ones()
    kk = k * kk_ref[...]
    k2 = k * (1.0 + (a - 1.0) * ka_ref[...])
    kkn = []
    for p in range(d // LANES):
        kkp = kk[:, p * LANES:(p + 1) * LANES]
        nrm = jnp.sqrt(_head_sum_bf16(kkp * kkp, ones))
        kkn.append(kkp / jnp.maximum(nrm, L2_EPS))
    kkn = jnp.concatenate(kkn, axis=1)
    bb = kkn * a
    for p in range(nslab):
        sl = slice(p * slab, (p + 1) * slab)
        r_out[p] = r[:, sl]
        lw_out[p] = lw[:, sl]
        k_out[p] = k2[:, sl]
        v_out[p] = v[:, sl]
        kk_out[p] = kkn[:, sl]
        b_out[p] = bb[:, sl]


def _pre_call(x, p, i, j, shift_states, vfirst, nb, tm):
    d = x.shape[-1]
    rows = x.shape[0] if x.ndim == 2 else x.shape[0] * x.shape[1]
    slab = _slab_lanes(d)
    nslab = d // slab
    has_vres = j > 0
    sm_spec = pl.BlockSpec((nslab, tm, slab), lambda i_: (0, i_, 0))
    ops = [(x, _x_spec(x, tm, nb)), _pick(shift_states, j), _pick(p['norm_mix'], i), _pick(p['mix'], j),
           _pick(p['w_rkv'], j, 0), _pick(p['w_rkv'], j, 1), _pick(p['w_rkv'], j, 2),
           _pick(p['w0'], j), _pick(p['w1'], j), _pick(p['w2'], j),
           _pick(p['a0'], j), _pick(p['a1'], j), _pick(p['a2'], j)]
    if has_vres:
        ops += [_pick(p['v0'], j - 1), _pick(p['v1'], j - 1), _pick(p['v2'], j - 1), (vfirst, sm_spec)]
    ops += [_pick(p['g1'], j), _pick(p['g2'], j), _pick(p['k_k'], j), _pick(p['k_a'], j)]
    sm = jax.ShapeDtypeStruct((nslab, rows, slab), F32)
    out_shape = [sm] * 6 + [jax.ShapeDtypeStruct((rows, d), F32), jax.ShapeDtypeStruct((nb, d), F32)]
    out_specs = [sm_spec] * 6 + [pl.BlockSpec((tm, d), lambda i_: (i_, 0)), _const_spec((nb, d))]
    return pl.pallas_call(
        functools.partial(_pre_kernel, has_vres, nb),
        grid=(rows // tm,),
        in_specs=[o[1] for o in ops],
        out_specs=out_specs,
        out_shape=out_shape,
        scratch_shapes=[pltpu.VMEM((nb, d), F32), pltpu.VMEM((d // LANES, tm, LANES), F32)],
        compiler_params=pltpu.CompilerParams(
            dimension_semantics=("arbitrary",), vmem_limit_bytes=VMEM_LIMIT),
        name="rwkv_pre",
    )(*[o[0] for o in ops])


_WKV_SLOTS = ("at", "rt", "u", "v", "bh", "kh", "x0", "y0")


def _wkv_kernel(tchunk, nchunks, has_alias, layer, r_ref, lw_ref, k_ref, v_ref, kk_ref, b_ref, s0_ref,
                rk_ref, gnw_ref, gnb_ref, *rest):
    y_ref, sout_ref, s_scr, *work = rest[1:] if has_alias else rest
    nseq = SUBLANES
    npl = s_scr.shape[1]
    cp = max(tchunk, 8)
    n = cp * nseq
    nv = tchunk * nseq
    grp = LANES // (2 * cp)
    nslot = len(_WKV_SLOTS)
    scr = [dict(zip(_WKV_SLOTS, work[q * nslot:(q + 1) * nslot])) for q in range(npl)]

    @pl.when(pl.program_id(2) == 0)
    def _():
        zero = jnp.zeros((HEAD_DIM, HEAD_DIM), F32)
        for s in range(nseq):
            for q in range(npl):
                top = jnp.concatenate([s0_ref[s, 2 * q], zero], axis=1)
                bot = jnp.concatenate([zero, s0_ref[s, 2 * q + 1]], axis=1)
                s_scr[s, q] = jnp.concatenate([top, bot], axis=0)

    ones = _head_ones()
    hr = lax.broadcasted_iota(jnp.int32, (LANES, LANES), 0) // HEAD_DIM
    hc = lax.broadcasted_iota(jnp.int32, (LANES, LANES), 1) // HEAD_DIM
    same_head = hr == hc
    lane_head = lax.broadcasted_iota(jnp.int32, (1, LANES), 1) // HEAD_DIM
    head_f32 = [jnp.where(lane_head == hh, 1.0, 0.0) for hh in range(2)]
    head_bf16 = [m.astype(BF16) for m in head_f32]
    ri = lax.broadcasted_iota(jnp.int32, (n, 2 * n), 0)
    ci = lax.broadcasted_iota(jnp.int32, (n, 2 * n), 1) % n
    same_seq = (ri % nseq) == (ci % nseq)
    tri_incl = jnp.where(same_seq & (ci // nseq <= ri // nseq), 1.0, 0.0)
    tri_strict = jnp.where(same_seq & (ci // nseq < ri // nseq), 1.0, 0.0)[:, :n]

    def load(ref, c, q):
        val = ref[pl.ds(c * tchunk, tchunk), :, q * LANES:(q + 1) * LANES].reshape(nv, LANES)
        if n > nv:
            val = jnp.concatenate([val, jnp.zeros((n - nv, LANES), F32)], axis=0)
        return val

    def tile_rows(slab, reps):
        return slab if reps == 1 else jnp.concatenate([slab] * reps, axis=0)

    def seq_rows(ref, s):
        return ref[pl.ds(s, cp, stride=nseq), :]

    def phase_a(c, q, st):
        w = scr[q]
        r = load(r_ref, c, q); lw = load(lw_ref, c, q); k = load(k_ref, c, q)
        v = load(v_ref, c, q); kk = load(kk_ref, c, q); b = load(b_ref, c, q)

        acc = jnp.zeros((nseq, LANES), F32)
        cums = []
        for t in range(cp):
            acc = acc + lw[t * nseq:(t + 1) * nseq]
            cums.append(acc)
        cum = jnp.concatenate(cums, axis=0)
        w_inc = jnp.exp(cum)
        w_inv = jnp.exp(-cum)
        w_exc = jnp.concatenate([jnp.ones((nseq, LANES), F32), w_inc[: n - nseq]], axis=0)
        w_tot = w_inc[n - nseq:]
        at = -kk * w_exc
        rt = r * w_inc
        kt = k * w_inv
        bt = b * w_inv
        w_tot_rows = tile_rows(w_tot, cp)
        st.update(r=r, k=k, v=v, at=at, rt=rt, w_tot=w_tot, bh=bt * w_tot_rows, kh=kt * w_tot_rows)

        at16 = at.astype(BF16)
        rt16 = rt.astype(BF16)
        kt16 = kt.astype(BF16)
        kb16 = jnp.concatenate([kt16, bt.astype(BF16)], axis=0)
        gram_a = _dot_nt(jnp.concatenate([at16 * head_bf16[0], at16 * head_bf16[1]], axis=0), kt16)
        gram_r = _dot_nt(jnp.concatenate([rt16 * head_bf16[0], rt16 * head_bf16[1]], axis=0), kb16)
        st["ga"] = jnp.concatenate([gram_a[:n] * tri_strict, gram_a[n:] * tri_strict], axis=1).astype(BF16)
        st["gr"] = jnp.concatenate([gram_r[:n] * tri_incl, gram_r[n:] * tri_incl], axis=1).astype(BF16)
        st["vm"] = jnp.concatenate([v * head_f32[0], v * head_f32[1]], axis=0)
        yield

        prods = [tile_rows(at[t * nseq:(t + 1) * nseq], t) * bt[:t * nseq] for t in range(1, cp)]
        st["coef"] = _head_sum_bf16(jnp.concatenate(prods, axis=0), ones)

    def phase_b(c, q, st):
        w = scr[q]
        w["at"][...] = st["at"]
        w["rt"][...] = st["rt"]
        w["v"][...] = st["v"]
        w["bh"][...] = st["bh"]
        w["kh"][...] = st["kh"]
        for s in range(nseq):
            lhs = jnp.concatenate([seq_rows(w["at"], s), seq_rows(w["rt"], s)], axis=0).astype(BF16)
            out = _dot_nt(lhs, s_scr[s, q].astype(BF16))
            w["x0"][pl.ds(s, cp, stride=nseq), :] = out[:cp]
            w["y0"][pl.ds(s, cp, stride=nseq), :] = out[cp:]
        yield

        x = w["x0"][...] + _dot(st["ga"], st["vm"].astype(BF16))
        yield

        coef = st["coef"]
        us = [x[0:nseq]]
        off = 0
        for t in range(1, cp):
            u_t = x[t * nseq:(t + 1) * nseq]
            for j in range(t):
                u_t = u_t + coef[off + j * nseq:off + (j + 1) * nseq] * us[j]
            off += t * nseq
            us.append(u_t)
        u = jnp.concatenate(us, axis=0)
        w["u"][...] = u
        yield

        vm = st["vm"]
        um = jnp.concatenate([u * head_f32[0], u * head_f32[1]], axis=0)
        vum = jnp.concatenate([vm[:n], um[:n], vm[n:], um[n:]], axis=0)
        st["y"] = w["y0"][...] + _dot(st["gr"], vum.astype(BF16))
        yield

        w_tot = st["w_tot"]
        zp = []
        for s in range(nseq):
            zp += [seq_rows(w["u"], s), seq_rows(w["v"], s)]
        z_t = jnp.concatenate(zp, axis=0).T.astype(BF16)
        for g0 in range(0, nseq, grp):
            cols = []
            for e in range(grp):
                gs = jnp.concatenate([seq_rows(w["bh"], g0 + e), seq_rows(w["kh"], g0 + e)],
                                     axis=0).astype(BF16)
                blk = [gs]
                if e > 0:
                    blk = [jnp.zeros((2 * cp * e, LANES), BF16)] + blk
                if e < grp - 1:
                    blk = blk + [jnp.zeros((2 * cp * (grp - 1 - e), LANES), BF16)]
                cols.append(jnp.concatenate(blk, axis=0))
            lhs = z_t[:, (g0 // grp) * LANES:(g0 // grp + 1) * LANES]
            delta = _dot(lhs, jnp.concatenate(cols, axis=1))
            for e in range(grp):
                s = g0 + e
                dl = delta[:, e * LANES:(e + 1) * LANES]
                s_scr[s, q] = s_scr[s, q] * w_tot[s:s + 1, :] + jnp.where(same_head, dl, 0.0)

    def phase_c(c, q, st):
        ql = slice(q * LANES, (q + 1) * LANES)
        yv, rv, kv, vv = st["y"][:nv], st["r"][:nv], st["k"][:nv], st["v"][:nv]
        hs = _head_sum_bf16(jnp.concatenate([yv, rv * kv * rk_ref[:, ql]], axis=0), ones)
        dlt = yv - hs[:nv] * (1.0 / HEAD_DIM)
        bonus = hs[nv:] * vv
        yield
        var = _head_sum_bf16(dlt * dlt, ones) * (1.0 / HEAD_DIM)
        yn = dlt * lax.rsqrt(var + GN_EPS) * gnw_ref[:, ql] + gnb_ref[:, ql]
        y_ref[pl.ds(c * tchunk, tchunk), :, ql] = (yn + bonus).reshape(tchunk, nseq, LANES)

    def advance(gens, nstages=1):
        for _ in range(nstages):
            for g in gens:
                next(g, None)

    def drain(gens):
        for _ in range(8):
            advance(gens)

    def chunk_group(cs):
        sts = [[{} for _ in range(npl)] for _ in cs]
        ga = [[phase_a(c, q, sts[i][q]) for q in range(npl)] for i, c in enumerate(cs)]
        gb = [[phase_b(c, q, sts[i][q]) for q in range(npl)] for i, c in enumerate(cs)]
        gc = [[phase_c(c, q, sts[i][q]) for q in range(npl)] for i, c in enumerate(cs)]
        drain(ga[0])
        for i in range(len(cs)):
            fill = []
            if i + 1 < len(cs):
                fill.append(ga[i + 1])
            if i > 0:
                fill.append(gc[i - 1])
            for _ in range(5):
                advance(gb[i])
                for f in fill:
                    advance(f)
            for f in fill:
                drain(f)
        drain(gc[-1])

    unroll = 2 if nchunks % 2 == 0 else 1

    def chunk(i, carry):
        chunk_group([i * unroll + e for e in range(unroll)])
        return carry

    lax.fori_loop(0, nchunks // unroll, chunk, 0)

    @pl.when(pl.program_id(2) == pl.num_programs(2) - 1)
    def _():
        out = sout_ref if has_alias else sout_ref.at[layer]
        if not has_alias:
            for other in range(sout_ref.shape[0]):
                if other != layer:
                    sout_ref[other] = jnp.zeros(sout_ref.shape[1:], F32)
        for s in range(nseq):
            for q in range(npl):
                out[s, 2 * q] = s_scr[s, q, :HEAD_DIM, :HEAD_DIM]
                out[s, 2 * q + 1] = s_scr[s, q, HEAD_DIM:, HEAD_DIM:]


def _wkv_call(r, lw, k, v, kk, b, states, new_states, layer, p, nb, nt):
    nslab, _, slab = r.shape
    npl = slab // LANES
    ngroup = nb // SUBLANES
    tchunk = min(WKV_CHUNK, nt)
    tb = min(WKV_BLOCK_ELEMS // (SUBLANES * slab), nt)
    nchunks = tb // tchunk
    assert nt % tb == 0 and tb % tchunk == 0 and nb % SUBLANES == 0
    view = lambda a: a.reshape(nslab, nt, ngroup, SUBLANES, slab)
    act_spec = pl.BlockSpec((None, tb, None, SUBLANES, slab), lambda g, p, t: (p, t, g, 0, 0))
    hps = 2 * npl
    st_in_spec = pl.BlockSpec((None, SUBLANES, hps, HEAD_DIM, HEAD_DIM),
                              lambda g, p, t: (layer, g, p, 0, 0))
    has_alias = new_states is not None
    if has_alias:
        st_out_spec = pl.BlockSpec((None, SUBLANES, hps, HEAD_DIM, HEAD_DIM),
                                   lambda g, p, t: (layer, g, p, 0, 0))
    else:
        st_out_spec = pl.BlockSpec((states.shape[0], SUBLANES, hps, HEAD_DIM, HEAD_DIM),
                                   lambda g, p, t: (0, g, p, 0, 0))
    extra_in = [new_states] if has_alias else []
    extra_spec = [pl.BlockSpec(memory_space=pl.ANY)] if has_alias else []
    par_spec = pl.BlockSpec((None, 1, slab), lambda g, p, t: (layer, 0, p))
    n = max(tchunk, 8) * SUBLANES
    y, s_out = pl.pallas_call(
        functools.partial(_wkv_kernel, tchunk, nchunks, has_alias, layer),
        grid=(ngroup, nslab, nt // tb),
        in_specs=[act_spec] * 6 + [st_in_spec] + [par_spec] * 3 + extra_spec,
        out_specs=[act_spec, st_out_spec],
        out_shape=[jax.ShapeDtypeStruct((nslab, nt, ngroup, SUBLANES, slab), F32),
                   jax.ShapeDtypeStruct(states.shape, F32)],
        input_output_aliases={10: 1} if has_alias else {},
        scratch_shapes=[pltpu.VMEM((SUBLANES, npl, LANES, LANES), F32)]
        + [pltpu.VMEM((n, LANES), F32)] * (len(_WKV_SLOTS) * npl),
        compiler_params=pltpu.CompilerParams(
            dimension_semantics=("arbitrary", "arbitrary", "arbitrary"),
            vmem_limit_bytes=VMEM_LIMIT),
        name="wkv7",
    )(view(r), view(lw), view(k), view(v), view(kk), view(b), states, p['r_k'], p['gn_w'], p['gn_b'],
      *extra_in)
    return y.reshape(nslab, nt * nb, slab), s_out


def _ffn(x, gffn, w1_ref, w2_ref):
    h = _rmsnorm(x, gffn).astype(BF16)
    dff = w1_ref.shape[1]
    step = min(dff, 1024)
    acc = x
    for c in range(dff // step):
        hh = _dot(h, w1_ref[:, c * step:(c + 1) * step])
        hh = jnp.square(jnp.maximum(hh, 0.0)).astype(BF16)
        acc = acc + _dot(hh, w2_ref[c * step:(c + 1) * step, :])
    return acc


def _attn_ffn_kernel(final_norm, nb, x_ref, y_ref, g_ref, wo_ref, gffn_ref, w1_ref, w2_ref, gfin_ref,
                     o_ref, xs_scr, os_scr):
    nslab = y_ref.shape[0]
    y = jnp.concatenate([y_ref[q] for q in range(nslab)], axis=1)
    x = _rows_in(x_ref, xs_scr, nb) + _dot((y * g_ref[...]).astype(BF16), wo_ref[...])
    out = _ffn(x, gffn_ref[...], w1_ref, w2_ref)
    if final_norm:
        out = _rmsnorm(out, gfin_ref[...])
    _rows_out(o_ref, os_scr, out, nb)


def _io_scratch(tm, d, x, batch_major_out):
    full, tiny = (d // LANES, tm, LANES), (1, SUBLANES, LANES)
    return [pltpu.VMEM(full if x.ndim == 3 else tiny, F32),
            pltpu.VMEM(full if batch_major_out else tiny, F32)]


def _out_x(rows, d, tm, nb, batch_major_out):
    if batch_major_out:
        return (jax.ShapeDtypeStruct((nb, rows // nb, d), F32),
                pl.BlockSpec((nb, tm // nb, d), lambda i: (0, i, 0)))
    return jax.ShapeDtypeStruct((rows, d), F32), pl.BlockSpec((tm, d), lambda i: (i, 0))


def _attn_ffn_call(x, y, g, p, i, j, final_norm, batch_major_out, nb, tm):
    rows, d = g.shape
    nslab, _, slab = y.shape
    out_sds, out_spec = _out_x(rows, d, tm, nb, batch_major_out)
    ops = [(x, _x_spec(x, tm, nb)), (y, pl.BlockSpec((nslab, tm, slab), lambda i_: (0, i_, 0))),
           (g, pl.BlockSpec((tm, d), lambda i_: (i_, 0))), _pick(p['w_o'], j), _pick(p['norm_ffn'], i),
           _pick(p['ffn_w1'], i), _pick(p['ffn_w2'], i), _pick(p['norm_final'])]
    return pl.pallas_call(
        functools.partial(_attn_ffn_kernel, final_norm, nb),
        grid=(rows // tm,),
        in_specs=[o[1] for o in ops],
        out_specs=out_spec,
        out_shape=out_sds,
        scratch_shapes=_io_scratch(tm, d, x, batch_major_out),
        compiler_params=pltpu.CompilerParams(
            dimension_semantics=("arbitrary",), vmem_limit_bytes=VMEM_LIMIT),
        name="attn_out_ffn",
    )(*[o[0] for o in ops])


def _pool_ffn_kernel(final_norm, nb, start_pos, x_ref, buf_ref, gmix_ref, pw_ref, ps_ref,
                     gffn_ref, w1_ref, w2_ref, gfin_ref, o_ref, buf_out, carry, xs_scr, os_scr):
    d = gmix_ref.shape[-1]
    tm = x_ref.shape[0] if len(x_ref.shape) == 2 else nb * x_ref.shape[1]
    ngrp = len(POOL_WINDOWS)
    gw = d // ngrp

    @pl.when(pl.program_id(0) == 0)
    def _():
        carry[...] = buf_ref[...]

    x = _rows_in(x_ref, xs_scr, nb)
    h = _rmsnorm(x, gmix_ref[...])
    ext = jnp.concatenate([carry[...], h], axis=0)
    carry[...] = ext[tm:]
    buf_out[...] = ext[tm:]

    rowi = lax.broadcasted_iota(jnp.int32, (tm, gw), 0) + pl.program_id(0) * tm
    pos = start_pos + jnp.right_shift(rowi, int(math.log2(nb)))
    ys = []
    for gi, w in enumerate(POOL_WINDOWS):
        sl = slice(gi * gw, (gi + 1) * gw)
        s = ext[:, sl]
        span = 1
        while span < w:
            s = s[span * nb:] + s[: s.shape[0] - span * nb]
            span *= 2
        s = s[s.shape[0] - tm:]
        cnt = jnp.minimum(pos + 1, w).astype(F32)
        diff = (s / cnt - h[:, sl]).astype(BF16)
        ys.append(_dot(diff, pw_ref[gi]))
    x = x + jnp.concatenate(ys, axis=1) * ps_ref[...]
    out = _ffn(x, gffn_ref[...], w1_ref, w2_ref)
    if final_norm:
        out = _rmsnorm(out, gfin_ref[...])
    _rows_out(o_ref, os_scr, out, nb)


def _pool_ffn_call(x, buf, p, i, j, final_norm, batch_major_out, nb, start_pos, tm):
    d = x.shape[-1]
    rows = x.shape[0] if x.ndim == 2 else x.shape[0] * x.shape[1]
    assert nb & (nb - 1) == 0
    out_sds, out_spec = _out_x(rows, d, tm, nb, batch_major_out)
    ops = [(x, _x_spec(x, tm, nb)), (buf, _const_spec(buf.shape)), _pick(p['norm_mix'], i),
           _pick(p['pool_w'], j), _pick(p['pool_scale'], j), _pick(p['norm_ffn'], i),
           _pick(p['ffn_w1'], i), _pick(p['ffn_w2'], i), _pick(p['norm_final'])]
    return pl.pallas_call(
        functools.partial(_pool_ffn_kernel, final_norm, nb, start_pos),
        grid=(rows // tm,),
        in_specs=[o[1] for o in ops],
        out_specs=[out_spec, _const_spec(buf.shape)],
        out_shape=[out_sds, jax.ShapeDtypeStruct(buf.shape, F32)],
        scratch_shapes=[pltpu.VMEM(buf.shape, F32)] + _io_scratch(tm, d, x, batch_major_out),
        compiler_params=pltpu.CompilerParams(
            dimension_semantics=("arbitrary",), vmem_limit_bytes=VMEM_LIMIT),
        name="pool_ffn",
    )(*[o[0] for o in ops])


def _trunk(x, start_pos, shift_states, wkv_states, pool_bufs, p):
    nb, nt, d = x.shape
    depth = p['norm_mix'].shape[0]
    rows = nb * nt
    tm = min(512, rows)
    assert rows % tm == 0 and tm % nb == 0
    batch_major = nb == SUBLANES
    xt = x if batch_major else x.transpose(1, 0, 2).reshape(rows, d)
    new_shift, new_wkv, new_pool = [], None, []
    v_first = None
    for i in range(depth):
        j = i // 2
        last = i == depth - 1
        bm_out = batch_major and last
        if i % 2 == 0:
            r, lw, k, v, kk, b, g, s_shift = _pre_call(xt, p, i, j, shift_states, v_first, nb, tm)
            if j == 0:
                v_first = v
            y, new_wkv = _wkv_call(r, lw, k, v, kk, b, wkv_states, new_wkv, j, p, nb, nt)
            new_shift.append(s_shift)
            xt = _attn_ffn_call(xt, y, g, p, i, j, last, bm_out, nb, tm)
        else:
            buf = pool_bufs[j].transpose(1, 0, 2).reshape(POOL_BUF * nb, d)
            xt, s_buf = _pool_ffn_call(xt, buf, p, i, j, last, bm_out, nb, start_pos, tm)
            new_pool.append(s_buf.reshape(POOL_BUF, nb, d).transpose(1, 0, 2))
    y = xt if batch_major else xt.reshape(nt, nb, d).transpose(1, 0, 2)
    return y, new_wkv, jnp.stack(new_shift), jnp.stack(new_pool)


def kernel(x_prompt, x_sample, state_wkv, state_shift, state_pool, norm_mix, norm_ffn, norm_final,
           rwkv_mix, rwkv_w_rkv, rwkv_w_o, rwkv_w0, rwkv_w1, rwkv_w2, rwkv_a0, rwkv_a1, rwkv_a2,
           rwkv_v0, rwkv_v1, rwkv_v2, rwkv_g1, rwkv_g2, rwkv_k_k, rwkv_k_a, rwkv_r_k,
           rwkv_gn_w, rwkv_gn_b, pool_w, pool_scale, ffn_w1, ffn_w2):
    bf = lambda a: a.astype(BF16)
    vec = lambda a: a.reshape(a.shape[0], 1, -1)
    p = {'norm_mix': vec(norm_mix), 'norm_ffn': vec(norm_ffn), 'norm_final': norm_final.reshape(1, -1),
         'mix': rwkv_mix, 'w_rkv': bf(rwkv_w_rkv), 'w_o': bf(rwkv_w_o), 'w0': vec(rwkv_w0),
         'w1': bf(rwkv_w1), 'w2': bf(rwkv_w2), 'a0': vec(rwkv_a0), 'a1': bf(rwkv_a1),
         'a2': bf(rwkv_a2), 'v0': vec(rwkv_v0), 'v1': bf(rwkv_v1), 'v2': bf(rwkv_v2),
         'g1': bf(rwkv_g1), 'g2': bf(rwkv_g2), 'k_k': vec(rwkv_k_k), 'k_a': vec(rwkv_k_a),
         'r_k': vec(rwkv_r_k), 'gn_w': vec(rwkv_gn_w), 'gn_b': vec(rwkv_gn_b), 'pool_w': bf(pool_w),
         'pool_scale': vec(pool_scale), 'ffn_w1': bf(ffn_w1), 'ffn_w2': bf(ffn_w2)}
    dt = x_prompt.dtype
    nb, _, d = x_prompt.shape
    n_rwkv = state_wkv.shape[0]
    n_pool = state_pool.shape[0]
    nh = d // HEAD_DIM
    z_shift = jnp.zeros((n_rwkv, nb, d), dt)
    z_wkv = jnp.zeros((n_rwkv, nb, nh, HEAD_DIM, HEAD_DIM), dt)
    z_pool = jnp.zeros((n_pool, nb, POOL_BUF, d), dt)
    y_p, wkv_p, shift_p, pool_p = _trunk(x_prompt, 0, z_shift, z_wkv, z_pool, p)
    y_s, wkv_s, shift_s, pool_s = _trunk(x_sample, PAST_LEN, state_shift, state_wkv, state_pool, p)
    return (y_p, y_s, wkv_p, shift_p, pool_p, wkv_s, shift_s, pool_s)
```

```python
import functools
import math

import jax
import jax.numpy as jnp
from jax import lax
from jax.experimental import pallas as pl
from jax.experimental.pallas import tpu as pltpu

HEAD_DIM = 64
LANES = 128
SUBLANES = 8
WKV_LANES = 512
WKV_BLOCK_ELEMS = 8 * 128 * 512
POOL_WINDOWS = (2, 4, 8, 16)
POOL_BUF = max(POOL_WINDOWS) - 1
PAST_LEN = 16384
NORM_EPS = 1e-6
GN_EPS = 64e-5
L2_EPS = 1e-12
WKV_CHUNK = 16
VMEM_LIMIT = 56 * 1024 * 1024

BF16 = jnp.bfloat16
F32 = jnp.float32


def _dot(a, b):
    return jnp.dot(a, b, preferred_element_type=F32)


def _dot_nt(a, b):
    return lax.dot_general(a, b, (((1,), (1,)), ((), ())), preferred_element_type=F32)


def _sigmoid(x):
    return 0.5 * jnp.tanh(0.5 * x) + 0.5


def _rmsnorm(x, g):
    ms = jnp.mean(x * x, axis=-1, keepdims=True)
    return x * lax.rsqrt(ms + NORM_EPS) * g


def _head_ones():
    r = lax.broadcasted_iota(jnp.int32, (LANES, LANES), 0) // HEAD_DIM
    c = lax.broadcasted_iota(jnp.int32, (LANES, LANES), 1) // HEAD_DIM
    return jnp.where(r == c, 1.0, 0.0).astype(BF16)


def _head_sum_bf16(p, ones):
    return _dot(p.astype(BF16), ones)


def _head_sum(p, ones):
    hi = p.astype(BF16)
    lo = (p - hi.astype(F32)).astype(BF16)
    return _dot(hi, ones) + _dot(lo, ones)


def _const_spec(shape):
    n = len(shape)
    return pl.BlockSpec(shape, lambda *_: (0,) * n)


def _pick(arr, *idx):
    shape = (None,) * len(idx) + arr.shape[len(idx):]
    at = tuple(idx) + (0,) * (arr.ndim - len(idx))
    return arr, pl.BlockSpec(shape, lambda *_: at)


def _rows_in(x_ref, xs_scr, nb):
    if len(x_ref.shape) == 2:
        return x_ref[...]
    tt = x_ref.shape[1]
    nblk = xs_scr.shape[0]
    for c in range(nblk):
        for b in range(nb):
            xs_scr[c, pl.ds(b, tt, stride=nb), :] = x_ref[b, :, c * LANES:(c + 1) * LANES]
    return jnp.concatenate([xs_scr[c] for c in range(nblk)], axis=1)


def _rows_out(o_ref, os_scr, val, nb):
    if len(o_ref.shape) == 2:
        o_ref[...] = val
        return
    tt = o_ref.shape[1]
    for c in range(os_scr.shape[0]):
        os_scr[c] = val[:, c * LANES:(c + 1) * LANES]
        for b in range(nb):
            o_ref[b, :, c * LANES:(c + 1) * LANES] = os_scr[c, pl.ds(b, tt, stride=nb), :]


def _x_spec(x, tm, nb):
    if x.ndim == 2:
        return pl.BlockSpec((tm, x.shape[1]), lambda i: (i, 0))
    return pl.BlockSpec((nb, tm // nb, x.shape[2]), lambda i: (0, i, 0))


def _slab_lanes(d):
    return WKV_LANES if d % WKV_LANES == 0 else LANES


def _pre_kernel(has_vres, nb, *refs):
    it = iter(refs)
    x_ref = next(it); shift_ref = next(it); gmix_ref = next(it); mix_ref = next(it)
    wr_ref = next(it); wk_ref = next(it); wv_ref = next(it)
    w0_ref = next(it); w1_ref = next(it); w2_ref = next(it)
    a0_ref = next(it); a1_ref = next(it); a2_ref = next(it)
    if has_vres:
        v0_ref = next(it); v1_ref = next(it); v2_ref = next(it); vfirst_ref = next(it)
    g1_ref = next(it); g2_ref = next(it); kk_ref = next(it); ka_ref = next(it)
    r_out = next(it); lw_out = next(it); k_out = next(it); v_out = next(it)
    kk_out = next(it); b_out = next(it); g_out = next(it); shift_out = next(it)
    carry = next(it); xs_scr = next(it)

    tm, d = g_out.shape
    nslab, _, slab = r_out.shape

    @pl.when(pl.program_id(0) == 0)
    def _():
        carry[...] = shift_ref[...]

    h = _rmsnorm(_rows_in(x_ref, xs_scr, nb), gmix_ref[...])
    if tm > nb:
        hp = jnp.concatenate([carry[...], h[: tm - nb]], axis=0)
    else:
        hp = carry[...]
    carry[...] = h[tm - nb:]
    shift_out[...] = h[tm - nb:]
    dx = hp - h

    def mixed(i):
        return (h + dx * mix_ref[i:i + 1, :]).astype(BF16)

    xv = mixed(2)
    lora_w = _dot(mixed(3), w1_ref[...])
    lora_a = _dot(mixed(4), a1_ref[...])
    if has_vres:
        lora_v = _dot(xv, v1_ref[...])
    lora_g = _dot(mixed(5), g1_ref[...])
    k = _dot(mixed(1), wk_ref[...])
    wpre = w0_ref[...] + _dot(jnp.tanh(lora_w).astype(BF16), w2_ref[...])
    a = _sigmoid(a0_ref[...] + _dot(lora_a.astype(BF16), a2_ref[...]))
    if has_vres:
        gate = _sigmoid(v0_ref[...] + _dot(lora_v.astype(BF16), v2_ref[...]))
    g = _dot(_sigmoid(lora_g).astype(BF16), g2_ref[...])
    g_out[...] = g
    lw = -math.exp(-0.5) * _sigmoid(wpre)

    ones = _head_ones()
    kk = k * kk_ref[...]
    k2 = k * (1.0 + (a - 1.0) * ka_ref[...])
    kkn = []
    for p in range(d // LANES):
        kkp = kk[:, p * LANES:(p + 1) * LANES]
        kkn.append(kkp * lax.rsqrt(jnp.maximum(_head_sum_bf16(kkp * kkp, ones), L2_EPS * L2_EPS)))
    kkn = jnp.concatenate(kkn, axis=1)
    bb = kkn * a
    v = _dot(xv, wv_ref[...])
    if has_vres:
        vfirst = jnp.concatenate([vfirst_ref[p] for p in range(nslab)], axis=1)
        v = v + (vfirst - v) * gate
    r = _dot(mixed(0), wr_ref[...])
    for p in range(nslab):
        sl = slice(p * slab, (p + 1) * slab)
        r_out[p] = r[:, sl]
        lw_out[p] = lw[:, sl]
        k_out[p] = k2[:, sl]
        v_out[p] = v[:, sl]
        kk_out[p] = kkn[:, sl]
        b_out[p] = bb[:, sl]


def _pre_call(x, p, i, j, shift_states, vfirst, nb, tm):
    d = x.shape[-1]
    rows = x.shape[0] if x.ndim == 2 else x.shape[0] * x.shape[1]
    slab = _slab_lanes(d)
    nslab = d // slab
    has_vres = j > 0
    sm_spec = pl.BlockSpec((nslab, tm, slab), lambda i_: (0, i_, 0))
    ops = [(x, _x_spec(x, tm, nb)), _pick(shift_states, j), _pick(p['norm_mix'], i), _pick(p['mix'], j),
           _pick(p['w_rkv'], j, 0), _pick(p['w_rkv'], j, 1), _pick(p['w_rkv'], j, 2),
           _pick(p['w0'], j), _pick(p['w1'], j), _pick(p['w2'], j),
           _pick(p['a0'], j), _pick(p['a1'], j), _pick(p['a2'], j)]
    if has_vres:
        ops += [_pick(p['v0'], j - 1), _pick(p['v1'], j - 1), _pick(p['v2'], j - 1), (vfirst, sm_spec)]
    ops += [_pick(p['g1'], j), _pick(p['g2'], j), _pick(p['k_k'], j), _pick(p['k_a'], j)]
    sm = jax.ShapeDtypeStruct((nslab, rows, slab), F32)
    out_shape = [sm] * 6 + [jax.ShapeDtypeStruct((rows, d), F32), jax.ShapeDtypeStruct((nb, d), F32)]
    out_specs = [sm_spec] * 6 + [pl.BlockSpec((tm, d), lambda i_: (i_, 0)), _const_spec((nb, d))]
    return pl.pallas_call(
        functools.partial(_pre_kernel, has_vres, nb),
        grid=(rows // tm,),
        in_specs=[o[1] for o in ops],
        out_specs=out_specs,
        out_shape=out_shape,
        scratch_shapes=[pltpu.VMEM((nb, d), F32), pltpu.VMEM((d // LANES, tm, LANES), F32)],
        compiler_params=pltpu.CompilerParams(
            dimension_semantics=("arbitrary",), vmem_limit_bytes=VMEM_LIMIT),
        name="rwkv_pre",
    )(*[o[0] for o in ops])


_WKV_SLOTS = ("at", "rt", "u", "v", "bh", "kh", "x0", "y0")


def _wkv_kernel(tchunk, nchunks, has_alias, layer, r_ref, lw_ref, k_ref, v_ref, kk_ref, b_ref, s0_ref,
                rk_ref, gnw_ref, gnb_ref, *rest):
    y_ref, sout_ref, s_scr, *work = rest[1:] if has_alias else rest
    nseq = SUBLANES
    npl = s_scr.shape[1]
    cp = max(tchunk, 8)
    n = cp * nseq
    nv = tchunk * nseq
    grp = LANES // (2 * cp)
    nslot = len(_WKV_SLOTS)
    scr = [dict(zip(_WKV_SLOTS, work[q * nslot:(q + 1) * nslot])) for q in range(npl)]

    @pl.when(pl.program_id(2) == 0)
    def _():
        zero = jnp.zeros((HEAD_DIM, HEAD_DIM), F32)
        for s in range(nseq):
            for q in range(npl):
                top = jnp.concatenate([s0_ref[s, 2 * q], zero], axis=1)
                bot = jnp.concatenate([zero, s0_ref[s, 2 * q + 1]], axis=1)
                s_scr[s, q] = jnp.concatenate([top, bot], axis=0)

    ones = _head_ones()
    hr = lax.broadcasted_iota(jnp.int32, (LANES, LANES), 0) // HEAD_DIM
    hc = lax.broadcasted_iota(jnp.int32, (LANES, LANES), 1) // HEAD_DIM
    same_head = hr == hc
    lane_head = lax.broadcasted_iota(jnp.int32, (1, LANES), 1) // HEAD_DIM
    head_f32 = [jnp.where(lane_head == hh, 1.0, 0.0) for hh in range(2)]
    head_bf16 = [m.astype(BF16) for m in head_f32]
    ri = lax.broadcasted_iota(jnp.int32, (n, 2 * n), 0)
    ci = lax.broadcasted_iota(jnp.int32, (n, 2 * n), 1) % n
    same_seq = (ri % nseq) == (ci % nseq)
    tri_incl = jnp.where(same_seq & (ci // nseq <= ri // nseq), 1.0, 0.0)
    tri_strict = jnp.where(same_seq & (ci // nseq < ri // nseq), 1.0, 0.0)[:, :n]

    def load(ref, c, q):
        val = ref[pl.ds(c * tchunk, tchunk), :, q * LANES:(q + 1) * LANES].reshape(nv, LANES)
        if n > nv:
            val = jnp.concatenate([val, jnp.zeros((n - nv, LANES), F32)], axis=0)
        return val

    def tile_rows(slab, reps):
        return slab if reps == 1 else jnp.concatenate([slab] * reps, axis=0)

    def seq_rows(ref, s):
        return ref[pl.ds(s, cp, stride=nseq), :]

    def phase_a(c, q, st):
        w = scr[q]
        r = load(r_ref, c, q); lw = load(lw_ref, c, q); k = load(k_ref, c, q)
        v = load(v_ref, c, q); kk = load(kk_ref, c, q); b = load(b_ref, c, q)

        acc = jnp.zeros((nseq, LANES), F32)
        cums = []
        for t in range(cp):
            acc = acc + lw[t * nseq:(t + 1) * nseq]
            cums.append(acc)
        cum = jnp.concatenate(cums, axis=0)
        w_inc = jnp.exp(cum)
        w_inv = jnp.exp(-cum)
        w_exc = jnp.concatenate([jnp.ones((nseq, LANES), F32), w_inc[: n - nseq]], axis=0)
        w_tot = w_inc[n - nseq:]
        at = -kk * w_exc
        rt = r * w_inc
        kt = k * w_inv
        bt = b * w_inv
        w_tot_rows = tile_rows(w_tot, cp)
        st.update(r=r, k=k, v=v, at=at, rt=rt, w_tot=w_tot, bh=bt * w_tot_rows, kh=kt * w_tot_rows)

        at16 = at.astype(BF16)
        rt16 = rt.astype(BF16)
        kt16 = kt.astype(BF16)
        kb16 = jnp.concatenate([kt16, bt.astype(BF16)], axis=0)
        gram_a = _dot_nt(jnp.concatenate([at16 * head_bf16[0], at16 * head_bf16[1]], axis=0), kt16)
        gram_r = _dot_nt(jnp.concatenate([rt16 * head_bf16[0], rt16 * head_bf16[1]], axis=0), kb16)
        st["ga"] = jnp.concatenate([gram_a[:n] * tri_strict, gram_a[n:] * tri_strict], axis=1).astype(BF16)
        st["gr"] = jnp.concatenate([gram_r[:n] * tri_incl, gram_r[n:] * tri_incl], axis=1).astype(BF16)
        st["vm"] = jnp.concatenate([v * head_f32[0], v * head_f32[1]], axis=0)
        yield

        prods = [tile_rows(at[t * nseq:(t + 1) * nseq], t) * bt[:t * nseq] for t in range(1, cp)]
        st["coef"] = _head_sum_bf16(jnp.concatenate(prods, axis=0), ones)

    def phase_b(c, q, st):
        w = scr[q]
        w["at"][...] = st["at"]
        w["rt"][...] = st["rt"]
        w["v"][...] = st["v"]
        w["bh"][...] = st["bh"]
        w["kh"][...] = st["kh"]
        for s in range(nseq):
            lhs = jnp.concatenate([seq_rows(w["at"], s), seq_rows(w["rt"], s)], axis=0).astype(BF16)
            out = _dot_nt(lhs, s_scr[s, q].astype(BF16))
            w["x0"][pl.ds(s, cp, stride=nseq), :] = out[:cp]
            w["y0"][pl.ds(s, cp, stride=nseq), :] = out[cp:]
        yield

        x = w["x0"][...] + _dot(st["ga"], st["vm"].astype(BF16))
        yield

        coef = st["coef"]
        us = [x[0:nseq]]
        off = 0
        for t in range(1, cp):
            u_t = x[t * nseq:(t + 1) * nseq]
            for j in range(t):
                u_t = u_t + coef[off + j * nseq:off + (j + 1) * nseq] * us[j]
            off += t * nseq
            us.append(u_t)
        u = jnp.concatenate(us, axis=0)
        w["u"][...] = u
        yield

        vm = st["vm"]
        um = jnp.concatenate([u * head_f32[0], u * head_f32[1]], axis=0)
        vum = jnp.concatenate([vm[:n], um[:n], vm[n:], um[n:]], axis=0)
        st["y"] = w["y0"][...] + _dot(st["gr"], vum.astype(BF16))
        yield

        w_tot = st["w_tot"]
        zp = []
        for s in range(nseq):
            zp += [seq_rows(w["u"], s), seq_rows(w["v"], s)]
        z_t = jnp.concatenate(zp, axis=0).T.astype(BF16)
        for g0 in range(0, nseq, grp):
            cols = []
            for e in range(grp):
                gs = jnp.concatenate([seq_rows(w["bh"], g0 + e), seq_rows(w["kh"], g0 + e)],
                                     axis=0).astype(BF16)
                blk = [gs]
                if e > 0:
                    blk = [jnp.zeros((2 * cp * e, LANES), BF16)] + blk
                if e < grp - 1:
                    blk = blk + [jnp.zeros((2 * cp * (grp - 1 - e), LANES), BF16)]
                cols.append(jnp.concatenate(blk, axis=0))
            lhs = z_t[:, (g0 // grp) * LANES:(g0 // grp + 1) * LANES]
            delta = _dot(lhs, jnp.concatenate(cols, axis=1))
            for e in range(grp):
                s = g0 + e
                dl = delta[:, e * LANES:(e + 1) * LANES]
                s_scr[s, q] = s_scr[s, q] * w_tot[s:s + 1, :] + jnp.where(same_head, dl, 0.0)

    def phase_c(c, q, st):
        ql = slice(q * LANES, (q + 1) * LANES)
        yv, rv, kv, vv = st["y"][:nv], st["r"][:nv], st["k"][:nv], st["v"][:nv]
        hs = _head_sum_bf16(jnp.concatenate([yv, rv * kv * rk_ref[:, ql]], axis=0), ones)
        dlt = yv - hs[:nv] * (1.0 / HEAD_DIM)
        bonus = hs[nv:] * vv
        yield
        var = _head_sum_bf16(dlt * dlt, ones) * (1.0 / HEAD_DIM)
        yn = dlt * lax.rsqrt(var + GN_EPS) * gnw_ref[:, ql] + gnb_ref[:, ql]
        y_ref[pl.ds(c * tchunk, tchunk), :, ql] = (yn + bonus).reshape(tchunk, nseq, LANES)

    def advance(gens, nstages=1):
        for _ in range(nstages):
            for g in gens:
                next(g, None)

    def drain(gens):
        for _ in range(8):
            advance(gens)

    def chunk_group(cs):
        sts = [[{} for _ in range(npl)] for _ in cs]
        ga = [[phase_a(c, q, sts[i][q]) for q in range(npl)] for i, c in enumerate(cs)]
        gb = [[phase_b(c, q, sts[i][q]) for q in range(npl)] for i, c in enumerate(cs)]
        gc = [[phase_c(c, q, sts[i][q]) for q in range(npl)] for i, c in enumerate(cs)]
        drain(ga[0])
        for i in range(len(cs)):
            fill = []
            if i + 1 < len(cs):
                fill.append(ga[i + 1])
            if i > 0:
                fill.append(gc[i - 1])
            for _ in range(5):
                advance(gb[i])
                for f in fill:
                    advance(f)
            for f in fill:
                drain(f)
        drain(gc[-1])

    unroll = 2 if nchunks % 2 == 0 else 1

    def chunk(i, carry):
        chunk_group([i * unroll + e for e in range(unroll)])
        return carry

    lax.fori_loop(0, nchunks // unroll, chunk, 0)

    @pl.when(pl.program_id(2) == pl.num_programs(2) - 1)
    def _():
        out = sout_ref if has_alias else sout_ref.at[layer]
        if not has_alias:
            for other in range(sout_ref.shape[0]):
                if other != layer:
                    sout_ref[other] = jnp.zeros(sout_ref.shape[1:], F32)
        for s in range(nseq):
            for q in range(npl):
                out[s, 2 * q] = s_scr[s, q, :HEAD_DIM, :HEAD_DIM]
                out[s, 2 * q + 1] = s_scr[s, q, HEAD_DIM:, HEAD_DIM:]


def _wkv_call(r, lw, k, v, kk, b, states, new_states, layer, p, nb, nt):
    nslab, _, slab = r.shape
    npl = slab // LANES
    ngroup = nb // SUBLANES
    tchunk = min(WKV_CHUNK, nt)
    tb = min(WKV_BLOCK_ELEMS // (SUBLANES * slab), nt)
    nchunks = tb // tchunk
    assert nt % tb == 0 and tb % tchunk == 0 and nb % SUBLANES == 0
    view = lambda a: a.reshape(nslab, nt, ngroup, SUBLANES, slab)
    act_spec = pl.BlockSpec((None, tb, None, SUBLANES, slab), lambda g, p, t: (p, t, g, 0, 0))
    hps = 2 * npl
    st_in_spec = pl.BlockSpec((None, SUBLANES, hps, HEAD_DIM, HEAD_DIM),
                              lambda g, p, t: (layer, g, p, 0, 0))
    has_alias = new_states is not None
    if has_alias:
        st_out_spec = pl.BlockSpec((None, SUBLANES, hps, HEAD_DIM, HEAD_DIM),
                                   lambda g, p, t: (layer, g, p, 0, 0))
    else:
        st_out_spec = pl.BlockSpec((states.shape[0], SUBLANES, hps, HEAD_DIM, HEAD_DIM),
                                   lambda g, p, t: (0, g, p, 0, 0))
    extra_in = [new_states] if has_alias else []
    extra_spec = [pl.BlockSpec(memory_space=pl.ANY)] if has_alias else []
    par_spec = pl.BlockSpec((None, 1, slab), lambda g, p, t: (layer, 0, p))
    n = max(tchunk, 8) * SUBLANES
    y, s_out = pl.pallas_call(
        functools.partial(_wkv_kernel, tchunk, nchunks, has_alias, layer),
        grid=(ngroup, nslab, nt // tb),
        in_specs=[act_spec] * 6 + [st_in_spec] + [par_spec] * 3 + extra_spec,
        out_specs=[act_spec, st_out_spec],
        out_shape=[jax.ShapeDtypeStruct((nslab, nt, ngroup, SUBLANES, slab), F32),
                   jax.ShapeDtypeStruct(states.shape, F32)],
        input_output_aliases={10: 1} if has_alias else {},
        scratch_shapes=[pltpu.VMEM((SUBLANES, npl, LANES, LANES), F32)]
        + [pltpu.VMEM((n, LANES), F32)] * (len(_WKV_SLOTS) * npl),
        compiler_params=pltpu.CompilerParams(
            dimension_semantics=("arbitrary", "arbitrary", "arbitrary"),
            vmem_limit_bytes=VMEM_LIMIT),
        name="wkv7",
    )(view(r), view(lw), view(k), view(v), view(kk), view(b), states, p['r_k'], p['gn_w'], p['gn_b'],
      *extra_in)
    return y.reshape(nslab, nt * nb, slab), s_out


def _ffn(x, gffn, w1_ref, w2_ref):
    h = _rmsnorm(x, gffn).astype(BF16)
    dff = w1_ref.shape[1]
    step = min(dff, 1024)
    acc = x
    for c in range(dff // step):
        hh = _dot(h, w1_ref[:, c * step:(c + 1) * step])
        hh = jnp.square(jnp.maximum(hh, 0.0)).astype(BF16)
        acc = acc + _dot(hh, w2_ref[c * step:(c + 1) * step, :])
    return acc


def _attn_ffn_kernel(final_norm, nb, x_ref, y_ref, g_ref, wo_ref, gffn_ref, w1_ref, w2_ref, gfin_ref,
                     o_ref, xs_scr, os_scr):
    nslab = y_ref.shape[0]
    y = jnp.concatenate([y_ref[q] for q in range(nslab)], axis=1)
    x = _rows_in(x_ref, xs_scr, nb) + _dot((y * g_ref[...]).astype(BF16), wo_ref[...])
    out = _ffn(x, gffn_ref[...], w1_ref, w2_ref)
    if final_norm:
        out = _rmsnorm(out, gfin_ref[...])
    _rows_out(o_ref, os_scr, out, nb)


def _io_scratch(tm, d, x, batch_major_out):
    full, tiny = (d // LANES, tm, LANES), (1, SUBLANES, LANES)
    return [pltpu.VMEM(full if x.ndim == 3 else tiny, F32),
            pltpu.VMEM(full if batch_major_out else tiny, F32)]


def _out_x(rows, d, tm, nb, batch_major_out):
    if batch_major_out:
        return (jax.ShapeDtypeStruct((nb, rows // nb, d), F32),
                pl.BlockSpec((nb, tm // nb, d), lambda i: (0, i, 0)))
    return jax.ShapeDtypeStruct((rows, d), F32), pl.BlockSpec((tm, d), lambda i: (i, 0))


def _attn_ffn_call(x, y, g, p, i, j, final_norm, batch_major_out, nb, tm):
    rows, d = g.shape
    nslab, _, slab = y.shape
    out_sds, out_spec = _out_x(rows, d, tm, nb, batch_major_out)
    ops = [(x, _x_spec(x, tm, nb)), (y, pl.BlockSpec((nslab, tm, slab), lambda i_: (0, i_, 0))),
           (g, pl.BlockSpec((tm, d), lambda i_: (i_, 0))), _pick(p['w_o'], j), _pick(p['norm_ffn'], i),
           _pick(p['ffn_w1'], i), _pick(p['ffn_w2'], i), _pick(p['norm_final'])]
    return pl.pallas_call(
        functools.partial(_attn_ffn_kernel, final_norm, nb),
        grid=(rows // tm,),
        in_specs=[o[1] for o in ops],
        out_specs=out_spec,
        out_shape=out_sds,
        scratch_shapes=_io_scratch(tm, d, x, batch_major_out),
        compiler_params=pltpu.CompilerParams(
            dimension_semantics=("arbitrary",), vmem_limit_bytes=VMEM_LIMIT),
        name="attn_out_ffn",
    )(*[o[0] for o in ops])


def _pool_ffn_kernel(final_norm, nb, start_pos, x_ref, buf_ref, gmix_ref, pw_ref, ps_ref,
                     gffn_ref, w1_ref, w2_ref, gfin_ref, o_ref, buf_out, carry, xs_scr, os_scr):
    d = gmix_ref.shape[-1]
    tm = x_ref.shape[0] if len(x_ref.shape) == 2 else nb * x_ref.shape[1]
    ngrp = len(POOL_WINDOWS)
    gw = d // ngrp

    @pl.when(pl.program_id(0) == 0)
    def _():
        carry[...] = buf_ref[...]

    x = _rows_in(x_ref, xs_scr, nb)
    h = _rmsnorm(x, gmix_ref[...])
    ext = jnp.concatenate([carry[...], h], axis=0)
    carry[...] = ext[tm:]
    buf_out[...] = ext[tm:]

    rowi = lax.broadcasted_iota(jnp.int32, (tm, gw), 0) + pl.program_id(0) * tm
    pos = start_pos + jnp.right_shift(rowi, int(math.log2(nb)))
    ys = []
    for gi, w in enumerate(POOL_WINDOWS):
        sl = slice(gi * gw, (gi + 1) * gw)
        s = ext[:, sl]
        span = 1
        while span < w:
            s = s[span * nb:] + s[: s.shape[0] - span * nb]
            span *= 2
        s = s[s.shape[0] - tm:]
        cnt = jnp.minimum(pos + 1, w).astype(F32)
        diff = (s / cnt - h[:, sl]).astype(BF16)
        ys.append(_dot(diff, pw_ref[gi]))
    x = x + jnp.concatenate(ys, axis=1) * ps_ref[...]
    out = _ffn(x, gffn_ref[...], w1_ref, w2_ref)
    if final_norm:
        out = _rmsnorm(out, gfin_ref[...])
    _rows_out(o_ref, os_scr, out, nb)


def _pool_ffn_call(x, buf, p, i, j, final_norm, batch_major_out, nb, start_pos, tm):
    d = x.shape[-1]
    rows = x.shape[0] if x.ndim == 2 else x.shape[0] * x.shape[1]
    assert nb & (nb - 1) == 0
    out_sds, out_spec = _out_x(rows, d, tm, nb, batch_major_out)
    ops = [(x, _x_spec(x, tm, nb)), (buf, _const_spec(buf.shape)), _pick(p['norm_mix'], i),
           _pick(p['pool_w'], j), _pick(p['pool_scale'], j), _pick(p['norm_ffn'], i),
           _pick(p['ffn_w1'], i), _pick(p['ffn_w2'], i), _pick(p['norm_final'])]
    return pl.pallas_call(
        functools.partial(_pool_ffn_kernel, final_norm, nb, start_pos),
        grid=(rows // tm,),
        in_specs=[o[1] for o in ops],
        out_specs=[out_spec, _const_spec(buf.shape)],
        out_shape=[out_sds, jax.ShapeDtypeStruct(buf.shape, F32)],
        scratch_shapes=[pltpu.VMEM(buf.shape, F32)] + _io_scratch(tm, d, x, batch_major_out),
        compiler_params=pltpu.CompilerParams(
            dimension_semantics=("arbitrary",), vmem_limit_bytes=VMEM_LIMIT),
        name="pool_ffn",
    )(*[o[0] for o in ops])


def _trunk(x, start_pos, shift_states, wkv_states, pool_bufs, p):
    nb, nt, d = x.shape
    depth = p['norm_mix'].shape[0]
    rows = nb * nt
    tm = min(512, rows)
    assert rows % tm == 0 and tm % nb == 0
    batch_major = nb == SUBLANES
    xt = x if batch_major else x.transpose(1, 0, 2).reshape(rows, d)
    new_shift, new_wkv, new_pool = [], None, []
    v_first = None
    for i in range(depth):
        j = i // 2
        last = i == depth - 1
        bm_out = batch_major and last
        if i % 2 == 0:
            r, lw, k, v, kk, b, g, s_shift = _pre_call(xt, p, i, j, shift_states, v_first, nb, tm)
            if j == 0:
                v_first = v
            y, new_wkv = _wkv_call(r, lw, k, v, kk, b, wkv_states, new_wkv, j, p, nb, nt)
            new_shift.append(s_shift)
            xt = _attn_ffn_call(xt, y, g, p, i, j, last, bm_out, nb, tm)
        else:
            buf = pool_bufs[j].transpose(1, 0, 2).reshape(POOL_BUF * nb, d)
            xt, s_buf = _pool_ffn_call(xt, buf, p, i, j, last, bm_out, nb, start_pos, tm)
            new_pool.append(s_buf.reshape(POOL_BUF, nb, d).transpose(1, 0, 2))
    y = xt if batch_major else xt.reshape(nt, nb, d).transpose(1, 0, 2)
    return y, new_wkv, jnp.stack(new_shift), jnp.stack(new_pool)


def kernel(x_prompt, x_sample, state_wkv, state_shift, state_pool, norm_mix, norm_ffn, norm_final,
           rwkv_mix, rwkv_w_rkv, rwkv_w_o, rwkv_w0, rwkv_w1, rwkv_w2, rwkv_a0, rwkv_a1, rwkv_a2,
           rwkv_v0, rwkv_v1, rwkv_v2, rwkv_g1, rwkv_g2, rwkv_k_k, rwkv_k_a, rwkv_r_k,
           rwkv_gn_w, rwkv_gn_b, pool_w, pool_scale, ffn_w1, ffn_w2):
    bf = lambda a: a.astype(BF16)
    vec = lambda a: a.reshape(a.shape[0], 1, -1)
    p = {'norm_mix': vec(norm_mix), 'norm_ffn': vec(norm_ffn), 'norm_final': norm_final.reshape(1, -1),
         'mix': rwkv_mix, 'w_rkv': bf(rwkv_w_rkv), 'w_o': bf(rwkv_w_o), 'w0': vec(rwkv_w0),
         'w1': bf(rwkv_w1), 'w2': bf(rwkv_w2), 'a0': vec(rwkv_a0), 'a1': bf(rwkv_a1),
         'a2': bf(rwkv_a2), 'v0': vec(rwkv_v0), 'v1': bf(rwkv_v1), 'v2': bf(rwkv_v2),
         'g1': bf(rwkv_g1), 'g2': bf(rwkv_g2), 'k_k': vec(rwkv_k_k), 'k_a': vec(rwkv_k_a),
         'r_k': vec(rwkv_r_k), 'gn_w': vec(rwkv_gn_w), 'gn_b': vec(rwkv_gn_b), 'pool_w': bf(pool_w),
         'pool_scale': vec(pool_scale), 'ffn_w1': bf(ffn_w1), 'ffn_w2': bf(ffn_w2)}
    dt = x_prompt.dtype
    nb, _, d = x_prompt.shape
    n_rwkv = state_wkv.shape[0]
    n_pool = state_pool.shape[0]
    nh = d // HEAD_DIM
    z_shift = jnp.zeros((n_rwkv, nb, d), dt)
    z_wkv = jnp.zeros((n_rwkv, nb, nh, HEAD_DIM, HEAD_DIM), dt)
    z_pool = jnp.zeros((n_pool, nb, POOL_BUF, d), dt)
    y_p, wkv_p, shift_p, pool_p = _trunk(x_prompt, 0, z_shift, z_wkv, z_pool, p)
    y_s, wkv_s, shift_s, pool_s = _trunk(x_sample, PAST_LEN, state_shift, state_wkv, state_pool, p)
    return (y_p, y_s, wkv_p, shift_p, pool_p, wkv_s, shift_s, pool_s)
```

```python
import functools
import math

import jax
import jax.numpy as jnp
from jax import lax
from jax.experimental import pallas as pl
from jax.experimental.pallas import tpu as pltpu

HEAD_DIM = 64
LANES = 128
SUBLANES = 8
WKV_LANES = 512
WKV_BLOCK_ELEMS = 8 * 128 * 512
POOL_WINDOWS = (2, 4, 8, 16)
POOL_BUF = max(POOL_WINDOWS) - 1
PAST_LEN = 16384
NORM_EPS = 1e-6
GN_EPS = 64e-5
L2_EPS = 1e-12
WKV_CHUNK = 16
VMEM_LIMIT = 56 * 1024 * 1024

BF16 = jnp.bfloat16
F32 = jnp.float32


def _dot(a, b):
    return jnp.dot(a, b, preferred_element_type=F32)


def _dot_nt(a, b):
    return lax.dot_general(a, b, (((1,), (1,)), ((), ())), preferred_element_type=F32)


def _sigmoid(x):
    return 0.5 * jnp.tanh(0.5 * x) + 0.5


def _rmsnorm(x, g):
    ms = jnp.mean(x * x, axis=-1, keepdims=True)
    return x * lax.rsqrt(ms + NORM_EPS) * g


def _head_ones():
    r = lax.broadcasted_iota(jnp.int32, (LANES, LANES), 0) // HEAD_DIM
    c = lax.broadcasted_iota(jnp.int32, (LANES, LANES), 1) // HEAD_DIM
    return jnp.where(r == c, 1.0, 0.0).astype(BF16)


def _head_sum_bf16(p, ones):
    return _dot(p.astype(BF16), ones)


def _head_sum(p, ones):
    hi = p.astype(BF16)
    lo = (p - hi.astype(F32)).astype(BF16)
    return _dot(hi, ones) + _dot(lo, ones)


def _const_spec(shape):
    n = len(shape)
    return pl.BlockSpec(shape, lambda *_: (0,) * n)


def _pick(arr, *idx):
    shape = (None,) * len(idx) + arr.shape[len(idx):]
    at = tuple(idx) + (0,) * (arr.ndim - len(idx))
    return arr, pl.BlockSpec(shape, lambda *_: at)


def _rows_in(x_ref, xs_scr, nb):
    if len(x_ref.shape) == 2:
        return x_ref[...]
    tt = x_ref.shape[1]
    nblk = xs_scr.shape[0]
    for c in range(nblk):
        for b in range(nb):
            xs_scr[c, pl.ds(b, tt, stride=nb), :] = x_ref[b, :, c * LANES:(c + 1) * LANES]
    return jnp.concatenate([xs_scr[c] for c in range(nblk)], axis=1)


def _rows_out(o_ref, os_scr, val, nb):
    if len(o_ref.shape) == 2:
        o_ref[...] = val
        return
    tt = o_ref.shape[1]
    for c in range(os_scr.shape[0]):
        os_scr[c] = val[:, c * LANES:(c + 1) * LANES]
        for b in range(nb):
            o_ref[b, :, c * LANES:(c + 1) * LANES] = os_scr[c, pl.ds(b, tt, stride=nb), :]


def _x_spec(x, tm, nb):
    if x.ndim == 2:
        return pl.BlockSpec((tm, x.shape[1]), lambda i: (i, 0))
    return pl.BlockSpec((nb, tm // nb, x.shape[2]), lambda i: (0, i, 0))


def _slab_lanes(d):
    return WKV_LANES if d % WKV_LANES == 0 else LANES


def _pre_kernel(has_vres, nb, *refs):
    it = iter(refs)
    x_ref = next(it); shift_ref = next(it); gmix_ref = next(it); mix_ref = next(it)
    wr_ref = next(it); wk_ref = next(it); wv_ref = next(it)
    w0_ref = next(it); w1_ref = next(it); w2_ref = next(it)
    a0_ref = next(it); a1_ref = next(it); a2_ref = next(it)
    if has_vres:
        v0_ref = next(it); v1_ref = next(it); v2_ref = next(it); vfirst_ref = next(it)
    g1_ref = next(it); g2_ref = next(it); kk_ref = next(it); ka_ref = next(it)
    r_out = next(it); lw_out = next(it); k_out = next(it); v_out = next(it)
    kk_out = next(it); b_out = next(it); g_out = next(it); shift_out = next(it)
    carry = next(it); xs_scr = next(it)

    tm, d = g_out.shape
    nslab, _, slab = r_out.shape

    @pl.when(pl.program_id(0) == 0)
    def _():
        carry[...] = shift_ref[...]

    h = _rmsnorm(_rows_in(x_ref, xs_scr, nb), gmix_ref[...])
    if tm > nb:
        hp = jnp.concatenate([carry[...], h[: tm - nb]], axis=0)
    else:
        hp = carry[...]
    carry[...] = h[tm - nb:]
    shift_out[...] = h[tm - nb:]
    dx = hp - h

    def mixed(i):
        return (h + dx * mix_ref[i:i + 1, :]).astype(BF16)

    xv = mixed(2)
    lora_w = _dot(mixed(3), w1_ref[...])
    lora_a = _dot(mixed(4), a1_ref[...])
    if has_vres:
        lora_v = _dot(xv, v1_ref[...])
    lora_g = _dot(mixed(5), g1_ref[...])
    k = _dot(mixed(1), wk_ref[...])
    wpre = w0_ref[...] + _dot(jnp.tanh(lora_w).astype(BF16), w2_ref[...])
    a = _sigmoid(a0_ref[...] + _dot(lora_a.astype(BF16), a2_ref[...]))
    if has_vres:
        gate = _sigmoid(v0_ref[...] + _dot(lora_v.astype(BF16), v2_ref[...]))
    g = _dot(_sigmoid(lora_g).astype(BF16), g2_ref[...])
    g_out[...] = g
    lw = -math.exp(-0.5) * _sigmoid(wpre)

    ones = _head_ones()
    kk = k * kk_ref[...]
    k2 = k * (1.0 + (a - 1.0) * ka_ref[...])
    kkn = []
    for p in range(d // LANES):
        kkp = kk[:, p * LANES:(p + 1) * LANES]
        kkn.append(kkp * lax.rsqrt(jnp.maximum(_head_sum_bf16(kkp * kkp, ones), L2_EPS * L2_EPS)))
    kkn = jnp.concatenate(kkn, axis=1)
    bb = kkn * a
    v = _dot(xv, wv_ref[...])
    if has_vres:
        vfirst = jnp.concatenate([vfirst_ref[p] for p in range(nslab)], axis=1)
        v = v + (vfirst - v) * gate
    r = _dot(mixed(0), wr_ref[...])
    for p in range(nslab):
        sl = slice(p * slab, (p + 1) * slab)
        r_out[p] = r[:, sl]
        lw_out[p] = lw[:, sl]
        k_out[p] = k2[:, sl]
        v_out[p] = v[:, sl]
        kk_out[p] = kkn[:, sl]
        b_out[p] = bb[:, sl]


def _pre_call(x, p, i, j, shift_states, vfirst, nb, tm):
    d = x.shape[-1]
    rows = x.shape[0] if x.ndim == 2 else x.shape[0] * x.shape[1]
    slab = _slab_lanes(d)
    nslab = d // slab
    has_vres = j > 0
    sm_spec = pl.BlockSpec((nslab, tm, slab), lambda i_: (0, i_, 0))
    ops = [(x, _x_spec(x, tm, nb)), _pick(shift_states, j), _pick(p['norm_mix'], i), _pick(p['mix'], j),
           _pick(p['w_rkv'], j, 0), _pick(p['w_rkv'], j, 1), _pick(p['w_rkv'], j, 2),
           _pick(p['w0'], j), _pick(p['w1'], j), _pick(p['w2'], j),
           _pick(p['a0'], j), _pick(p['a1'], j), _pick(p['a2'], j)]
    if has_vres:
        ops += [_pick(p['v0'], j - 1), _pick(p['v1'], j - 1), _pick(p['v2'], j - 1), (vfirst, sm_spec)]
    ops += [_pick(p['g1'], j), _pick(p['g2'], j), _pick(p['k_k'], j), _pick(p['k_a'], j)]
    sm = jax.ShapeDtypeStruct((nslab, rows, slab), F32)
    out_shape = [sm] * 6 + [jax.ShapeDtypeStruct((rows, d), F32), jax.ShapeDtypeStruct((nb, d), F32)]
    out_specs = [sm_spec] * 6 + [pl.BlockSpec((tm, d), lambda i_: (i_, 0)), _const_spec((nb, d))]
    return pl.pallas_call(
        functools.partial(_pre_kernel, has_vres, nb),
        grid=(rows // tm,),
        in_specs=[o[1] for o in ops],
        out_specs=out_specs,
        out_shape=out_shape,
        scratch_shapes=[pltpu.VMEM((nb, d), F32), pltpu.VMEM((d // LANES, tm, LANES), F32)],
        compiler_params=pltpu.CompilerParams(
            dimension_semantics=("arbitrary",), vmem_limit_bytes=VMEM_LIMIT),
        name="rwkv_pre",
    )(*[o[0] for o in ops])


_WKV_SLOTS = ("at", "rt", "u", "v", "bh", "kh", "x0", "y0")


def _wkv_kernel(tchunk, nchunks, has_alias, layer, batch_minor, r_ref, lw_ref, k_ref, v_ref, kk_ref, b_ref,
                s0_ref, rk_ref, gnw_ref, gnb_ref, *rest):
    y_ref, sout_ref, s_scr, *work = rest[1:] if has_alias else rest
    if batch_minor:
        t_scr, *work = work
    nseq = SUBLANES
    npl = s_scr.shape[1]
    cp = max(tchunk, 8)
    n = cp * nseq
    nv = tchunk * nseq
    grp = LANES // (2 * cp)
    nslot = len(_WKV_SLOTS)
    scr = [dict(zip(_WKV_SLOTS, work[q * nslot:(q + 1) * nslot])) for q in range(npl)]

    if batch_minor:
        group = pl.program_id(1)

        @pl.when((group == 0) & (pl.program_id(2) == 0))
        def _():
            for q in range(npl):
                def relayout(v, carry):
                    tile = jnp.concatenate([s0_ref[2 * q, v], s0_ref[2 * q + 1, v]], axis=0)
                    t_scr[q, v] = tile.T
                    return carry
                lax.fori_loop(0, HEAD_DIM, relayout, 0)

        @pl.when(pl.program_id(2) == 0)
        def _():
            lane = lax.broadcasted_iota(jnp.int32, (HEAD_DIM, LANES), 1)
            for s in range(nseq):
                for q in range(npl):
                    side = t_scr[q, :, group * nseq + s, :]
                    s_scr[s, q] = jnp.concatenate([jnp.where(lane < HEAD_DIM, side, 0.0),
                                                   jnp.where(lane >= HEAD_DIM, side, 0.0)], axis=0)
    else:
        @pl.when(pl.program_id(2) == 0)
        def _():
            zero = jnp.zeros((HEAD_DIM, HEAD_DIM), F32)
            for s in range(nseq):
                for q in range(npl):
                    top = jnp.concatenate([s0_ref[s, 2 * q], zero], axis=1)
                    bot = jnp.concatenate([zero, s0_ref[s, 2 * q + 1]], axis=1)
                    s_scr[s, q] = jnp.concatenate([top, bot], axis=0)

    ones = _head_ones()
    hr = lax.broadcasted_iota(jnp.int32, (LANES, LANES), 0) // HEAD_DIM
    hc = lax.broadcasted_iota(jnp.int32, (LANES, LANES), 1) // HEAD_DIM
    same_head = hr == hc
    lane_head = lax.broadcasted_iota(jnp.int32, (1, LANES), 1) // HEAD_DIM
    head_f32 = [jnp.where(lane_head == hh, 1.0, 0.0) for hh in range(2)]
    head_bf16 = [m.astype(BF16) for m in head_f32]
    ri = lax.broadcasted_iota(jnp.int32, (n, 2 * n), 0)
    ci = lax.broadcasted_iota(jnp.int32, (n, 2 * n), 1) % n
    same_seq = (ri % nseq) == (ci % nseq)
    tri_incl = jnp.where(same_seq & (ci // nseq <= ri // nseq), 1.0, 0.0)
    tri_strict = jnp.where(same_seq & (ci // nseq < ri // nseq), 1.0, 0.0)[:, :n]

    def load(ref, c, q):
        val = ref[pl.ds(c * tchunk, tchunk), :, q * LANES:(q + 1) * LANES].reshape(nv, LANES)
        if n > nv:
            val = jnp.concatenate([val, jnp.zeros((n - nv, LANES), F32)], axis=0)
        return val

    def tile_rows(slab, reps):
        return slab if reps == 1 else jnp.concatenate([slab] * reps, axis=0)

    def seq_rows(ref, s):
        return ref[pl.ds(s, cp, stride=nseq), :]

    def phase_a(c, q, st):
        w = scr[q]
        r = load(r_ref, c, q); lw = load(lw_ref, c, q); k = load(k_ref, c, q)
        v = load(v_ref, c, q); kk = load(kk_ref, c, q); b = load(b_ref, c, q)

        acc = jnp.zeros((nseq, LANES), F32)
        cums = []
        for t in range(cp):
            acc = acc + lw[t * nseq:(t + 1) * nseq]
            cums.append(acc)
        cum = jnp.concatenate(cums, axis=0)
        w_inc = jnp.exp(cum)
        w_inv = jnp.exp(-cum)
        w_exc = jnp.concatenate([jnp.ones((nseq, LANES), F32), w_inc[: n - nseq]], axis=0)
        w_tot = w_inc[n - nseq:]
        at = -kk * w_exc
        rt = r * w_inc
        kt = k * w_inv
        bt = b * w_inv
        w_tot_rows = tile_rows(w_tot, cp)
        st.update(r=r, k=k, v=v, at=at, rt=rt, w_tot=w_tot, bh=bt * w_tot_rows, kh=kt * w_tot_rows)

        at16 = at.astype(BF16)
        rt16 = rt.astype(BF16)
        kt16 = kt.astype(BF16)
        kb16 = jnp.concatenate([kt16, bt.astype(BF16)], axis=0)
        gram_a = _dot_nt(jnp.concatenate([at16 * head_bf16[0], at16 * head_bf16[1]], axis=0), kt16)
        gram_r = _dot_nt(jnp.concatenate([rt16 * head_bf16[0], rt16 * head_bf16[1]], axis=0), kb16)
        st["ga"] = jnp.concatenate([gram_a[:n] * tri_strict, gram_a[n:] * tri_strict], axis=1).astype(BF16)
        st["gr"] = jnp.concatenate([gram_r[:n] * tri_incl, gram_r[n:] * tri_incl], axis=1).astype(BF16)
        st["vm"] = jnp.concatenate([v * head_f32[0], v * head_f32[1]], axis=0)
        yield

        prods = [tile_rows(at[t * nseq:(t + 1) * nseq], t) * bt[:t * nseq] for t in range(1, cp)]
        st["coef"] = _head_sum_bf16(jnp.concatenate(prods, axis=0), ones)

    def phase_b(c, q, st):
        w = scr[q]
        w["at"][...] = st["at"]
        w["rt"][...] = st["rt"]
        w["v"][...] = st["v"]
        w["bh"][...] = st["bh"]
        w["kh"][...] = st["kh"]
        for s in range(nseq):
            lhs = jnp.concatenate([seq_rows(w["at"], s), seq_rows(w["rt"], s)], axis=0).astype(BF16)
            out = _dot_nt(lhs, s_scr[s, q].astype(BF16))
            w["x0"][pl.ds(s, cp, stride=nseq), :] = out[:cp]
            w["y0"][pl.ds(s, cp, stride=nseq), :] = out[cp:]
        yield

        x = w["x0"][...] + _dot(st["ga"], st["vm"].astype(BF16))
        yield

        coef = st["coef"]
        us = [x[0:nseq]]
        off = 0
        for t in range(1, cp):
            u_t = x[t * nseq:(t + 1) * nseq]
            for j in range(t):
                u_t = u_t + coef[off + j * nseq:off + (j + 1) * nseq] * us[j]
            off += t * nseq
            us.append(u_t)
        u = jnp.concatenate(us, axis=0)
        w["u"][...] = u
        yield

        vm = st["vm"]
        um = jnp.concatenate([u * head_f32[0], u * head_f32[1]], axis=0)
        vum = jnp.concatenate([vm[:n], um[:n], vm[n:], um[n:]], axis=0)
        st["y"] = w["y0"][...] + _dot(st["gr"], vum.astype(BF16))
        yield

        w_tot = st["w_tot"]
        zp = []
        for s in range(nseq):
            zp += [seq_rows(w["u"], s), seq_rows(w["v"], s)]
        z_t = jnp.concatenate(zp, axis=0).T.astype(BF16)
        for g0 in range(0, nseq, grp):
            cols = []
            for e in range(grp):
                gs = jnp.concatenate([seq_rows(w["bh"], g0 + e), seq_rows(w["kh"], g0 + e)],
                                     axis=0).astype(BF16)
                blk = [gs]
                if e > 0:
                    blk = [jnp.zeros((2 * cp * e, LANES), BF16)] + blk
                if e < grp - 1:
                    blk = blk + [jnp.zeros((2 * cp * (grp - 1 - e), LANES), BF16)]
                cols.append(jnp.concatenate(blk, axis=0))
            lhs = z_t[:, (g0 // grp) * LANES:(g0 // grp + 1) * LANES]
            delta = _dot(lhs, jnp.concatenate(cols, axis=1))
            for e in range(grp):
                s = g0 + e
                dl = delta[:, e * LANES:(e + 1) * LANES]
                s_scr[s, q] = s_scr[s, q] * w_tot[s:s + 1, :] + jnp.where(same_head, dl, 0.0)

    def phase_c(c, q, st):
        ql = slice(q * LANES, (q + 1) * LANES)
        yv, rv, kv, vv = st["y"][:nv], st["r"][:nv], st["k"][:nv], st["v"][:nv]
        hs = _head_sum_bf16(jnp.concatenate([yv, rv * kv * rk_ref[:, ql]], axis=0), ones)
        dlt = yv - hs[:nv] * (1.0 / HEAD_DIM)
        bonus = hs[nv:] * vv
        yield
        var = _head_sum_bf16(dlt * dlt, ones) * (1.0 / HEAD_DIM)
        yn = dlt * lax.rsqrt(var + GN_EPS) * gnw_ref[:, ql] + gnb_ref[:, ql]
        y_ref[pl.ds(c * tchunk, tchunk), :, ql] = (yn + bonus).reshape(tchunk, nseq, LANES)

    def advance(gens, nstages=1):
        for _ in range(nstages):
            for g in gens:
                next(g, None)

    def drain(gens):
        for _ in range(8):
            advance(gens)

    def chunk_group(cs):
        sts = [[{} for _ in range(npl)] for _ in cs]
        ga = [[phase_a(c, q, sts[i][q]) for q in range(npl)] for i, c in enumerate(cs)]
        gb = [[phase_b(c, q, sts[i][q]) for q in range(npl)] for i, c in enumerate(cs)]
        gc = [[phase_c(c, q, sts[i][q]) for q in range(npl)] for i, c in enumerate(cs)]
        drain(ga[0])
        for i in range(len(cs)):
            fill = []
            if i + 1 < len(cs):
                fill.append(ga[i + 1])
            if i > 0:
                fill.append(gc[i - 1])
            for _ in range(5):
                advance(gb[i])
                for f in fill:
                    advance(f)
            for f in fill:
                drain(f)
        drain(gc[-1])

    unroll = 2 if nchunks % 2 == 0 else 1

    def chunk(i, carry):
        chunk_group([i * unroll + e for e in range(unroll)])
        return carry

    lax.fori_loop(0, nchunks // unroll, chunk, 0)

    @pl.when(pl.program_id(2) == pl.num_programs(2) - 1)
    def _():
        out = sout_ref if has_alias else sout_ref.at[layer]
        if not has_alias:
            for other in range(sout_ref.shape[0]):
                if other != layer:
                    sout_ref[other] = jnp.zeros(sout_ref.shape[1:], F32)
        for s in range(nseq):
            for q in range(npl):
                out[s, 2 * q] = s_scr[s, q, :HEAD_DIM, :HEAD_DIM]
                out[s, 2 * q + 1] = s_scr[s, q, HEAD_DIM:, HEAD_DIM:]


def _wkv_call(r, lw, k, v, kk, b, states, new_states, layer, p, nb, nt):
    nslab, _, slab = r.shape
    npl = slab // LANES
    ngroup = nb // SUBLANES
    tchunk = min(WKV_CHUNK, nt)
    tb = min(WKV_BLOCK_ELEMS // (SUBLANES * slab), nt)
    nchunks = tb // tchunk
    assert nt % tb == 0 and tb % tchunk == 0 and nb % SUBLANES == 0
    view = lambda a: a.reshape(nslab, nt, ngroup, SUBLANES, slab)
    hps = 2 * npl
    batch_minor = nb == LANES
    if batch_minor:
        order = lambda f: (lambda p_, g, t: f(g, p_, t))
        grid = (nslab, ngroup, nt // tb)
        states_in = states.transpose(0, 2, 3, 4, 1)
        st_in_spec = pl.BlockSpec((None, hps, HEAD_DIM, HEAD_DIM, nb), lambda p_, g, t: (layer, p_, 0, 0, 0),
                                  pipeline_mode=pl.Buffered(1))
        relayout_scr = [pltpu.VMEM((npl, HEAD_DIM, LANES, LANES), F32)]
    else:
        order = lambda f: f
        grid = (ngroup, nslab, nt // tb)
        states_in = states
        st_in_spec = pl.BlockSpec((None, SUBLANES, hps, HEAD_DIM, HEAD_DIM),
                                  lambda g, p_, t: (layer, g, p_, 0, 0))
        relayout_scr = []
    act_spec = pl.BlockSpec((None, tb, None, SUBLANES, slab), order(lambda g, p_, t: (p_, t, g, 0, 0)))
    has_alias = new_states is not None
    if has_alias:
        st_out_spec = pl.BlockSpec((None, SUBLANES, hps, HEAD_DIM, HEAD_DIM),
                                   order(lambda g, p_, t: (layer, g, p_, 0, 0)))
    else:
        st_out_spec = pl.BlockSpec((states.shape[0], SUBLANES, hps, HEAD_DIM, HEAD_DIM),
                                   order(lambda g, p_, t: (0, g, p_, 0, 0)))
    extra_in = [new_states] if has_alias else []
    extra_spec = [pl.BlockSpec(memory_space=pl.ANY)] if has_alias else []
    par_spec = pl.BlockSpec((None, 1, slab), order(lambda g, p_, t: (layer, 0, p_)))
    n = max(tchunk, 8) * SUBLANES
    y, s_out = pl.pallas_call(
        functools.partial(_wkv_kernel, tchunk, nchunks, has_alias, layer, batch_minor),
        grid=grid,
        in_specs=[act_spec] * 6 + [st_in_spec] + [par_spec] * 3 + extra_spec,
        out_specs=[act_spec, st_out_spec],
        out_shape=[jax.ShapeDtypeStruct((nslab, nt, ngroup, SUBLANES, slab), F32),
                   jax.ShapeDtypeStruct(states.shape, F32)],
        input_output_aliases={10: 1} if has_alias else {},
        scratch_shapes=[pltpu.VMEM((SUBLANES, npl, LANES, LANES), F32)] + relayout_scr
        + [pltpu.VMEM((n, LANES), F32)] * (len(_WKV_SLOTS) * npl),
        compiler_params=pltpu.CompilerParams(
            dimension_semantics=("arbitrary", "arbitrary", "arbitrary"),
            vmem_limit_bytes=VMEM_LIMIT),
        name="wkv7",
    )(view(r), view(lw), view(k), view(v), view(kk), view(b), states_in, p['r_k'], p['gn_w'], p['gn_b'],
      *extra_in)
    return y.reshape(nslab, nt * nb, slab), s_out


def _ffn(x, gffn, w1_ref, w2_ref):
    h = _rmsnorm(x, gffn).astype(BF16)
    dff = w1_ref.shape[1]
    step = min(dff, 1024)
    acc = x
    for c in range(dff // step):
        hh = _dot(h, w1_ref[:, c * step:(c + 1) * step])
        hh = jnp.square(jnp.maximum(hh, 0.0)).astype(BF16)
        acc = acc + _dot(hh, w2_ref[c * step:(c + 1) * step, :])
    return acc


def _attn_ffn_kernel(final_norm, nb, x_ref, y_ref, g_ref, wo_ref, gffn_ref, w1_ref, w2_ref, gfin_ref,
                     o_ref, xs_scr, os_scr):
    nslab = y_ref.shape[0]
    y = jnp.concatenate([y_ref[q] for q in range(nslab)], axis=1)
    x = _rows_in(x_ref, xs_scr, nb) + _dot((y * g_ref[...]).astype(BF16), wo_ref[...])
    out = _ffn(x, gffn_ref[...], w1_ref, w2_ref)
    if final_norm:
        out = _rmsnorm(out, gfin_ref[...])
    _rows_out(o_ref, os_scr, out, nb)


def _io_scratch(tm, d, x, batch_major_out):
    full, tiny = (d // LANES, tm, LANES), (1, SUBLANES, LANES)
    return [pltpu.VMEM(full if x.ndim == 3 else tiny, F32),
            pltpu.VMEM(full if batch_major_out else tiny, F32)]


def _out_x(rows, d, tm, nb, batch_major_out):
    if batch_major_out:
        return (jax.ShapeDtypeStruct((nb, rows // nb, d), F32),
                pl.BlockSpec((nb, tm // nb, d), lambda i: (0, i, 0)))
    return jax.ShapeDtypeStruct((rows, d), F32), pl.BlockSpec((tm, d), lambda i: (i, 0))


def _attn_ffn_call(x, y, g, p, i, j, final_norm, batch_major_out, nb, tm):
    rows, d = g.shape
    nslab, _, slab = y.shape
    out_sds, out_spec = _out_x(rows, d, tm, nb, batch_major_out)
    ops = [(x, _x_spec(x, tm, nb)), (y, pl.BlockSpec((nslab, tm, slab), lambda i_: (0, i_, 0))),
           (g, pl.BlockSpec((tm, d), lambda i_: (i_, 0))), _pick(p['w_o'], j), _pick(p['norm_ffn'], i),
           _pick(p['ffn_w1'], i), _pick(p['ffn_w2'], i), _pick(p['norm_final'])]
    return pl.pallas_call(
        functools.partial(_attn_ffn_kernel, final_norm, nb),
        grid=(rows // tm,),
        in_specs=[o[1] for o in ops],
        out_specs=out_spec,
        out_shape=out_sds,
        scratch_shapes=_io_scratch(tm, d, x, batch_major_out),
        compiler_params=pltpu.CompilerParams(
            dimension_semantics=("arbitrary",), vmem_limit_bytes=VMEM_LIMIT),
        name="attn_out_ffn",
    )(*[o[0] for o in ops])


def _pool_ffn_kernel(final_norm, nb, start_pos, x_ref, buf_ref, gmix_ref, pw_ref, ps_ref,
                     gffn_ref, w1_ref, w2_ref, gfin_ref, o_ref, buf_out, carry, xs_scr, os_scr):
    d = gmix_ref.shape[-1]
    tm = x_ref.shape[0] if len(x_ref.shape) == 2 else nb * x_ref.shape[1]
    ngrp = len(POOL_WINDOWS)
    gw = d // ngrp

    @pl.when(pl.program_id(0) == 0)
    def _():
        carry[...] = buf_ref[...]

    x = _rows_in(x_ref, xs_scr, nb)
    h = _rmsnorm(x, gmix_ref[...])
    ext = jnp.concatenate([carry[...], h], axis=0)
    carry[...] = ext[tm:]
    buf_out[...] = ext[tm:]

    rowi = lax.broadcasted_iota(jnp.int32, (tm, gw), 0) + pl.program_id(0) * tm
    pos = start_pos + jnp.right_shift(rowi, int(math.log2(nb)))
    ys = []
    for gi, w in enumerate(POOL_WINDOWS):
        sl = slice(gi * gw, (gi + 1) * gw)
        s = ext[:, sl]
        span = 1
        while span < w:
            s = s[span * nb:] + s[: s.shape[0] - span * nb]
            span *= 2
        s = s[s.shape[0] - tm:]
        cnt = jnp.minimum(pos + 1, w).astype(F32)
        diff = (s / cnt - h[:, sl]).astype(BF16)
        ys.append(_dot(diff, pw_ref[gi]))
    x = x + jnp.concatenate(ys, axis=1) * ps_ref[...]
    out = _ffn(x, gffn_ref[...], w1_ref, w2_ref)
    if final_norm:
        out = _rmsnorm(out, gfin_ref[...])
    _rows_out(o_ref, os_scr, out, nb)


def _pool_ffn_call(x, buf, p, i, j, final_norm, batch_major_out, nb, start_pos, tm):
    d = x.shape[-1]
    rows = x.shape[0] if x.ndim == 2 else x.shape[0] * x.shape[1]
    assert nb & (nb - 1) == 0
    out_sds, out_spec = _out_x(rows, d, tm, nb, batch_major_out)
    ops = [(x, _x_spec(x, tm, nb)), (buf, _const_spec(buf.shape)), _pick(p['norm_mix'], i),
           _pick(p['pool_w'], j), _pick(p['pool_scale'], j), _pick(p['norm_ffn'], i),
           _pick(p['ffn_w1'], i), _pick(p['ffn_w2'], i), _pick(p['norm_final'])]
    return pl.pallas_call(
        functools.partial(_pool_ffn_kernel, final_norm, nb, start_pos),
        grid=(rows // tm,),
        in_specs=[o[1] for o in ops],
        out_specs=[out_spec, _const_spec(buf.shape)],
        out_shape=[out_sds, jax.ShapeDtypeStruct(buf.shape, F32)],
        scratch_shapes=[pltpu.VMEM(buf.shape, F32)] + _io_scratch(tm, d, x, batch_major_out),
        compiler_params=pltpu.CompilerParams(
            dimension_semantics=("arbitrary",), vmem_limit_bytes=VMEM_LIMIT),
        name="pool_ffn",
    )(*[o[0] for o in ops])


def _trunk(x, start_pos, shift_states, wkv_states, pool_bufs, p):
    nb, nt, d = x.shape
    depth = p['norm_mix'].shape[0]
    rows = nb * nt
    tm = min(512, rows)
    assert rows % tm == 0 and tm % nb == 0
    batch_major = nb == SUBLANES
    xt = x if batch_major else x.transpose(1, 0, 2).reshape(rows, d)
    new_shift, new_wkv, new_pool = [], None, []
    v_first = None
    for i in range(depth):
        j = i // 2
        last = i == depth - 1
        bm_out = batch_major and last
        if i % 2 == 0:
            r, lw, k, v, kk, b, g, s_shift = _pre_call(xt, p, i, j, shift_states, v_first, nb, tm)
            if j == 0:
                v_first = v
            y, new_wkv = _wkv_call(r, lw, k, v, kk, b, wkv_states, new_wkv, j, p, nb, nt)
            new_shift.append(s_shift)
            xt = _attn_ffn_call(xt, y, g, p, i, j, last, bm_out, nb, tm)
        else:
            buf = pool_bufs[j].transpose(1, 0, 2).reshape(POOL_BUF * nb, d)
            xt, s_buf = _pool_ffn_call(xt, buf, p, i, j, last, bm_out, nb, start_pos, tm)
            new_pool.append(s_buf.reshape(POOL_BUF, nb, d).transpose(1, 0, 2))
    y = xt if batch_major else xt.reshape(nt, nb, d).transpose(1, 0, 2)
    return y, new_wkv, jnp.stack(new_shift), jnp.stack(new_pool)


def kernel(x_prompt, x_sample, state_wkv, state_shift, state_pool, norm_mix, norm_ffn, norm_final,
           rwkv_mix, rwkv_w_rkv, rwkv_w_o, rwkv_w0, rwkv_w1, rwkv_w2, rwkv_a0, rwkv_a1, rwkv_a2,
           rwkv_v0, rwkv_v1, rwkv_v2, rwkv_g1, rwkv_g2, rwkv_k_k, rwkv_k_a, rwkv_r_k,
           rwkv_gn_w, rwkv_gn_b, pool_w, pool_scale, ffn_w1, ffn_w2):
    bf = lambda a: a.astype(BF16)
    vec = lambda a: a.reshape(a.shape[0], 1, -1)
    p = {'norm_mix': vec(norm_mix), 'norm_ffn': vec(norm_ffn), 'norm_final': norm_final.reshape(1, -1),
         'mix': rwkv_mix, 'w_rkv': bf(rwkv_w_rkv), 'w_o': bf(rwkv_w_o), 'w0': vec(rwkv_w0),
         'w1': bf(rwkv_w1), 'w2': bf(rwkv_w2), 'a0': vec(rwkv_a0), 'a1': bf(rwkv_a1),
         'a2': bf(rwkv_a2), 'v0': vec(rwkv_v0), 'v1': bf(rwkv_v1), 'v2': bf(rwkv_v2),
         'g1': bf(rwkv_g1), 'g2': bf(rwkv_g2), 'k_k': vec(rwkv_k_k), 'k_a': vec(rwkv_k_a),
         'r_k': vec(rwkv_r_k), 'gn_w': vec(rwkv_gn_w), 'gn_b': vec(rwkv_gn_b), 'pool_w': bf(pool_w),
         'pool_scale': vec(pool_scale), 'ffn_w1': bf(ffn_w1), 'ffn_w2': bf(ffn_w2)}
    dt = x_prompt.dtype
    nb, _, d = x_prompt.shape
    n_rwkv = state_wkv.shape[0]
    n_pool = state_pool.shape[0]
    nh = d // HEAD_DIM
    z_shift = jnp.zeros((n_rwkv, nb, d), dt)
    z_wkv = jnp.zeros((n_rwkv, nb, nh, HEAD_DIM, HEAD_DIM), dt)
    z_pool = jnp.zeros((n_pool, nb, POOL_BUF, d), dt)
    y_p, wkv_p, shift_p, pool_p = _trunk(x_prompt, 0, z_shift, z_wkv, z_pool, p)
    y_s, wkv_s, shift_s, pool_s = _trunk(x_sample, PAST_LEN, state_shift, state_wkv, state_pool, p)
    return (y_p, y_s, wkv_p, shift_p, pool_p, wkv_s, shift_s, pool_s)
```

```python
import functools
import math

import jax
import jax.numpy as jnp
from jax import lax
from jax.experimental import pallas as pl
from jax.experimental.pallas import tpu as pltpu

HEAD_DIM = 64
LANES = 128
SUBLANES = 8
WKV_LANES = 512
WKV_BLOCK_ELEMS = 8 * 128 * 512
POOL_WINDOWS = (2, 4, 8, 16)
POOL_BUF = max(POOL_WINDOWS) - 1
PAST_LEN = 16384
NORM_EPS = 1e-6
GN_EPS = 64e-5
L2_EPS = 1e-12
WKV_CHUNK = 16
VMEM_LIMIT = 56 * 1024 * 1024

BF16 = jnp.bfloat16
F32 = jnp.float32


def _dot(a, b):
    return jnp.dot(a, b, preferred_element_type=F32)


def _dot_nt(a, b):
    return lax.dot_general(a, b, (((1,), (1,)), ((), ())), preferred_element_type=F32)


def _sigmoid(x):
    return 0.5 * jnp.tanh(0.5 * x) + 0.5


def _rmsnorm(x, g):
    ms = jnp.mean(x * x, axis=-1, keepdims=True)
    return x * lax.rsqrt(ms + NORM_EPS) * g


def _head_ones():
    r = lax.broadcasted_iota(jnp.int32, (LANES, LANES), 0) // HEAD_DIM
    c = lax.broadcasted_iota(jnp.int32, (LANES, LANES), 1) // HEAD_DIM
    return jnp.where(r == c, 1.0, 0.0).astype(BF16)


def _head_sum_bf16(p, ones):
    return _dot(p.astype(BF16), ones)


def _head_sum(p, ones):
    hi = p.astype(BF16)
    lo = (p - hi.astype(F32)).astype(BF16)
    return _dot(hi, ones) + _dot(lo, ones)


def _const_spec(shape):
    n = len(shape)
    return pl.BlockSpec(shape, lambda *_: (0,) * n)


def _pick(arr, *idx):
    shape = (None,) * len(idx) + arr.shape[len(idx):]
    at = tuple(idx) + (0,) * (arr.ndim - len(idx))
    return arr, pl.BlockSpec(shape, lambda *_: at)


def _rows_in(x_ref, xs_scr, nb):
    if len(x_ref.shape) == 2:
        return x_ref[...]
    tt = x_ref.shape[1]
    nblk = xs_scr.shape[0]
    for c in range(nblk):
        for b in range(nb):
            xs_scr[c, pl.ds(b, tt, stride=nb), :] = x_ref[b, :, c * LANES:(c + 1) * LANES]
    return jnp.concatenate([xs_scr[c] for c in range(nblk)], axis=1)


def _rows_out(o_ref, os_scr, val, nb):
    if len(o_ref.shape) == 2:
        o_ref[...] = val
        return
    tt = o_ref.shape[1]
    for c in range(os_scr.shape[0]):
        os_scr[c] = val[:, c * LANES:(c + 1) * LANES]
        for b in range(nb):
            o_ref[b, :, c * LANES:(c + 1) * LANES] = os_scr[c, pl.ds(b, tt, stride=nb), :]


def _x_spec(x, tm, nb):
    if x.ndim == 2:
        return pl.BlockSpec((tm, x.shape[1]), lambda i: (i, 0))
    return pl.BlockSpec((nb, tm // nb, x.shape[2]), lambda i: (0, i, 0))


def _slab_lanes(d):
    return WKV_LANES if d % WKV_LANES == 0 else LANES


def _pre_kernel(has_vres, nb, *refs):
    it = iter(refs)
    x_ref = next(it); shift_ref = next(it); gmix_ref = next(it); mix_ref = next(it)
    wr_ref = next(it); wk_ref = next(it); wv_ref = next(it)
    w0_ref = next(it); w1_ref = next(it); w2_ref = next(it)
    a0_ref = next(it); a1_ref = next(it); a2_ref = next(it)
    if has_vres:
        v0_ref = next(it); v1_ref = next(it); v2_ref = next(it); vfirst_ref = next(it)
    g1_ref = next(it); g2_ref = next(it); kk_ref = next(it); ka_ref = next(it)
    r_out = next(it); lw_out = next(it); k_out = next(it); v_out = next(it)
    kk_out = next(it); b_out = next(it); g_out = next(it); shift_out = next(it)
    carry = next(it); xs_scr = next(it)

    tm, d = g_out.shape
    nslab, _, slab = r_out.shape

    @pl.when(pl.program_id(0) == 0)
    def _():
        carry[...] = shift_ref[...]

    h = _rmsnorm(_rows_in(x_ref, xs_scr, nb), gmix_ref[...])
    if tm > nb:
        hp = jnp.concatenate([carry[...], h[: tm - nb]], axis=0)
    else:
        hp = carry[...]
    carry[...] = h[tm - nb:]
    shift_out[...] = h[tm - nb:]
    dx = hp - h

    def mixed(i):
        return (h + dx * mix_ref[i:i + 1, :]).astype(BF16)

    xv = mixed(2)
    lora_w = _dot(mixed(3), w1_ref[...])
    lora_a = _dot(mixed(4), a1_ref[...])
    if has_vres:
        lora_v = _dot(xv, v1_ref[...])
    lora_g = _dot(mixed(5), g1_ref[...])
    k = _dot(mixed(1), wk_ref[...])
    wpre = w0_ref[...] + _dot(jnp.tanh(lora_w).astype(BF16), w2_ref[...])
    a = _sigmoid(a0_ref[...] + _dot(lora_a.astype(BF16), a2_ref[...]))
    if has_vres:
        gate = _sigmoid(v0_ref[...] + _dot(lora_v.astype(BF16), v2_ref[...]))
    g = _dot(_sigmoid(lora_g).astype(BF16), g2_ref[...])
    g_out[...] = g
    lw = -math.exp(-0.5) * _sigmoid(wpre)

    ones = _head_ones()
    kk = k * kk_ref[...]
    k2 = k * (1.0 + (a - 1.0) * ka_ref[...])
    kkn = []
    for p in range(d // LANES):
        kkp = kk[:, p * LANES:(p + 1) * LANES]
        kkn.append(kkp * lax.rsqrt(jnp.maximum(_head_sum_bf16(kkp * kkp, ones), L2_EPS * L2_EPS)))
    kkn = jnp.concatenate(kkn, axis=1)
    bb = kkn * a
    v = _dot(xv, wv_ref[...])
    if has_vres:
        vfirst = jnp.concatenate([vfirst_ref[p] for p in range(nslab)], axis=1)
        v = v + (vfirst - v) * gate
    r = _dot(mixed(0), wr_ref[...])
    for p in range(nslab):
        sl = slice(p * slab, (p + 1) * slab)
        r_out[p] = r[:, sl]
        lw_out[p] = lw[:, sl]
        k_out[p] = k2[:, sl]
        v_out[p] = v[:, sl]
        kk_out[p] = kkn[:, sl]
        b_out[p] = bb[:, sl]


def _pre_call(x, p, i, j, shift_states, vfirst, nb, tm):
    d = x.shape[-1]
    rows = x.shape[0] if x.ndim == 2 else x.shape[0] * x.shape[1]
    slab = _slab_lanes(d)
    nslab = d // slab
    has_vres = j > 0
    sm_spec = pl.BlockSpec((nslab, tm, slab), lambda i_: (0, i_, 0))
    ops = [(x, _x_spec(x, tm, nb)), _pick(shift_states, j), _pick(p['norm_mix'], i), _pick(p['mix'], j),
           _pick(p['w_rkv'], j, 0), _pick(p['w_rkv'], j, 1), _pick(p['w_rkv'], j, 2),
           _pick(p['w0'], j), _pick(p['w1'], j), _pick(p['w2'], j),
           _pick(p['a0'], j), _pick(p['a1'], j), _pick(p['a2'], j)]
    if has_vres:
        ops += [_pick(p['v0'], j - 1), _pick(p['v1'], j - 1), _pick(p['v2'], j - 1), (vfirst, sm_spec)]
    ops += [_pick(p['g1'], j), _pick(p['g2'], j), _pick(p['k_k'], j), _pick(p['k_a'], j)]
    sm = jax.ShapeDtypeStruct((nslab, rows, slab), F32)
    out_shape = [sm] * 6 + [jax.ShapeDtypeStruct((rows, d), F32), jax.ShapeDtypeStruct((nb, d), F32)]
    out_specs = [sm_spec] * 6 + [pl.BlockSpec((tm, d), lambda i_: (i_, 0)), _const_spec((nb, d))]
    return pl.pallas_call(
        functools.partial(_pre_kernel, has_vres, nb),
        grid=(rows // tm,),
        in_specs=[o[1] for o in ops],
        out_specs=out_specs,
        out_shape=out_shape,
        scratch_shapes=[pltpu.VMEM((nb, d), F32), pltpu.VMEM((d // LANES, tm, LANES), F32)],
        compiler_params=pltpu.CompilerParams(
            dimension_semantics=("arbitrary",), vmem_limit_bytes=VMEM_LIMIT),
        name="rwkv_pre",
    )(*[o[0] for o in ops])


_WKV_SLOTS = ("at", "rt", "u", "v", "bh", "kh", "x0", "y0")


def _wkv_kernel(tchunk, nchunks, has_alias, layer, batch_minor, r_ref, lw_ref, k_ref, v_ref, kk_ref, b_ref,
                s0_ref, rk_ref, gnw_ref, gnb_ref, *rest):
    y_ref, sout_ref, s_scr, *work = rest[1:] if has_alias else rest
    if batch_minor:
        t_scr, *work = work
    nseq = SUBLANES
    npl = s_scr.shape[1]
    cp = max(tchunk, 8)
    n = cp * nseq
    nv = tchunk * nseq
    grp = LANES // (2 * cp)
    nslot = len(_WKV_SLOTS)
    scr = [dict(zip(_WKV_SLOTS, work[q * nslot:(q + 1) * nslot])) for q in range(npl)]

    if batch_minor:
        group = pl.program_id(1)

        @pl.when((group == 0) & (pl.program_id(2) == 0))
        def _():
            def relayout(i, carry):
                for q in range(npl):
                    for e in range(SUBLANES):
                        v = i * SUBLANES + e
                        tile = jnp.concatenate([s0_ref[2 * q, v], s0_ref[2 * q + 1, v]], axis=0)
                        t_scr[q, v] = tile.T
                return carry
            lax.fori_loop(0, HEAD_DIM // SUBLANES, relayout, 0)

        @pl.when(pl.program_id(2) == 0)
        def _():
            lane = lax.broadcasted_iota(jnp.int32, (HEAD_DIM, LANES), 1)
            for s in range(nseq):
                for q in range(npl):
                    side = t_scr[q, :, group * nseq + s, :]
                    s_scr[s, q] = jnp.concatenate([jnp.where(lane < HEAD_DIM, side, 0.0),
                                                   jnp.where(lane >= HEAD_DIM, side, 0.0)], axis=0)
    else:
        @pl.when(pl.program_id(2) == 0)
        def _():
            zero = jnp.zeros((HEAD_DIM, HEAD_DIM), F32)
            for s in range(nseq):
                for q in range(npl):
                    top = jnp.concatenate([s0_ref[s, 2 * q], zero], axis=1)
                    bot = jnp.concatenate([zero, s0_ref[s, 2 * q + 1]], axis=1)
                    s_scr[s, q] = jnp.concatenate([top, bot], axis=0)

    ones = _head_ones()
    hr = lax.broadcasted_iota(jnp.int32, (LANES, LANES), 0) // HEAD_DIM
    hc = lax.broadcasted_iota(jnp.int32, (LANES, LANES), 1) // HEAD_DIM
    same_head = hr == hc
    lane_head = lax.broadcasted_iota(jnp.int32, (1, LANES), 1) // HEAD_DIM
    head_f32 = [jnp.where(lane_head == hh, 1.0, 0.0) for hh in range(2)]
    head_bf16 = [m.astype(BF16) for m in head_f32]
    ri = lax.broadcasted_iota(jnp.int32, (n, 2 * n), 0)
    ci = lax.broadcasted_iota(jnp.int32, (n, 2 * n), 1) % n
    same_seq = (ri % nseq) == (ci % nseq)
    tri_incl = jnp.where(same_seq & (ci // nseq <= ri // nseq), 1.0, 0.0)
    tri_strict = jnp.where(same_seq & (ci // nseq < ri // nseq), 1.0, 0.0)[:, :n]

    def load(ref, c, q):
        val = ref[pl.ds(c * tchunk, tchunk), :, q * LANES:(q + 1) * LANES].reshape(nv, LANES)
        if n > nv:
            val = jnp.concatenate([val, jnp.zeros((n - nv, LANES), F32)], axis=0)
        return val

    def tile_rows(slab, reps):
        return slab if reps == 1 else jnp.concatenate([slab] * reps, axis=0)

    def seq_rows(ref, s):
        return ref[pl.ds(s, cp, stride=nseq), :]

    def phase_a(c, q, st):
        w = scr[q]
        r = load(r_ref, c, q); lw = load(lw_ref, c, q); k = load(k_ref, c, q)
        v = load(v_ref, c, q); kk = load(kk_ref, c, q); b = load(b_ref, c, q)

        acc = jnp.zeros((nseq, LANES), F32)
        cums = []
        for t in range(cp):
            acc = acc + lw[t * nseq:(t + 1) * nseq]
            cums.append(acc)
        cum = jnp.concatenate(cums, axis=0)
        w_inc = jnp.exp(cum)
        w_inv = jnp.exp(-cum)
        w_exc = jnp.concatenate([jnp.ones((nseq, LANES), F32), w_inc[: n - nseq]], axis=0)
        w_tot = w_inc[n - nseq:]
        at = -kk * w_exc
        rt = r * w_inc
        kt = k * w_inv
        bt = b * w_inv
        w_tot_rows = tile_rows(w_tot, cp)
        st.update(r=r, k=k, v=v, at=at, rt=rt, w_tot=w_tot, bh=bt * w_tot_rows, kh=kt * w_tot_rows)

        at16 = at.astype(BF16)
        rt16 = rt.astype(BF16)
        kt16 = kt.astype(BF16)
        kb16 = jnp.concatenate([kt16, bt.astype(BF16)], axis=0)
        gram_a = _dot_nt(jnp.concatenate([at16 * head_bf16[0], at16 * head_bf16[1]], axis=0), kt16)
        gram_r = _dot_nt(jnp.concatenate([rt16 * head_bf16[0], rt16 * head_bf16[1]], axis=0), kb16)
        st["ga"] = jnp.concatenate([gram_a[:n] * tri_strict, gram_a[n:] * tri_strict], axis=1).astype(BF16)
        st["gr"] = jnp.concatenate([gram_r[:n] * tri_incl, gram_r[n:] * tri_incl], axis=1).astype(BF16)
        st["vm"] = jnp.concatenate([v * head_f32[0], v * head_f32[1]], axis=0)
        yield

        prods = [tile_rows(at[t * nseq:(t + 1) * nseq], t) * bt[:t * nseq] for t in range(1, cp)]
        st["coef"] = _head_sum_bf16(jnp.concatenate(prods, axis=0), ones)

    def phase_b(c, q, st):
        w = scr[q]
        w["at"][...] = st["at"]
        w["rt"][...] = st["rt"]
        w["v"][...] = st["v"]
        w["bh"][...] = st["bh"]
        w["kh"][...] = st["kh"]
        for s in range(nseq):
            lhs = jnp.concatenate([seq_rows(w["at"], s), seq_rows(w["rt"], s)], axis=0).astype(BF16)
            out = _dot_nt(lhs, s_scr[s, q].astype(BF16))
            w["x0"][pl.ds(s, cp, stride=nseq), :] = out[:cp]
            w["y0"][pl.ds(s, cp, stride=nseq), :] = out[cp:]
        yield

        x = w["x0"][...] + _dot(st["ga"], st["vm"].astype(BF16))
        yield

        coef = st["coef"]
        us = [x[0:nseq]]
        off = 0
        for t in range(1, cp):
            u_t = x[t * nseq:(t + 1) * nseq]
            for j in range(t):
                u_t = u_t + coef[off + j * nseq:off + (j + 1) * nseq] * us[j]
            off += t * nseq
            us.append(u_t)
        u = jnp.concatenate(us, axis=0)
        w["u"][...] = u
        yield

        vm = st["vm"]
        um = jnp.concatenate([u * head_f32[0], u * head_f32[1]], axis=0)
        vum = jnp.concatenate([vm[:n], um[:n], vm[n:], um[n:]], axis=0)
        st["y"] = w["y0"][...] + _dot(st["gr"], vum.astype(BF16))
        yield

        w_tot = st["w_tot"]
        zp = []
        for s in range(nseq):
            zp += [seq_rows(w["u"], s), seq_rows(w["v"], s)]
        z_t = jnp.concatenate(zp, axis=0).T.astype(BF16)
        for g0 in range(0, nseq, grp):
            cols = []
            for e in range(grp):
                gs = jnp.concatenate([seq_rows(w["bh"], g0 + e), seq_rows(w["kh"], g0 + e)],
                                     axis=0).astype(BF16)
                blk = [gs]
                if e > 0:
                    blk = [jnp.zeros((2 * cp * e, LANES), BF16)] + blk
                if e < grp - 1:
                    blk = blk + [jnp.zeros((2 * cp * (grp - 1 - e), LANES), BF16)]
                cols.append(jnp.concatenate(blk, axis=0))
            lhs = z_t[:, (g0 // grp) * LANES:(g0 // grp + 1) * LANES]
            delta = _dot(lhs, jnp.concatenate(cols, axis=1))
            for e in range(grp):
                s = g0 + e
                dl = delta[:, e * LANES:(e + 1) * LANES]
                s_scr[s, q] = s_scr[s, q] * w_tot[s:s + 1, :] + jnp.where(same_head, dl, 0.0)

    def phase_c(c, q, st):
        ql = slice(q * LANES, (q + 1) * LANES)
        yv, rv, kv, vv = st["y"][:nv], st["r"][:nv], st["k"][:nv], st["v"][:nv]
        hs = _head_sum_bf16(jnp.concatenate([yv, rv * kv * rk_ref[:, ql]], axis=0), ones)
        dlt = yv - hs[:nv] * (1.0 / HEAD_DIM)
        bonus = hs[nv:] * vv
        yield
        var = _head_sum_bf16(dlt * dlt, ones) * (1.0 / HEAD_DIM)
        yn = dlt * lax.rsqrt(var + GN_EPS) * gnw_ref[:, ql] + gnb_ref[:, ql]
        y_ref[pl.ds(c * tchunk, tchunk), :, ql] = (yn + bonus).reshape(tchunk, nseq, LANES)

    def advance(gens, nstages=1):
        for _ in range(nstages):
            for g in gens:
                next(g, None)

    def drain(gens):
        for _ in range(8):
            advance(gens)

    def chunk_group(cs):
        sts = [[{} for _ in range(npl)] for _ in cs]
        ga = [[phase_a(c, q, sts[i][q]) for q in range(npl)] for i, c in enumerate(cs)]
        gb = [[phase_b(c, q, sts[i][q]) for q in range(npl)] for i, c in enumerate(cs)]
        gc = [[phase_c(c, q, sts[i][q]) for q in range(npl)] for i, c in enumerate(cs)]
        drain(ga[0])
        for i in range(len(cs)):
            fill = []
            if i + 1 < len(cs):
                fill.append(ga[i + 1])
            if i > 0:
                fill.append(gc[i - 1])
            for _ in range(5):
                advance(gb[i])
                for f in fill:
                    advance(f)
            for f in fill:
                drain(f)
        drain(gc[-1])

    unroll = 2 if nchunks % 2 == 0 else 1

    def chunk(i, carry):
        chunk_group([i * unroll + e for e in range(unroll)])
        return carry

    lax.fori_loop(0, nchunks // unroll, chunk, 0)

    @pl.when(pl.program_id(2) == pl.num_programs(2) - 1)
    def _():
        out = sout_ref if has_alias else sout_ref.at[layer]
        if not has_alias:
            for other in range(sout_ref.shape[0]):
                if other != layer:
                    sout_ref[other] = jnp.zeros(sout_ref.shape[1:], F32)
        for s in range(nseq):
            for q in range(npl):
                out[s, 2 * q] = s_scr[s, q, :HEAD_DIM, :HEAD_DIM]
                out[s, 2 * q + 1] = s_scr[s, q, HEAD_DIM:, HEAD_DIM:]


def _wkv_call(r, lw, k, v, kk, b, states, new_states, layer, p, nb, nt):
    nslab, _, slab = r.shape
    npl = slab // LANES
    ngroup = nb // SUBLANES
    tchunk = min(WKV_CHUNK, nt)
    tb = min(WKV_BLOCK_ELEMS // (SUBLANES * slab), nt)
    nchunks = tb // tchunk
    assert nt % tb == 0 and tb % tchunk == 0 and nb % SUBLANES == 0
    view = lambda a: a.reshape(nslab, nt, ngroup, SUBLANES, slab)
    hps = 2 * npl
    batch_minor = nb == LANES
    if batch_minor:
        order = lambda f: (lambda p_, g, t: f(g, p_, t))
        grid = (nslab, ngroup, nt // tb)
        states_in = states.transpose(0, 2, 3, 4, 1)
        st_in_spec = pl.BlockSpec((None, hps, HEAD_DIM, HEAD_DIM, nb), lambda p_, g, t: (layer, p_, 0, 0, 0),
                                  pipeline_mode=pl.Buffered(1))
        relayout_scr = [pltpu.VMEM((npl, HEAD_DIM, LANES, LANES), F32)]
    else:
        order = lambda f: f
        grid = (ngroup, nslab, nt // tb)
        states_in = states
        st_in_spec = pl.BlockSpec((None, SUBLANES, hps, HEAD_DIM, HEAD_DIM),
                                  lambda g, p_, t: (layer, g, p_, 0, 0))
        relayout_scr = []
    act_spec = pl.BlockSpec((None, tb, None, SUBLANES, slab), order(lambda g, p_, t: (p_, t, g, 0, 0)))
    has_alias = new_states is not None
    if has_alias:
        st_out_spec = pl.BlockSpec((None, SUBLANES, hps, HEAD_DIM, HEAD_DIM),
                                   order(lambda g, p_, t: (layer, g, p_, 0, 0)))
    else:
        st_out_spec = pl.BlockSpec((states.shape[0], SUBLANES, hps, HEAD_DIM, HEAD_DIM),
                                   order(lambda g, p_, t: (0, g, p_, 0, 0)))
    extra_in = [new_states] if has_alias else []
    extra_spec = [pl.BlockSpec(memory_space=pl.ANY)] if has_alias else []
    par_spec = pl.BlockSpec((None, 1, slab), order(lambda g, p_, t: (layer, 0, p_)))
    n = max(tchunk, 8) * SUBLANES
    y, s_out = pl.pallas_call(
        functools.partial(_wkv_kernel, tchunk, nchunks, has_alias, layer, batch_minor),
        grid=grid,
        in_specs=[act_spec] * 6 + [st_in_spec] + [par_spec] * 3 + extra_spec,
        out_specs=[act_spec, st_out_spec],
        out_shape=[jax.ShapeDtypeStruct((nslab, nt, ngroup, SUBLANES, slab), F32),
                   jax.ShapeDtypeStruct(states.shape, F32)],
        input_output_aliases={10: 1} if has_alias else {},
        scratch_shapes=[pltpu.VMEM((SUBLANES, npl, LANES, LANES), F32)] + relayout_scr
        + [pltpu.VMEM((n, LANES), F32)] * (len(_WKV_SLOTS) * npl),
        compiler_params=pltpu.CompilerParams(
            dimension_semantics=("arbitrary", "arbitrary", "arbitrary"),
            vmem_limit_bytes=VMEM_LIMIT),
        name="wkv7",
    )(view(r), view(lw), view(k), view(v), view(kk), view(b), states_in, p['r_k'], p['gn_w'], p['gn_b'],
      *extra_in)
    return y.reshape(nslab, nt * nb, slab), s_out


def _ffn(x, gffn, w1_ref, w2_ref):
    h = _rmsnorm(x, gffn).astype(BF16)
    dff = w1_ref.shape[1]
    step = min(dff, 1024)
    acc = x
    for c in range(dff // step):
        hh = _dot(h, w1_ref[:, c * step:(c + 1) * step])
        hh = jnp.square(jnp.maximum(hh, 0.0)).astype(BF16)
        acc = acc + _dot(hh, w2_ref[c * step:(c + 1) * step, :])
    return acc


def _attn_ffn_kernel(final_norm, nb, x_ref, y_ref, g_ref, wo_ref, gffn_ref, w1_ref, w2_ref, gfin_ref,
                     o_ref, xs_scr, os_scr):
    nslab = y_ref.shape[0]
    y = jnp.concatenate([y_ref[q] for q in range(nslab)], axis=1)
    x = _rows_in(x_ref, xs_scr, nb) + _dot((y * g_ref[...]).astype(BF16), wo_ref[...])
    out = _ffn(x, gffn_ref[...], w1_ref, w2_ref)
    if final_norm:
        out = _rmsnorm(out, gfin_ref[...])
    _rows_out(o_ref, os_scr, out, nb)


def _io_scratch(tm, d, x, batch_major_out):
    full, tiny = (d // LANES, tm, LANES), (1, SUBLANES, LANES)
    return [pltpu.VMEM(full if x.ndim == 3 else tiny, F32),
            pltpu.VMEM(full if batch_major_out else tiny, F32)]


def _out_x(rows, d, tm, nb, batch_major_out):
    if batch_major_out:
        return (jax.ShapeDtypeStruct((nb, rows // nb, d), F32),
                pl.BlockSpec((nb, tm // nb, d), lambda i: (0, i, 0)))
    return jax.ShapeDtypeStruct((rows, d), F32), pl.BlockSpec((tm, d), lambda i: (i, 0))


def _attn_ffn_call(x, y, g, p, i, j, final_norm, batch_major_out, nb, tm):
    rows, d = g.shape
    nslab, _, slab = y.shape
    out_sds, out_spec = _out_x(rows, d, tm, nb, batch_major_out)
    ops = [(x, _x_spec(x, tm, nb)), (y, pl.BlockSpec((nslab, tm, slab), lambda i_: (0, i_, 0))),
           (g, pl.BlockSpec((tm, d), lambda i_: (i_, 0))), _pick(p['w_o'], j), _pick(p['norm_ffn'], i),
           _pick(p['ffn_w1'], i), _pick(p['ffn_w2'], i), _pick(p['norm_final'])]
    return pl.pallas_call(
        functools.partial(_attn_ffn_kernel, final_norm, nb),
        grid=(rows // tm,),
        in_specs=[o[1] for o in ops],
        out_specs=out_spec,
        out_shape=out_sds,
        scratch_shapes=_io_scratch(tm, d, x, batch_major_out),
        compiler_params=pltpu.CompilerParams(
            dimension_semantics=("arbitrary",), vmem_limit_bytes=VMEM_LIMIT),
        name="attn_out_ffn",
    )(*[o[0] for o in ops])


def _pool_ffn_kernel(final_norm, nb, start_pos, x_ref, buf_ref, gmix_ref, pw_ref, ps_ref,
                     gffn_ref, w1_ref, w2_ref, gfin_ref, o_ref, buf_out, carry, xs_scr, os_scr):
    d = gmix_ref.shape[-1]
    tm = x_ref.shape[0] if len(x_ref.shape) == 2 else nb * x_ref.shape[1]
    ngrp = len(POOL_WINDOWS)
    gw = d // ngrp

    @pl.when(pl.program_id(0) == 0)
    def _():
        carry[...] = buf_ref[...]

    x = _rows_in(x_ref, xs_scr, nb)
    h = _rmsnorm(x, gmix_ref[...])
    ext = jnp.concatenate([carry[...], h], axis=0)
    carry[...] = ext[tm:]
    buf_out[...] = ext[tm:]

    rowi = lax.broadcasted_iota(jnp.int32, (tm, gw), 0) + pl.program_id(0) * tm
    pos = start_pos + jnp.right_shift(rowi, int(math.log2(nb)))
    ys = []
    for gi, w in enumerate(POOL_WINDOWS):
        sl = slice(gi * gw, (gi + 1) * gw)
        s = ext[:, sl]
        span = 1
        while span < w:
            s = s[span * nb:] + s[: s.shape[0] - span * nb]
            span *= 2
        s = s[s.shape[0] - tm:]
        cnt = jnp.minimum(pos + 1, w).astype(F32)
        diff = (s / cnt - h[:, sl]).astype(BF16)
        ys.append(_dot(diff, pw_ref[gi]))
    x = x + jnp.concatenate(ys, axis=1) * ps_ref[...]
    out = _ffn(x, gffn_ref[...], w1_ref, w2_ref)
    if final_norm:
        out = _rmsnorm(out, gfin_ref[...])
    _rows_out(o_ref, os_scr, out, nb)


def _pool_ffn_call(x, buf, p, i, j, final_norm, batch_major_out, nb, start_pos, tm):
    d = x.shape[-1]
    rows = x.shape[0] if x.ndim == 2 else x.shape[0] * x.shape[1]
    assert nb & (nb - 1) == 0
    out_sds, out_spec = _out_x(rows, d, tm, nb, batch_major_out)
    ops = [(x, _x_spec(x, tm, nb)), (buf, _const_spec(buf.shape)), _pick(p['norm_mix'], i),
           _pick(p['pool_w'], j), _pick(p['pool_scale'], j), _pick(p['norm_ffn'], i),
           _pick(p['ffn_w1'], i), _pick(p['ffn_w2'], i), _pick(p['norm_final'])]
    return pl.pallas_call(
        functools.partial(_pool_ffn_kernel, final_norm, nb, start_pos),
        grid=(rows // tm,),
        in_specs=[o[1] for o in ops],
        out_specs=[out_spec, _const_spec(buf.shape)],
        out_shape=[out_sds, jax.ShapeDtypeStruct(buf.shape, F32)],
        scratch_shapes=[pltpu.VMEM(buf.shape, F32)] + _io_scratch(tm, d, x, batch_major_out),
        compiler_params=pltpu.CompilerParams(
            dimension_semantics=("arbitrary",), vmem_limit_bytes=VMEM_LIMIT),
        name="pool_ffn",
    )(*[o[0] for o in ops])


def _trunk(x, start_pos, shift_states, wkv_states, pool_bufs, p):
    nb, nt, d = x.shape
    depth = p['norm_mix'].shape[0]
    rows = nb * nt
    tm = min(512, rows)
    assert rows % tm == 0 and tm % nb == 0
    batch_major = nb == SUBLANES
    xt = x if batch_major else x.transpose(1, 0, 2).reshape(rows, d)
    new_shift, new_wkv, new_pool = [], None, []
    v_first = None
    for i in range(depth):
        j = i // 2
        last = i == depth - 1
        bm_out = batch_major and last
        if i % 2 == 0:
            r, lw, k, v, kk, b, g, s_shift = _pre_call(xt, p, i, j, shift_states, v_first, nb, tm)
            if j == 0:
                v_first = v
            y, new_wkv = _wkv_call(r, lw, k, v, kk, b, wkv_states, new_wkv, j, p, nb, nt)
            new_shift.append(s_shift)
            xt = _attn_ffn_call(xt, y, g, p, i, j, last, bm_out, nb, tm)
        else:
            buf = pool_bufs[j].transpose(1, 0, 2).reshape(POOL_BUF * nb, d)
            xt, s_buf = _pool_ffn_call(xt, buf, p, i, j, last, bm_out, nb, start_pos, tm)
            new_pool.append(s_buf.reshape(POOL_BUF, nb, d).transpose(1, 0, 2))
    y = xt if batch_major else xt.reshape(nt, nb, d).transpose(1, 0, 2)
    return y, new_wkv, jnp.stack(new_shift), jnp.stack(new_pool)


def kernel(x_prompt, x_sample, state_wkv, state_shift, state_pool, norm_mix, norm_ffn, norm_final,
           rwkv_mix, rwkv_w_rkv, rwkv_w_o, rwkv_w0, rwkv_w1, rwkv_w2, rwkv_a0, rwkv_a1, rwkv_a2,
           rwkv_v0, rwkv_v1, rwkv_v2, rwkv_g1, rwkv_g2, rwkv_k_k, rwkv_k_a, rwkv_r_k,
           rwkv_gn_w, rwkv_gn_b, pool_w, pool_scale, ffn_w1, ffn_w2):
    bf = lambda a: a.astype(BF16)
    vec = lambda a: a.reshape(a.shape[0], 1, -1)
    p = {'norm_mix': vec(norm_mix), 'norm_ffn': vec(norm_ffn), 'norm_final': norm_final.reshape(1, -1),
         'mix': rwkv_mix, 'w_rkv': bf(rwkv_w_rkv), 'w_o': bf(rwkv_w_o), 'w0': vec(rwkv_w0),
         'w1': bf(rwkv_w1), 'w2': bf(rwkv_w2), 'a0': vec(rwkv_a0), 'a1': bf(rwkv_a1),
         'a2': bf(rwkv_a2), 'v0': vec(rwkv_v0), 'v1': bf(rwkv_v1), 'v2': bf(rwkv_v2),
         'g1': bf(rwkv_g1), 'g2': bf(rwkv_g2), 'k_k': vec(rwkv_k_k), 'k_a': vec(rwkv_k_a),
         'r_k': vec(rwkv_r_k), 'gn_w': vec(rwkv_gn_w), 'gn_b': vec(rwkv_gn_b), 'pool_w': bf(pool_w),
         'pool_scale': vec(pool_scale), 'ffn_w1': bf(ffn_w1), 'ffn_w2': bf(ffn_w2)}
    dt = x_prompt.dtype
    nb, _, d = x_prompt.shape
    n_rwkv = state_wkv.shape[0]
    n_pool = state_pool.shape[0]
    nh = d // HEAD_DIM
    z_shift = jnp.zeros((n_rwkv, nb, d), dt)
    z_wkv = jnp.zeros((n_rwkv, nb, nh, HEAD_DIM, HEAD_DIM), dt)
    z_pool = jnp.zeros((n_pool, nb, POOL_BUF, d), dt)
    y_p, wkv_p, shift_p, pool_p = _trunk(x_prompt, 0, z_shift, z_wkv, z_pool, p)
    y_s, wkv_s, shift_s, pool_s = _trunk(x_sample, PAST_LEN, state_shift, state_wkv, state_pool, p)
    return (y_p, y_s, wkv_p, shift_p, pool_p, wkv_s, shift_s, pool_s)
```

```python
import functools
import math

import jax
import jax.numpy as jnp
from jax import lax
from jax.experimental import pallas as pl
from jax.experimental.pallas import tpu as pltpu

HEAD_DIM = 64
LANES = 128
SUBLANES = 8
WKV_LANES = 512
WKV_BLOCK_ELEMS = 8 * 128 * 512
POOL_WINDOWS = (2, 4, 8, 16)
POOL_BUF = max(POOL_WINDOWS) - 1
PAST_LEN = 16384
NORM_EPS = 1e-6
GN_EPS = 64e-5
L2_EPS = 1e-12
WKV_CHUNK = 16
RELAYOUT_PITCH = 136
VMEM_LIMIT = 56 * 1024 * 1024

BF16 = jnp.bfloat16
F32 = jnp.float32


def _dot(a, b):
    return jnp.dot(a, b, preferred_element_type=F32)


def _dot_nt(a, b):
    return lax.dot_general(a, b, (((1,), (1,)), ((), ())), preferred_element_type=F32)


def _sigmoid(x):
    return 0.5 * jnp.tanh(0.5 * x) + 0.5


def _rmsnorm(x, g):
    ms = jnp.mean(x * x, axis=-1, keepdims=True)
    return x * lax.rsqrt(ms + NORM_EPS) * g


def _head_ones():
    r = lax.broadcasted_iota(jnp.int32, (LANES, LANES), 0) // HEAD_DIM
    c = lax.broadcasted_iota(jnp.int32, (LANES, LANES), 1) // HEAD_DIM
    return jnp.where(r == c, 1.0, 0.0).astype(BF16)


def _head_sum_bf16(p, ones):
    return _dot(p.astype(BF16), ones)


def _head_sum(p, ones):
    hi = p.astype(BF16)
    lo = (p - hi.astype(F32)).astype(BF16)
    return _dot(hi, ones) + _dot(lo, ones)


def _const_spec(shape):
    n = len(shape)
    return pl.BlockSpec(shape, lambda *_: (0,) * n)


def _pick(arr, *idx):
    shape = (None,) * len(idx) + arr.shape[len(idx):]
    at = tuple(idx) + (0,) * (arr.ndim - len(idx))
    return arr, pl.BlockSpec(shape, lambda *_: at)


def _rows_in(x_ref, xs_scr, nb):
    if len(x_ref.shape) == 2:
        return x_ref[...]
    tt = x_ref.shape[1]
    nblk = xs_scr.shape[0]
    for c in range(nblk):
        for b in range(nb):
            xs_scr[c, pl.ds(b, tt, stride=nb), :] = x_ref[b, :, c * LANES:(c + 1) * LANES]
    return jnp.concatenate([xs_scr[c] for c in range(nblk)], axis=1)


def _rows_out(o_ref, os_scr, val, nb):
    if len(o_ref.shape) == 2:
        o_ref[...] = val
        return
    tt = o_ref.shape[1]
    for c in range(os_scr.shape[0]):
        os_scr[c] = val[:, c * LANES:(c + 1) * LANES]
        for b in range(nb):
            o_ref[b, :, c * LANES:(c + 1) * LANES] = os_scr[c, pl.ds(b, tt, stride=nb), :]


def _x_spec(x, tm, nb):
    if x.ndim == 2:
        return pl.BlockSpec((tm, x.shape[1]), lambda i: (i, 0))
    return pl.BlockSpec((nb, tm // nb, x.shape[2]), lambda i: (0, i, 0))


def _slab_lanes(d):
    return WKV_LANES if d % WKV_LANES == 0 else LANES


def _pre_kernel(has_vres, nb, *refs):
    it = iter(refs)
    x_ref = next(it); shift_ref = next(it); gmix_ref = next(it); mix_ref = next(it)
    wr_ref = next(it); wk_ref = next(it); wv_ref = next(it)
    w0_ref = next(it); w1_ref = next(it); w2_ref = next(it)
    a0_ref = next(it); a1_ref = next(it); a2_ref = next(it)
    if has_vres:
        v0_ref = next(it); v1_ref = next(it); v2_ref = next(it); vfirst_ref = next(it)
    g1_ref = next(it); g2_ref = next(it); kk_ref = next(it); ka_ref = next(it)
    r_out = next(it); lw_out = next(it); k_out = next(it); v_out = next(it)
    kk_out = next(it); b_out = next(it); g_out = next(it); shift_out = next(it)
    carry = next(it); xs_scr = next(it)

    tm, d = g_out.shape
    nslab, _, slab = r_out.shape

    @pl.when(pl.program_id(0) == 0)
    def _():
        carry[...] = shift_ref[...]

    h = _rmsnorm(_rows_in(x_ref, xs_scr, nb), gmix_ref[...])
    if tm > nb:
        hp = jnp.concatenate([carry[...], h[: tm - nb]], axis=0)
    else:
        hp = carry[...]
    carry[...] = h[tm - nb:]
    shift_out[...] = h[tm - nb:]
    dx = hp - h

    def mixed(i):
        return (h + dx * mix_ref[i:i + 1, :]).astype(BF16)

    xv = mixed(2)
    lora_w = _dot(mixed(3), w1_ref[...])
    lora_a = _dot(mixed(4), a1_ref[...])
    if has_vres:
        lora_v = _dot(xv, v1_ref[...])
    lora_g = _dot(mixed(5), g1_ref[...])
    k = _dot(mixed(1), wk_ref[...])
    wpre = w0_ref[...] + _dot(jnp.tanh(lora_w).astype(BF16), w2_ref[...])
    a = _sigmoid(a0_ref[...] + _dot(lora_a.astype(BF16), a2_ref[...]))
    if has_vres:
        gate = _sigmoid(v0_ref[...] + _dot(lora_v.astype(BF16), v2_ref[...]))
    g = _dot(_sigmoid(lora_g).astype(BF16), g2_ref[...])
    g_out[...] = g
    lw = -math.exp(-0.5) * _sigmoid(wpre)

    ones = _head_ones()
    kk = k * kk_ref[...]
    k2 = k * (1.0 + (a - 1.0) * ka_ref[...])
    kkn = []
    for p in range(d // LANES):
        kkp = kk[:, p * LANES:(p + 1) * LANES]
        kkn.append(kkp * lax.rsqrt(jnp.maximum(_head_sum_bf16(kkp * kkp, ones), L2_EPS * L2_EPS)))
    kkn = jnp.concatenate(kkn, axis=1)
    bb = kkn * a
    v = _dot(xv, wv_ref[...])
    if has_vres:
        vfirst = jnp.concatenate([vfirst_ref[p] for p in range(nslab)], axis=1)
        v = v + (vfirst - v) * gate
    r = _dot(mixed(0), wr_ref[...])
    for p in range(nslab):
        sl = slice(p * slab, (p + 1) * slab)
        r_out[p] = r[:, sl]
        lw_out[p] = lw[:, sl]
        k_out[p] = k2[:, sl]
        v_out[p] = v[:, sl]
        kk_out[p] = kkn[:, sl]
        b_out[p] = bb[:, sl]


def _pre_call(x, p, i, j, shift_states, vfirst, nb, tm):
    d = x.shape[-1]
    rows = x.shape[0] if x.ndim == 2 else x.shape[0] * x.shape[1]
    slab = _slab_lanes(d)
    nslab = d // slab
    has_vres = j > 0
    sm_spec = pl.BlockSpec((nslab, tm, slab), lambda i_: (0, i_, 0))
    ops = [(x, _x_spec(x, tm, nb)), _pick(shift_states, j), _pick(p['norm_mix'], i), _pick(p['mix'], j),
           _pick(p['w_rkv'], j, 0), _pick(p['w_rkv'], j, 1), _pick(p['w_rkv'], j, 2),
           _pick(p['w0'], j), _pick(p['w1'], j), _pick(p['w2'], j),
           _pick(p['a0'], j), _pick(p['a1'], j), _pick(p['a2'], j)]
    if has_vres:
        ops += [_pick(p['v0'], j - 1), _pick(p['v1'], j - 1), _pick(p['v2'], j - 1), (vfirst, sm_spec)]
    ops += [_pick(p['g1'], j), _pick(p['g2'], j), _pick(p['k_k'], j), _pick(p['k_a'], j)]
    sm = jax.ShapeDtypeStruct((nslab, rows, slab), F32)
    out_shape = [sm] * 6 + [jax.ShapeDtypeStruct((rows, d), F32), jax.ShapeDtypeStruct((nb, d), F32)]
    out_specs = [sm_spec] * 6 + [pl.BlockSpec((tm, d), lambda i_: (i_, 0)), _const_spec((nb, d))]
    return pl.pallas_call(
        functools.partial(_pre_kernel, has_vres, nb),
        grid=(rows // tm,),
        in_specs=[o[1] for o in ops],
        out_specs=out_specs,
        out_shape=out_shape,
        scratch_shapes=[pltpu.VMEM((nb, d), F32), pltpu.VMEM((d // LANES, tm, LANES), F32)],
        compiler_params=pltpu.CompilerParams(
            dimension_semantics=("arbitrary",), vmem_limit_bytes=VMEM_LIMIT),
        name="rwkv_pre",
    )(*[o[0] for o in ops])


_WKV_SLOTS = ("at", "rt", "u", "v", "bh", "kh", "x0", "y0")


def _wkv_kernel(tchunk, nchunks, has_alias, layer, batch_minor, r_ref, lw_ref, k_ref, v_ref, kk_ref, b_ref,
                s0_ref, rk_ref, gnw_ref, gnb_ref, *rest):
    y_ref, sout_ref, s_scr, *work = rest[1:] if has_alias else rest
    if batch_minor:
        t_scr, *work = work
    nseq = SUBLANES
    npl = s_scr.shape[1]
    cp = max(tchunk, 8)
    n = cp * nseq
    nv = tchunk * nseq
    grp = LANES // (2 * cp)
    nslot = len(_WKV_SLOTS)
    scr = [dict(zip(_WKV_SLOTS, work[q * nslot:(q + 1) * nslot])) for q in range(npl)]

    if batch_minor:
        group = pl.program_id(1)

        @pl.when((group == 0) & (pl.program_id(2) == 0))
        def _():
            def relayout(i, carry):
                for q in range(npl):
                    for e in range(SUBLANES):
                        v = i * SUBLANES + e
                        tile = jnp.concatenate([s0_ref[2 * q, v], s0_ref[2 * q + 1, v]], axis=0)
                        t_scr[q, pl.ds(pl.multiple_of(v * RELAYOUT_PITCH, SUBLANES), LANES), :] = tile.T
                return carry
            lax.fori_loop(0, HEAD_DIM // SUBLANES, relayout, 0)

        @pl.when(pl.program_id(2) == 0)
        def _():
            lane = lax.broadcasted_iota(jnp.int32, (HEAD_DIM, LANES), 1)
            for s in range(nseq):
                for q in range(npl):
                    side = t_scr[q, pl.ds(group * nseq + s, HEAD_DIM, stride=RELAYOUT_PITCH), :]
                    s_scr[s, q] = jnp.concatenate([jnp.where(lane < HEAD_DIM, side, 0.0),
                                                   jnp.where(lane >= HEAD_DIM, side, 0.0)], axis=0)
    else:
        @pl.when(pl.program_id(2) == 0)
        def _():
            zero = jnp.zeros((HEAD_DIM, HEAD_DIM), F32)
            for s in range(nseq):
                for q in range(npl):
                    top = jnp.concatenate([s0_ref[s, 2 * q], zero], axis=1)
                    bot = jnp.concatenate([zero, s0_ref[s, 2 * q + 1]], axis=1)
                    s_scr[s, q] = jnp.concatenate([top, bot], axis=0)

    ones = _head_ones()
    hr = lax.broadcasted_iota(jnp.int32, (LANES, LANES), 0) // HEAD_DIM
    hc = lax.broadcasted_iota(jnp.int32, (LANES, LANES), 1) // HEAD_DIM
    same_head = hr == hc
    lane_head = lax.broadcasted_iota(jnp.int32, (1, LANES), 1) // HEAD_DIM
    head_f32 = [jnp.where(lane_head == hh, 1.0, 0.0) for hh in range(2)]
    head_bf16 = [m.astype(BF16) for m in head_f32]
    ri = lax.broadcasted_iota(jnp.int32, (n, 2 * n), 0)
    ci = lax.broadcasted_iota(jnp.int32, (n, 2 * n), 1) % n
    same_seq = (ri % nseq) == (ci % nseq)
    tri_incl = jnp.where(same_seq & (ci // nseq <= ri // nseq), 1.0, 0.0)
    tri_strict = jnp.where(same_seq & (ci // nseq < ri // nseq), 1.0, 0.0)[:, :n]
    half = cp // 2 if cp >= 16 else cp
    nh = half * nseq
    if half < cp:
        rh = lax.broadcasted_iota(jnp.int32, (n, nh), 0) % nseq
        ch = lax.broadcasted_iota(jnp.int32, (n, nh), 1) % nseq
        same_seq_half = jnp.where(rh == ch, 1.0, 0.0).astype(BF16)

    def load(ref, c, q):
        val = ref[pl.ds(c * tchunk, tchunk), :, q * LANES:(q + 1) * LANES].reshape(nv, LANES)
        if n > nv:
            val = jnp.concatenate([val, jnp.zeros((n - nv, LANES), F32)], axis=0)
        return val

    def tile_rows(slab, reps):
        return slab if reps == 1 else jnp.concatenate([slab] * reps, axis=0)

    def seq_rows(ref, s):
        return ref[pl.ds(s, cp, stride=nseq), :]

    def phase_a(c, q, st):
        w = scr[q]
        r = load(r_ref, c, q); lw = load(lw_ref, c, q); k = load(k_ref, c, q)
        v = load(v_ref, c, q); kk = load(kk_ref, c, q); b = load(b_ref, c, q)

        acc = jnp.zeros((nseq, LANES), F32)
        cums = []
        for t in range(cp):
            acc = acc + lw[t * nseq:(t + 1) * nseq]
            cums.append(acc)
        cum = jnp.concatenate(cums, axis=0)
        w_inc = jnp.exp(cum)
        w_inv = jnp.exp(-cum)
        w_exc = jnp.concatenate([jnp.ones((nseq, LANES), F32), w_inc[: n - nseq]], axis=0)
        w_tot = w_inc[n - nseq:]
        at = -kk * w_exc
        rt = r * w_inc
        kt = k * w_inv
        bt = b * w_inv
        w_tot_rows = tile_rows(w_tot, cp)
        st.update(r=r, k=k, v=v, at=at, rt=rt, w_tot=w_tot, bh=bt * w_tot_rows, kh=kt * w_tot_rows)

        at16 = at.astype(BF16)
        rt16 = rt.astype(BF16)
        kt16 = kt.astype(BF16)
        kb16 = jnp.concatenate([kt16, bt.astype(BF16)], axis=0)
        gram_a = _dot_nt(jnp.concatenate([at16 * head_bf16[0], at16 * head_bf16[1]], axis=0), kt16)
        gram_r = _dot_nt(jnp.concatenate([rt16 * head_bf16[0], rt16 * head_bf16[1]], axis=0), kb16)
        st["ga"] = jnp.concatenate([gram_a[:n] * tri_strict, gram_a[n:] * tri_strict], axis=1).astype(BF16)
        st["gr"] = jnp.concatenate([gram_r[:n] * tri_incl, gram_r[n:] * tri_incl], axis=1).astype(BF16)
        st["vm"] = jnp.concatenate([v * head_f32[0], v * head_f32[1]], axis=0)
        yield

        prods = []
        for t in range(1, cp):
            j0 = (t // half) * half
            if t > j0:
                prods.append(tile_rows(at[t * nseq:(t + 1) * nseq], t - j0) * bt[j0 * nseq:t * nseq])
        st["coef"] = _head_sum_bf16(jnp.concatenate(prods, axis=0), ones)
        if half < cp:
            lhs = jnp.concatenate([at16[nh:] * head_bf16[0], at16[nh:] * head_bf16[1]], axis=0)
            n21 = _dot_nt(lhs, bt.astype(BF16)[:nh]).astype(BF16) * same_seq_half
            st["n21"] = jnp.concatenate([n21[:nh], n21[nh:]], axis=1)

    def phase_b(c, q, st):
        w = scr[q]
        w["at"][...] = st["at"]
        w["rt"][...] = st["rt"]
        w["v"][...] = st["v"]
        w["bh"][...] = st["bh"]
        w["kh"][...] = st["kh"]
        for s in range(nseq):
            lhs = jnp.concatenate([seq_rows(w["at"], s), seq_rows(w["rt"], s)], axis=0).astype(BF16)
            out = _dot_nt(lhs, s_scr[s, q].astype(BF16))
            w["x0"][pl.ds(s, cp, stride=nseq), :] = out[:cp]
            w["y0"][pl.ds(s, cp, stride=nseq), :] = out[cp:]
        yield

        x = w["x0"][...] + _dot(st["ga"], st["vm"].astype(BF16))
        yield

        coef = st["coef"]
        us = []
        off = 0
        for t in range(cp):
            j0 = (t // half) * half
            if t == half:
                u1 = jnp.concatenate(us, axis=0).astype(BF16)
                x = jnp.concatenate([x[:nh], x[nh:] + _dot(
                    st["n21"], jnp.concatenate([u1 * head_bf16[0], u1 * head_bf16[1]], axis=0))], axis=0)
            u_t = x[t * nseq:(t + 1) * nseq]
            for j in range(j0, t):
                u_t = u_t + coef[off + (j - j0) * nseq:off + (j - j0 + 1) * nseq] * us[j]
            off += (t - j0) * nseq
            us.append(u_t)
        u = jnp.concatenate(us, axis=0)
        w["u"][...] = u
        yield

        vm = st["vm"]
        um = jnp.concatenate([u * head_f32[0], u * head_f32[1]], axis=0)
        vum = jnp.concatenate([vm[:n], um[:n], vm[n:], um[n:]], axis=0)
        st["y"] = w["y0"][...] + _dot(st["gr"], vum.astype(BF16))
        yield

        w_tot = st["w_tot"]
        zp = []
        for s in range(nseq):
            zp += [seq_rows(w["u"], s), seq_rows(w["v"], s)]
        z_t = jnp.concatenate(zp, axis=0).T.astype(BF16)
        for g0 in range(0, nseq, grp):
            cols = []
            for e in range(grp):
                gs = jnp.concatenate([seq_rows(w["bh"], g0 + e), seq_rows(w["kh"], g0 + e)],
                                     axis=0).astype(BF16)
                blk = [gs]
                if e > 0:
                    blk = [jnp.zeros((2 * cp * e, LANES), BF16)] + blk
                if e < grp - 1:
                    blk = blk + [jnp.zeros((2 * cp * (grp - 1 - e), LANES), BF16)]
                cols.append(jnp.concatenate(blk, axis=0))
            lhs = z_t[:, (g0 // grp) * LANES:(g0 // grp + 1) * LANES]
            delta = _dot(lhs, jnp.concatenate(cols, axis=1))
            for e in range(grp):
                s = g0 + e
                dl = delta[:, e * LANES:(e + 1) * LANES]
                s_scr[s, q] = s_scr[s, q] * w_tot[s:s + 1, :] + jnp.where(same_head, dl, 0.0)

    def phase_c(c, q, st):
        ql = slice(q * LANES, (q + 1) * LANES)
        yv, rv, kv, vv = st["y"][:nv], st["r"][:nv], st["k"][:nv], st["v"][:nv]
        hs = _head_sum_bf16(jnp.concatenate([yv, rv * kv * rk_ref[:, ql]], axis=0), ones)
        dlt = yv - hs[:nv] * (1.0 / HEAD_DIM)
        bonus = hs[nv:] * vv
        yield
        var = _head_sum_bf16(dlt * dlt, ones) * (1.0 / HEAD_DIM)
        yn = dlt * lax.rsqrt(var + GN_EPS) * gnw_ref[:, ql] + gnb_ref[:, ql]
        y_ref[pl.ds(c * tchunk, tchunk), :, ql] = (yn + bonus).reshape(tchunk, nseq, LANES)

    def advance(gens, nstages=1):
        for _ in range(nstages):
            for g in gens:
                next(g, None)

    def drain(gens):
        for _ in range(8):
            advance(gens)

    def chunk_group(cs):
        sts = [[{} for _ in range(npl)] for _ in cs]
        ga = [[phase_a(c, q, sts[i][q]) for q in range(npl)] for i, c in enumerate(cs)]
        gb = [[phase_b(c, q, sts[i][q]) for q in range(npl)] for i, c in enumerate(cs)]
        gc = [[phase_c(c, q, sts[i][q]) for q in range(npl)] for i, c in enumerate(cs)]
        drain(ga[0])
        for i in range(len(cs)):
            fill = []
            if i + 1 < len(cs):
                fill.append(ga[i + 1])
            if i > 0:
                fill.append(gc[i - 1])
            for _ in range(5):
                advance(gb[i])
                for f in fill:
                    advance(f)
            for f in fill:
                drain(f)
        drain(gc[-1])

    unroll = 2 if nchunks % 2 == 0 else 1

    def chunk(i, carry):
        chunk_group([i * unroll + e for e in range(unroll)])
        return carry

    lax.fori_loop(0, nchunks // unroll, chunk, 0)

    @pl.when(pl.program_id(2) == pl.num_programs(2) - 1)
    def _():
        out = sout_ref if has_alias else sout_ref.at[layer]
        if not has_alias:
            for other in range(sout_ref.shape[0]):
                if other != layer:
                    sout_ref[other] = jnp.zeros(sout_ref.shape[1:], F32)
        for s in range(nseq):
            for q in range(npl):
                out[s, 2 * q] = s_scr[s, q, :HEAD_DIM, :HEAD_DIM]
                out[s, 2 * q + 1] = s_scr[s, q, HEAD_DIM:, HEAD_DIM:]


def _wkv_call(r, lw, k, v, kk, b, states, new_states, layer, p, nb, nt):
    nslab, _, slab = r.shape
    npl = slab // LANES
    ngroup = nb // SUBLANES
    tchunk = min(WKV_CHUNK, nt)
    tb = min(WKV_BLOCK_ELEMS // (SUBLANES * slab), nt)
    nchunks = tb // tchunk
    assert nt % tb == 0 and tb % tchunk == 0 and nb % SUBLANES == 0
    view = lambda a: a.reshape(nslab, nt, ngroup, SUBLANES, slab)
    hps = 2 * npl
    batch_minor = nb == LANES
    if batch_minor:
        order = lambda f: (lambda p_, g, t: f(g, p_, t))
        grid = (nslab, ngroup, nt // tb)
        states_in = states.transpose(0, 2, 3, 4, 1)
        st_in_spec = pl.BlockSpec((None, hps, HEAD_DIM, HEAD_DIM, nb), lambda p_, g, t: (layer, p_, 0, 0, 0),
                                  pipeline_mode=pl.Buffered(1))
        relayout_scr = [pltpu.VMEM((npl, HEAD_DIM * RELAYOUT_PITCH, LANES), F32)]
    else:
        order = lambda f: f
        grid = (ngroup, nslab, nt // tb)
        states_in = states
        st_in_spec = pl.BlockSpec((None, SUBLANES, hps, HEAD_DIM, HEAD_DIM),
                                  lambda g, p_, t: (layer, g, p_, 0, 0))
        relayout_scr = []
    act_spec = pl.BlockSpec((None, tb, None, SUBLANES, slab), order(lambda g, p_, t: (p_, t, g, 0, 0)))
    has_alias = new_states is not None
    if has_alias:
        st_out_spec = pl.BlockSpec((None, SUBLANES, hps, HEAD_DIM, HEAD_DIM),
                                   order(lambda g, p_, t: (layer, g, p_, 0, 0)))
    else:
        st_out_spec = pl.BlockSpec((states.shape[0], SUBLANES, hps, HEAD_DIM, HEAD_DIM),
                                   order(lambda g, p_, t: (0, g, p_, 0, 0)))
    extra_in = [new_states] if has_alias else []
    extra_spec = [pl.BlockSpec(memory_space=pl.ANY)] if has_alias else []
    par_spec = pl.BlockSpec((None, 1, slab), order(lambda g, p_, t: (layer, 0, p_)))
    n = max(tchunk, 8) * SUBLANES
    y, s_out = pl.pallas_call(
        functools.partial(_wkv_kernel, tchunk, nchunks, has_alias, layer, batch_minor),
        grid=grid,
        in_specs=[act_spec] * 6 + [st_in_spec] + [par_spec] * 3 + extra_spec,
        out_specs=[act_spec, st_out_spec],
        out_shape=[jax.ShapeDtypeStruct((nslab, nt, ngroup, SUBLANES, slab), F32),
                   jax.ShapeDtypeStruct(states.shape, F32)],
        input_output_aliases={10: 1} if has_alias else {},
        scratch_shapes=[pltpu.VMEM((SUBLANES, npl, LANES, LANES), F32)] + relayout_scr
        + [pltpu.VMEM((n, LANES), F32)] * (len(_WKV_SLOTS) * npl),
        compiler_params=pltpu.CompilerParams(
            dimension_semantics=("arbitrary", "arbitrary", "arbitrary"),
            vmem_limit_bytes=VMEM_LIMIT),
        name="wkv7",
    )(view(r), view(lw), view(k), view(v), view(kk), view(b), states_in, p['r_k'], p['gn_w'], p['gn_b'],
      *extra_in)
    return y.reshape(nslab, nt * nb, slab), s_out


def _ffn(x, gffn, w1_ref, w2_ref):
    h = _rmsnorm(x, gffn).astype(BF16)
    dff = w1_ref.shape[1]
    step = min(dff, 1024)
    acc = x
    for c in range(dff // step):
        hh = _dot(h, w1_ref[:, c * step:(c + 1) * step])
        hh = jnp.square(jnp.maximum(hh, 0.0)).astype(BF16)
        acc = acc + _dot(hh, w2_ref[c * step:(c + 1) * step, :])
    return acc


def _attn_ffn_kernel(final_norm, nb, x_ref, y_ref, g_ref, wo_ref, gffn_ref, w1_ref, w2_ref, gfin_ref,
                     o_ref, xs_scr, os_scr):
    nslab = y_ref.shape[0]
    y = jnp.concatenate([y_ref[q] for q in range(nslab)], axis=1)
    x = _rows_in(x_ref, xs_scr, nb) + _dot((y * g_ref[...]).astype(BF16), wo_ref[...])
    out = _ffn(x, gffn_ref[...], w1_ref, w2_ref)
    if final_norm:
        out = _rmsnorm(out, gfin_ref[...])
    _rows_out(o_ref, os_scr, out, nb)


def _io_scratch(tm, d, x, batch_major_out):
    full, tiny = (d // LANES, tm, LANES), (1, SUBLANES, LANES)
    return [pltpu.VMEM(full if x.ndim == 3 else tiny, F32),
            pltpu.VMEM(full if batch_major_out else tiny, F32)]


def _out_x(rows, d, tm, nb, batch_major_out):
    if batch_major_out:
        return (jax.ShapeDtypeStruct((nb, rows // nb, d), F32),
                pl.BlockSpec((nb, tm // nb, d), lambda i: (0, i, 0)))
    return jax.ShapeDtypeStruct((rows, d), F32), pl.BlockSpec((tm, d), lambda i: (i, 0))


def _attn_ffn_call(x, y, g, p, i, j, final_norm, batch_major_out, nb, tm):
    rows, d = g.shape
    nslab, _, slab = y.shape
    out_sds, out_spec = _out_x(rows, d, tm, nb, batch_major_out)
    ops = [(x, _x_spec(x, tm, nb)), (y, pl.BlockSpec((nslab, tm, slab), lambda i_: (0, i_, 0))),
           (g, pl.BlockSpec((tm, d), lambda i_: (i_, 0))), _pick(p['w_o'], j), _pick(p['norm_ffn'], i),
           _pick(p['ffn_w1'], i), _pick(p['ffn_w2'], i), _pick(p['norm_final'])]
    return pl.pallas_call(
        functools.partial(_attn_ffn_kernel, final_norm, nb),
        grid=(rows // tm,),
        in_specs=[o[1] for o in ops],
        out_specs=out_spec,
        out_shape=out_sds,
        scratch_shapes=_io_scratch(tm, d, x, batch_major_out),
        compiler_params=pltpu.CompilerParams(
            dimension_semantics=("arbitrary",), vmem_limit_bytes=VMEM_LIMIT),
        name="attn_out_ffn",
    )(*[o[0] for o in ops])


def _pool_ffn_kernel(final_norm, nb, start_pos, x_ref, buf_ref, gmix_ref, pw_ref, ps_ref,
                     gffn_ref, w1_ref, w2_ref, gfin_ref, o_ref, buf_out, carry, xs_scr, os_scr):
    d = gmix_ref.shape[-1]
    tm = x_ref.shape[0] if len(x_ref.shape) == 2 else nb * x_ref.shape[1]
    ngrp = len(POOL_WINDOWS)
    gw = d // ngrp

    @pl.when(pl.program_id(0) == 0)
    def _():
        carry[...] = buf_ref[...]

    x = _rows_in(x_ref, xs_scr, nb)
    h = _rmsnorm(x, gmix_ref[...])
    ext = jnp.concatenate([carry[...], h], axis=0)
    carry[...] = ext[tm:]
    buf_out[...] = ext[tm:]

    rowi = lax.broadcasted_iota(jnp.int32, (tm, gw), 0) + pl.program_id(0) * tm
    pos = start_pos + jnp.right_shift(rowi, int(math.log2(nb)))
    ys = []
    for gi, w in enumerate(POOL_WINDOWS):
        sl = slice(gi * gw, (gi + 1) * gw)
        s = ext[:, sl]
        span = 1
        while span < w:
            s = s[span * nb:] + s[: s.shape[0] - span * nb]
            span *= 2
        s = s[s.shape[0] - tm:]
        cnt = jnp.minimum(pos + 1, w).astype(F32)
        diff = (s / cnt - h[:, sl]).astype(BF16)
        ys.append(_dot(diff, pw_ref[gi]))
    x = x + jnp.concatenate(ys, axis=1) * ps_ref[...]
    out = _ffn(x, gffn_ref[...], w1_ref, w2_ref)
    if final_norm:
        out = _rmsnorm(out, gfin_ref[...])
    _rows_out(o_ref, os_scr, out, nb)


def _pool_ffn_call(x, buf, p, i, j, final_norm, batch_major_out, nb, start_pos, tm):
    d = x.shape[-1]
    rows = x.shape[0] if x.ndim == 2 else x.shape[0] * x.shape[1]
    assert nb & (nb - 1) == 0
    out_sds, out_spec = _out_x(rows, d, tm, nb, batch_major_out)
    ops = [(x, _x_spec(x, tm, nb)), (buf, _const_spec(buf.shape)), _pick(p['norm_mix'], i),
           _pick(p['pool_w'], j), _pick(p['pool_scale'], j), _pick(p['norm_ffn'], i),
           _pick(p['ffn_w1'], i), _pick(p['ffn_w2'], i), _pick(p['norm_final'])]
    return pl.pallas_call(
        functools.partial(_pool_ffn_kernel, final_norm, nb, start_pos),
        grid=(rows // tm,),
        in_specs=[o[1] for o in ops],
        out_specs=[out_spec, _const_spec(buf.shape)],
        out_shape=[out_sds, jax.ShapeDtypeStruct(buf.shape, F32)],
        scratch_shapes=[pltpu.VMEM(buf.shape, F32)] + _io_scratch(tm, d, x, batch_major_out),
        compiler_params=pltpu.CompilerParams(
            dimension_semantics=("arbitrary",), vmem_limit_bytes=VMEM_LIMIT),
        name="pool_ffn",
    )(*[o[0] for o in ops])


def _trunk(x, start_pos, shift_states, wkv_states, pool_bufs, p):
    nb, nt, d = x.shape
    depth = p['norm_mix'].shape[0]
    rows = nb * nt
    tm = min(512, rows)
    assert rows % tm == 0 and tm % nb == 0
    batch_major = nb == SUBLANES
    xt = x if batch_major else x.transpose(1, 0, 2).reshape(rows, d)
    new_shift, new_wkv, new_pool = [], None, []
    v_first = None
    for i in range(depth):
        j = i // 2
        last = i == depth - 1
        bm_out = batch_major and last
        if i % 2 == 0:
            r, lw, k, v, kk, b, g, s_shift = _pre_call(xt, p, i, j, shift_states, v_first, nb, tm)
            if j == 0:
                v_first = v
            y, new_wkv = _wkv_call(r, lw, k, v, kk, b, wkv_states, new_wkv, j, p, nb, nt)
            new_shift.append(s_shift)
            xt = _attn_ffn_call(xt, y, g, p, i, j, last, bm_out, nb, tm)
        else:
            buf = pool_bufs[j].transpose(1, 0, 2).reshape(POOL_BUF * nb, d)
            xt, s_buf = _pool_ffn_call(xt, buf, p, i, j, last, bm_out, nb, start_pos, tm)
            new_pool.append(s_buf.reshape(POOL_BUF, nb, d).transpose(1, 0, 2))
    y = xt if batch_major else xt.reshape(nt, nb, d).transpose(1, 0, 2)
    return y, new_wkv, jnp.stack(new_shift), jnp.stack(new_pool)


def kernel(x_prompt, x_sample, state_wkv, state_shift, state_pool, norm_mix, norm_ffn, norm_final,
           rwkv_mix, rwkv_w_rkv, rwkv_w_o, rwkv_w0, rwkv_w1, rwkv_w2, rwkv_a0, rwkv_a1, rwkv_a2,
           rwkv_v0, rwkv_v1, rwkv_v2, rwkv_g1, rwkv_g2, rwkv_k_k, rwkv_k_a, rwkv_r_k,
           rwkv_gn_w, rwkv_gn_b, pool_w, pool_scale, ffn_w1, ffn_w2):
    bf = lambda a: a.astype(BF16)
    vec = lambda a: a.reshape(a.shape[0], 1, -1)
    p = {'norm_mix': vec(norm_mix), 'norm_ffn': vec(norm_ffn), 'norm_final': norm_final.reshape(1, -1),
         'mix': rwkv_mix, 'w_rkv': bf(rwkv_w_rkv), 'w_o': bf(rwkv_w_o), 'w0': vec(rwkv_w0),
         'w1': bf(rwkv_w1), 'w2': bf(rwkv_w2), 'a0': vec(rwkv_a0), 'a1': bf(rwkv_a1),
         'a2': bf(rwkv_a2), 'v0': vec(rwkv_v0), 'v1': bf(rwkv_v1), 'v2': bf(rwkv_v2),
         'g1': bf(rwkv_g1), 'g2': bf(rwkv_g2), 'k_k': vec(rwkv_k_k), 'k_a': vec(rwkv_k_a),
         'r_k': vec(rwkv_r_k), 'gn_w': vec(rwkv_gn_w), 'gn_b': vec(rwkv_gn_b), 'pool_w': bf(pool_w),
         'pool_scale': vec(pool_scale), 'ffn_w1': bf(ffn_w1), 'ffn_w2': bf(ffn_w2)}
    dt = x_prompt.dtype
    nb, _, d = x_prompt.shape
    n_rwkv = state_wkv.shape[0]
    n_pool = state_pool.shape[0]
    nh = d // HEAD_DIM
    z_shift = jnp.zeros((n_rwkv, nb, d), dt)
    z_wkv = jnp.zeros((n_rwkv, nb, nh, HEAD_DIM, HEAD_DIM), dt)
    z_pool = jnp.zeros((n_pool, nb, POOL_BUF, d), dt)
    y_p, wkv_p, shift_p, pool_p = _trunk(x_prompt, 0, z_shift, z_wkv, z_pool, p)
    y_s, wkv_s, shift_s, pool_s = _trunk(x_sample, PAST_LEN, state_shift, state_wkv, state_pool, p)
    return (y_p, y_s, wkv_p, shift_p, pool_p, wkv_s, shift_s, pool_s)
```

```python
import functools
import math

import jax
import jax.numpy as jnp
from jax import lax
from jax.experimental import pallas as pl
from jax.experimental.pallas import tpu as pltpu

HEAD_DIM = 64
LANES = 128
SUBLANES = 8
WKV_LANES = 512
WKV_BLOCK_ELEMS = 8 * 128 * 512
POOL_WINDOWS = (2, 4, 8, 16)
POOL_BUF = max(POOL_WINDOWS) - 1
PAST_LEN = 16384
NORM_EPS = 1e-6
GN_EPS = 64e-5
L2_EPS = 1e-12
ROW_BLOCK = 512
FFN_CHUNK = 1024
WKV_CHUNK = 16
RELAYOUT_PITCH = 136
VMEM_LIMIT = 56 * 1024 * 1024

BF16 = jnp.bfloat16
F32 = jnp.float32


def _dot(a, b):
    return jnp.dot(a, b, preferred_element_type=F32)


def _dot_nt(a, b):
    return lax.dot_general(a, b, (((1,), (1,)), ((), ())), preferred_element_type=F32)


def _sigmoid(x):
    return 0.5 * jnp.tanh(0.5 * x) + 0.5


def _rmsnorm(x, g):
    ms = jnp.mean(x * x, axis=-1, keepdims=True)
    return x * lax.rsqrt(ms + NORM_EPS) * g


def _head_ones():
    r = lax.broadcasted_iota(jnp.int32, (LANES, LANES), 0) // HEAD_DIM
    c = lax.broadcasted_iota(jnp.int32, (LANES, LANES), 1) // HEAD_DIM
    return jnp.where(r == c, 1.0, 0.0).astype(BF16)


def _head_sum_bf16(p, ones):
    return _dot(p.astype(BF16), ones)


def _const_spec(shape):
    n = len(shape)
    return pl.BlockSpec(shape, lambda *_: (0,) * n)


def _pick(arr, *idx):
    shape = (None,) * len(idx) + arr.shape[len(idx):]
    at = tuple(idx) + (0,) * (arr.ndim - len(idx))
    return arr, pl.BlockSpec(shape, lambda *_: at)


def _rows_in(x_ref, xs_scr, nb):
    if len(x_ref.shape) == 2:
        return x_ref[...]
    tt = x_ref.shape[1]
    nblk = xs_scr.shape[0]
    for c in range(nblk):
        for b in range(nb):
            xs_scr[c, pl.ds(b, tt, stride=nb), :] = x_ref[b, :, c * LANES:(c + 1) * LANES]
    return jnp.concatenate([xs_scr[c] for c in range(nblk)], axis=1)


def _rows_out(o_ref, os_scr, val, nb):
    if len(o_ref.shape) == 2:
        o_ref[...] = val
        return
    tt = o_ref.shape[1]
    for c in range(os_scr.shape[0]):
        os_scr[c] = val[:, c * LANES:(c + 1) * LANES]
        for b in range(nb):
            o_ref[b, :, c * LANES:(c + 1) * LANES] = os_scr[c, pl.ds(b, tt, stride=nb), :]


def _x_spec(x, tm, nb):
    if x.ndim == 2:
        return pl.BlockSpec((tm, x.shape[1]), lambda i: (i, 0))
    return pl.BlockSpec((nb, tm // nb, x.shape[2]), lambda i: (0, i, 0))


def _slab_lanes(d):
    return WKV_LANES if d % WKV_LANES == 0 else LANES


def _pre_kernel(has_vres, nb, *refs):
    it = iter(refs)
    x_ref = next(it); shift_ref = next(it); gmix_ref = next(it); mix_ref = next(it)
    wr_ref = next(it); wk_ref = next(it); wv_ref = next(it)
    w0_ref = next(it); w1_ref = next(it); w2_ref = next(it)
    a0_ref = next(it); a1_ref = next(it); a2_ref = next(it)
    if has_vres:
        v0_ref = next(it); v1_ref = next(it); v2_ref = next(it); vfirst_ref = next(it)
    g1_ref = next(it); g2_ref = next(it); kk_ref = next(it); ka_ref = next(it)
    r_out = next(it); lw_out = next(it); k_out = next(it); v_out = next(it)
    kk_out = next(it); b_out = next(it); g_out = next(it); shift_out = next(it)
    carry = next(it); xs_scr = next(it)

    tm, d = g_out.shape
    nslab, _, slab = r_out.shape

    @pl.when(pl.program_id(0) == 0)
    def _():
        carry[...] = shift_ref[...]

    h = _rmsnorm(_rows_in(x_ref, xs_scr, nb), gmix_ref[...])
    if tm > nb:
        hp = jnp.concatenate([carry[...], h[: tm - nb]], axis=0)
    else:
        hp = carry[...]
    carry[...] = h[tm - nb:]
    shift_out[...] = h[tm - nb:]
    dx = hp - h

    def mixed(i):
        return (h + dx * mix_ref[i:i + 1, :]).astype(BF16)

    xv = mixed(2)
    lora_w = _dot(mixed(3), w1_ref[...])
    lora_a = _dot(mixed(4), a1_ref[...])
    if has_vres:
        lora_v = _dot(xv, v1_ref[...])
    lora_g = _dot(mixed(5), g1_ref[...])
    k = _dot(mixed(1), wk_ref[...])
    wpre = w0_ref[...] + _dot(jnp.tanh(lora_w).astype(BF16), w2_ref[...])
    a = _sigmoid(a0_ref[...] + _dot(lora_a.astype(BF16), a2_ref[...]))
    if has_vres:
        gate = _sigmoid(v0_ref[...] + _dot(lora_v.astype(BF16), v2_ref[...]))
    g = _dot(_sigmoid(lora_g).astype(BF16), g2_ref[...])
    g_out[...] = g
    lw = -math.exp(-0.5) * _sigmoid(wpre)

    ones = _head_ones()
    kk = k * kk_ref[...]
    k2 = k * (1.0 + (a - 1.0) * ka_ref[...])
    kkn = []
    for p in range(d // LANES):
        kkp = kk[:, p * LANES:(p + 1) * LANES]
        kkn.append(kkp * lax.rsqrt(jnp.maximum(_head_sum_bf16(kkp * kkp, ones), L2_EPS * L2_EPS)))
    kkn = jnp.concatenate(kkn, axis=1)
    bb = kkn * a
    v = _dot(xv, wv_ref[...])
    if has_vres:
        vfirst = jnp.concatenate([vfirst_ref[p] for p in range(nslab)], axis=1)
        v = v + (vfirst - v) * gate
    r = _dot(mixed(0), wr_ref[...])
    for p in range(nslab):
        sl = slice(p * slab, (p + 1) * slab)
        r_out[p] = r[:, sl]
        lw_out[p] = lw[:, sl]
        k_out[p] = k2[:, sl]
        v_out[p] = v[:, sl]
        kk_out[p] = kkn[:, sl]
        b_out[p] = bb[:, sl]


def _pre_call(x, p, i, j, shift_states, vfirst, nb, tm):
    d = x.shape[-1]
    rows = x.shape[0] if x.ndim == 2 else x.shape[0] * x.shape[1]
    slab = _slab_lanes(d)
    nslab = d // slab
    has_vres = j > 0
    sm_spec = pl.BlockSpec((nslab, tm, slab), lambda i_: (0, i_, 0))
    ops = [(x, _x_spec(x, tm, nb)), _pick(shift_states, j), _pick(p['norm_mix'], i), _pick(p['mix'], j),
           _pick(p['w_rkv'], j, 0), _pick(p['w_rkv'], j, 1), _pick(p['w_rkv'], j, 2),
           _pick(p['w0'], j), _pick(p['w1'], j), _pick(p['w2'], j),
           _pick(p['a0'], j), _pick(p['a1'], j), _pick(p['a2'], j)]
    if has_vres:
        ops += [_pick(p['v0'], j - 1), _pick(p['v1'], j - 1), _pick(p['v2'], j - 1), (vfirst, sm_spec)]
    ops += [_pick(p['g1'], j), _pick(p['g2'], j), _pick(p['k_k'], j), _pick(p['k_a'], j)]
    sm = jax.ShapeDtypeStruct((nslab, rows, slab), F32)
    out_shape = [sm] * 6 + [jax.ShapeDtypeStruct((rows, d), F32), jax.ShapeDtypeStruct((nb, d), F32)]
    out_specs = [sm_spec] * 6 + [pl.BlockSpec((tm, d), lambda i_: (i_, 0)), _const_spec((nb, d))]
    return pl.pallas_call(
        functools.partial(_pre_kernel, has_vres, nb),
        grid=(rows // tm,),
        in_specs=[o[1] for o in ops],
        out_specs=out_specs,
        out_shape=out_shape,
        scratch_shapes=[pltpu.VMEM((nb, d), F32), pltpu.VMEM((d // LANES, tm, LANES), F32)],
        compiler_params=pltpu.CompilerParams(
            dimension_semantics=("arbitrary",), vmem_limit_bytes=VMEM_LIMIT),
        name="rwkv_pre",
    )(*[o[0] for o in ops])


_WKV_SLOTS = ("at", "rt", "u", "v", "bh", "kh", "x0", "y0")


def _wkv_kernel(tchunk, nchunks, has_alias, layer, batch_minor, r_ref, lw_ref, k_ref, v_ref, kk_ref, b_ref,
                s0_ref, rk_ref, gnw_ref, gnb_ref, *rest):
    y_ref, sout_ref, s_scr, *work = rest[1:] if has_alias else rest
    if batch_minor:
        t_scr, *work = work
    nseq = SUBLANES
    npl = s_scr.shape[1]
    cp = max(tchunk, 8)
    n = cp * nseq
    nv = tchunk * nseq
    grp = LANES // (2 * cp)
    nslot = len(_WKV_SLOTS)
    scr = [dict(zip(_WKV_SLOTS, work[q * nslot:(q + 1) * nslot])) for q in range(npl)]

    if batch_minor:
        group = pl.program_id(1)

        @pl.when((group == 0) & (pl.program_id(2) == 0))
        def _():
            def relayout(i, carry):
                for q in range(npl):
                    for e in range(SUBLANES):
                        v = i * SUBLANES + e
                        tile = jnp.concatenate([s0_ref[2 * q, v], s0_ref[2 * q + 1, v]], axis=0)
                        t_scr[q, pl.ds(pl.multiple_of(v * RELAYOUT_PITCH, SUBLANES), LANES), :] = tile.T
                return carry
            lax.fori_loop(0, HEAD_DIM // SUBLANES, relayout, 0)

        @pl.when(pl.program_id(2) == 0)
        def _():
            lane = lax.broadcasted_iota(jnp.int32, (HEAD_DIM, LANES), 1)
            for s in range(nseq):
                for q in range(npl):
                    side = t_scr[q, pl.ds(group * nseq + s, HEAD_DIM, stride=RELAYOUT_PITCH), :]
                    s_scr[s, q] = jnp.concatenate([jnp.where(lane < HEAD_DIM, side, 0.0),
                                                   jnp.where(lane >= HEAD_DIM, side, 0.0)], axis=0)
    else:
        @pl.when(pl.program_id(2) == 0)
        def _():
            zero = jnp.zeros((HEAD_DIM, HEAD_DIM), F32)
            for s in range(nseq):
                for q in range(npl):
                    top = jnp.concatenate([s0_ref[s, 2 * q], zero], axis=1)
                    bot = jnp.concatenate([zero, s0_ref[s, 2 * q + 1]], axis=1)
                    s_scr[s, q] = jnp.concatenate([top, bot], axis=0)

    ones = _head_ones()
    hr = lax.broadcasted_iota(jnp.int32, (LANES, LANES), 0) // HEAD_DIM
    hc = lax.broadcasted_iota(jnp.int32, (LANES, LANES), 1) // HEAD_DIM
    same_head = hr == hc
    lane_head = lax.broadcasted_iota(jnp.int32, (1, LANES), 1) // HEAD_DIM
    head_f32 = [jnp.where(lane_head == hh, 1.0, 0.0) for hh in range(2)]
    head_bf16 = [m.astype(BF16) for m in head_f32]
    ri = lax.broadcasted_iota(jnp.int32, (n, 2 * n), 0)
    ci = lax.broadcasted_iota(jnp.int32, (n, 2 * n), 1) % n
    same_seq = (ri % nseq) == (ci % nseq)
    tri_incl = jnp.where(same_seq & (ci // nseq <= ri // nseq), 1.0, 0.0)
    tri_strict = jnp.where(same_seq & (ci // nseq < ri // nseq), 1.0, 0.0)[:, :n]
    half = cp // 2 if cp >= 16 else cp
    nh = half * nseq
    if half < cp:
        rh = lax.broadcasted_iota(jnp.int32, (n, nh), 0) % nseq
        ch = lax.broadcasted_iota(jnp.int32, (n, nh), 1) % nseq
        same_seq_half = jnp.where(rh == ch, 1.0, 0.0).astype(BF16)

    def load(ref, c, q):
        val = ref[pl.ds(c * tchunk, tchunk), :, q * LANES:(q + 1) * LANES].reshape(nv, LANES)
        if n > nv:
            val = jnp.concatenate([val, jnp.zeros((n - nv, LANES), F32)], axis=0)
        return val

    def tile_rows(slab, reps):
        return slab if reps == 1 else jnp.concatenate([slab] * reps, axis=0)

    def seq_rows(ref, s):
        return ref[pl.ds(s, cp, stride=nseq), :]

    def phase_a(c, q, st):
        r = load(r_ref, c, q); lw = load(lw_ref, c, q); k = load(k_ref, c, q)
        v = load(v_ref, c, q); kk = load(kk_ref, c, q); b = load(b_ref, c, q)

        acc = jnp.zeros((nseq, LANES), F32)
        cums = []
        for t in range(cp):
            acc = acc + lw[t * nseq:(t + 1) * nseq]
            cums.append(acc)
        cum = jnp.concatenate(cums, axis=0)
        w_inc = jnp.exp(cum)
        w_inv = jnp.exp(-cum)
        w_exc = jnp.concatenate([jnp.ones((nseq, LANES), F32), w_inc[: n - nseq]], axis=0)
        w_tot = w_inc[n - nseq:]
        at = -kk * w_exc
        rt = r * w_inc
        kt = k * w_inv
        bt = b * w_inv
        w_tot_rows = tile_rows(w_tot, cp)
        st.update(r=r, k=k, v=v, at=at, rt=rt, w_tot=w_tot, bh=bt * w_tot_rows, kh=kt * w_tot_rows)

        at16 = at.astype(BF16)
        rt16 = rt.astype(BF16)
        kt16 = kt.astype(BF16)
        kb16 = jnp.concatenate([kt16, bt.astype(BF16)], axis=0)
        gram_a = _dot_nt(jnp.concatenate([at16 * head_bf16[0], at16 * head_bf16[1]], axis=0), kt16)
        gram_r = _dot_nt(jnp.concatenate([rt16 * head_bf16[0], rt16 * head_bf16[1]], axis=0), kb16)
        st["ga"] = jnp.concatenate([gram_a[:n] * tri_strict, gram_a[n:] * tri_strict], axis=1).astype(BF16)
        st["gr"] = jnp.concatenate([gram_r[:n] * tri_incl, gram_r[n:] * tri_incl], axis=1).astype(BF16)
        st["vm"] = jnp.concatenate([v * head_f32[0], v * head_f32[1]], axis=0)
        yield

        prods = []
        for t in range(1, cp):
            j0 = (t // half) * half
            if t > j0:
                prods.append(tile_rows(at[t * nseq:(t + 1) * nseq], t - j0) * bt[j0 * nseq:t * nseq])
        st["coef"] = _head_sum_bf16(jnp.concatenate(prods, axis=0), ones)
        if half < cp:
            lhs = jnp.concatenate([at16[nh:] * head_bf16[0], at16[nh:] * head_bf16[1]], axis=0)
            n21 = _dot_nt(lhs, bt.astype(BF16)[:nh]).astype(BF16) * same_seq_half
            st["n21"] = jnp.concatenate([n21[:nh], n21[nh:]], axis=1)

    def phase_b(c, q, st):
        w = scr[q]
        w["at"][...] = st["at"]
        w["rt"][...] = st["rt"]
        w["v"][...] = st["v"]
        w["bh"][...] = st["bh"]
        w["kh"][...] = st["kh"]
        for s in range(nseq):
            lhs = jnp.concatenate([seq_rows(w["at"], s), seq_rows(w["rt"], s)], axis=0).astype(BF16)
            out = _dot_nt(lhs, s_scr[s, q].astype(BF16))
            w["x0"][pl.ds(s, cp, stride=nseq), :] = out[:cp]
            w["y0"][pl.ds(s, cp, stride=nseq), :] = out[cp:]
        yield

        x = w["x0"][...] + _dot(st["ga"], st["vm"].astype(BF16))
        yield

        coef = st["coef"]
        us = []
        off = 0
        for t in range(cp):
            j0 = (t // half) * half
            if t == half:
                u1 = jnp.concatenate(us, axis=0).astype(BF16)
                x = jnp.concatenate([x[:nh], x[nh:] + _dot(
                    st["n21"], jnp.concatenate([u1 * head_bf16[0], u1 * head_bf16[1]], axis=0))], axis=0)
            u_t = x[t * nseq:(t + 1) * nseq]
            for j in range(j0, t):
                u_t = u_t + coef[off + (j - j0) * nseq:off + (j - j0 + 1) * nseq] * us[j]
            off += (t - j0) * nseq
            us.append(u_t)
        u = jnp.concatenate(us, axis=0)
        w["u"][...] = u
        yield

        vm = st["vm"]
        um = jnp.concatenate([u * head_f32[0], u * head_f32[1]], axis=0)
        vum = jnp.concatenate([vm[:n], um[:n], vm[n:], um[n:]], axis=0)
        st["y"] = w["y0"][...] + _dot(st["gr"], vum.astype(BF16))
        yield

        w_tot = st["w_tot"]
        zp = []
        for s in range(nseq):
            zp += [seq_rows(w["u"], s), seq_rows(w["v"], s)]
        z_t = jnp.concatenate(zp, axis=0).T.astype(BF16)
        for g0 in range(0, nseq, grp):
            cols = []
            for e in range(grp):
                gs = jnp.concatenate([seq_rows(w["bh"], g0 + e), seq_rows(w["kh"], g0 + e)],
                                     axis=0).astype(BF16)
                blk = [gs]
                if e > 0:
                    blk = [jnp.zeros((2 * cp * e, LANES), BF16)] + blk
                if e < grp - 1:
                    blk = blk + [jnp.zeros((2 * cp * (grp - 1 - e), LANES), BF16)]
                cols.append(jnp.concatenate(blk, axis=0))
            lhs = z_t[:, (g0 // grp) * LANES:(g0 // grp + 1) * LANES]
            delta = _dot(lhs, jnp.concatenate(cols, axis=1))
            for e in range(grp):
                s = g0 + e
                dl = delta[:, e * LANES:(e + 1) * LANES]
                s_scr[s, q] = s_scr[s, q] * w_tot[s:s + 1, :] + jnp.where(same_head, dl, 0.0)

    def phase_c(c, q, st):
        ql = slice(q * LANES, (q + 1) * LANES)
        yv, rv, kv, vv = st["y"][:nv], st["r"][:nv], st["k"][:nv], st["v"][:nv]
        hs = _head_sum_bf16(jnp.concatenate([yv, rv * kv * rk_ref[:, ql]], axis=0), ones)
        dlt = yv - hs[:nv] * (1.0 / HEAD_DIM)
        bonus = hs[nv:] * vv
        yield
        var = _head_sum_bf16(dlt * dlt, ones) * (1.0 / HEAD_DIM)
        yn = dlt * lax.rsqrt(var + GN_EPS) * gnw_ref[:, ql] + gnb_ref[:, ql]
        y_ref[pl.ds(c * tchunk, tchunk), :, ql] = (yn + bonus).reshape(tchunk, nseq, LANES)

    def advance(gens):
        return [next(g, "done") is None for g in gens]

    def drain(gens):
        while any(advance(gens)):
            pass

    def chunk_group(cs):
        sts = [[{} for _ in range(npl)] for _ in cs]
        ga = [[phase_a(c, q, sts[i][q]) for q in range(npl)] for i, c in enumerate(cs)]
        gb = [[phase_b(c, q, sts[i][q]) for q in range(npl)] for i, c in enumerate(cs)]
        gc = [[phase_c(c, q, sts[i][q]) for q in range(npl)] for i, c in enumerate(cs)]
        drain(ga[0])
        for i in range(len(cs)):
            fill = []
            if i + 1 < len(cs):
                fill.append(ga[i + 1])
            if i > 0:
                fill.append(gc[i - 1])
            while any(advance(gb[i])):
                for f in fill:
                    advance(f)
            for f in fill:
                drain(f)
        drain(gc[-1])

    unroll = 2 if nchunks % 2 == 0 else 1

    def chunk(i, carry):
        chunk_group([i * unroll + e for e in range(unroll)])
        return carry

    lax.fori_loop(0, nchunks // unroll, chunk, 0)

    @pl.when(pl.program_id(2) == pl.num_programs(2) - 1)
    def _():
        out = sout_ref if has_alias else sout_ref.at[layer]
        if not has_alias:
            for other in range(sout_ref.shape[0]):
                if other != layer:
                    sout_ref[other] = jnp.zeros(sout_ref.shape[1:], F32)
        for s in range(nseq):
            for q in range(npl):
                out[s, 2 * q] = s_scr[s, q, :HEAD_DIM, :HEAD_DIM]
                out[s, 2 * q + 1] = s_scr[s, q, HEAD_DIM:, HEAD_DIM:]


def _wkv_call(r, lw, k, v, kk, b, states, new_states, layer, p, nb, nt):
    nslab, _, slab = r.shape
    npl = slab // LANES
    ngroup = nb // SUBLANES
    tchunk = min(WKV_CHUNK, nt)
    tb = min(WKV_BLOCK_ELEMS // (SUBLANES * slab), nt)
    nchunks = tb // tchunk
    assert nt % tb == 0 and tb % tchunk == 0 and nb % SUBLANES == 0
    view = lambda a: a.reshape(nslab, nt, ngroup, SUBLANES, slab)
    hps = 2 * npl
    batch_minor = nb == LANES
    if batch_minor:
        order = lambda f: (lambda p_, g, t: f(g, p_, t))
        grid = (nslab, ngroup, nt // tb)
        states_in = states.transpose(0, 2, 3, 4, 1)
        st_in_spec = pl.BlockSpec((None, hps, HEAD_DIM, HEAD_DIM, nb), lambda p_, g, t: (layer, p_, 0, 0, 0),
                                  pipeline_mode=pl.Buffered(1))
        relayout_scr = [pltpu.VMEM((npl, HEAD_DIM * RELAYOUT_PITCH, LANES), F32)]
    else:
        order = lambda f: f
        grid = (ngroup, nslab, nt // tb)
        states_in = states
        st_in_spec = pl.BlockSpec((None, SUBLANES, hps, HEAD_DIM, HEAD_DIM),
                                  lambda g, p_, t: (layer, g, p_, 0, 0))
        relayout_scr = []
    act_spec = pl.BlockSpec((None, tb, None, SUBLANES, slab), order(lambda g, p_, t: (p_, t, g, 0, 0)))
    has_alias = new_states is not None
    if has_alias:
        st_out_spec = pl.BlockSpec((None, SUBLANES, hps, HEAD_DIM, HEAD_DIM),
                                   order(lambda g, p_, t: (layer, g, p_, 0, 0)))
    else:
        st_out_spec = pl.BlockSpec((states.shape[0], SUBLANES, hps, HEAD_DIM, HEAD_DIM),
                                   order(lambda g, p_, t: (0, g, p_, 0, 0)))
    extra_in = [new_states] if has_alias else []
    extra_spec = [pl.BlockSpec(memory_space=pl.ANY)] if has_alias else []
    par_spec = pl.BlockSpec((None, 1, slab), order(lambda g, p_, t: (layer, 0, p_)))
    n = max(tchunk, 8) * SUBLANES
    y, s_out = pl.pallas_call(
        functools.partial(_wkv_kernel, tchunk, nchunks, has_alias, layer, batch_minor),
        grid=grid,
        in_specs=[act_spec] * 6 + [st_in_spec] + [par_spec] * 3 + extra_spec,
        out_specs=[act_spec, st_out_spec],
        out_shape=[jax.ShapeDtypeStruct((nslab, nt, ngroup, SUBLANES, slab), F32),
                   jax.ShapeDtypeStruct(states.shape, F32)],
        input_output_aliases={10: 1} if has_alias else {},
        scratch_shapes=[pltpu.VMEM((SUBLANES, npl, LANES, LANES), F32)] + relayout_scr
        + [pltpu.VMEM((n, LANES), F32)] * (len(_WKV_SLOTS) * npl),
        compiler_params=pltpu.CompilerParams(
            dimension_semantics=("arbitrary", "arbitrary", "arbitrary"),
            vmem_limit_bytes=VMEM_LIMIT),
        name="wkv7",
    )(view(r), view(lw), view(k), view(v), view(kk), view(b), states_in, p['r_k'], p['gn_w'], p['gn_b'],
      *extra_in)
    return y.reshape(nslab, nt * nb, slab), s_out


def _ffn(x, gffn, w1_ref, w2_ref):
    h = _rmsnorm(x, gffn).astype(BF16)
    dff = w1_ref.shape[1]
    step = min(dff, FFN_CHUNK)
    acc = x
    for c in range(dff // step):
        hh = _dot(h, w1_ref[:, c * step:(c + 1) * step])
        hh = jnp.square(jnp.maximum(hh, 0.0)).astype(BF16)
        acc = acc + _dot(hh, w2_ref[c * step:(c + 1) * step, :])
    return acc


def _attn_ffn_kernel(final_norm, nb, x_ref, y_ref, g_ref, wo_ref, gffn_ref, w1_ref, w2_ref, gfin_ref,
                     o_ref, xs_scr, os_scr):
    nslab = y_ref.shape[0]
    y = jnp.concatenate([y_ref[q] for q in range(nslab)], axis=1)
    x = _rows_in(x_ref, xs_scr, nb) + _dot((y * g_ref[...]).astype(BF16), wo_ref[...])
    out = _ffn(x, gffn_ref[...], w1_ref, w2_ref)
    if final_norm:
        out = _rmsnorm(out, gfin_ref[...])
    _rows_out(o_ref, os_scr, out, nb)


def _io_scratch(tm, d, x, batch_major_out):
    full, tiny = (d // LANES, tm, LANES), (1, SUBLANES, LANES)
    return [pltpu.VMEM(full if x.ndim == 3 else tiny, F32),
            pltpu.VMEM(full if batch_major_out else tiny, F32)]


def _out_x(rows, d, tm, nb, batch_major_out):
    if batch_major_out:
        return (jax.ShapeDtypeStruct((nb, rows // nb, d), F32),
                pl.BlockSpec((nb, tm // nb, d), lambda i: (0, i, 0)))
    return jax.ShapeDtypeStruct((rows, d), F32), pl.BlockSpec((tm, d), lambda i: (i, 0))


def _attn_ffn_call(x, y, g, p, i, j, final_norm, batch_major_out, nb, tm):
    rows, d = g.shape
    nslab, _, slab = y.shape
    out_sds, out_spec = _out_x(rows, d, tm, nb, batch_major_out)
    ops = [(x, _x_spec(x, tm, nb)), (y, pl.BlockSpec((nslab, tm, slab), lambda i_: (0, i_, 0))),
           (g, pl.BlockSpec((tm, d), lambda i_: (i_, 0))), _pick(p['w_o'], j), _pick(p['norm_ffn'], i),
           _pick(p['ffn_w1'], i), _pick(p['ffn_w2'], i), _pick(p['norm_final'])]
    return pl.pallas_call(
        functools.partial(_attn_ffn_kernel, final_norm, nb),
        grid=(rows // tm,),
        in_specs=[o[1] for o in ops],
        out_specs=out_spec,
        out_shape=out_sds,
        scratch_shapes=_io_scratch(tm, d, x, batch_major_out),
        compiler_params=pltpu.CompilerParams(
            dimension_semantics=("arbitrary",), vmem_limit_bytes=VMEM_LIMIT),
        name="attn_out_ffn",
    )(*[o[0] for o in ops])


def _pool_ffn_kernel(final_norm, nb, start_pos, x_ref, buf_ref, gmix_ref, pw_ref, ps_ref,
                     gffn_ref, w1_ref, w2_ref, gfin_ref, o_ref, buf_out, carry, xs_scr, os_scr):
    d = gmix_ref.shape[-1]
    tm = x_ref.shape[0] if len(x_ref.shape) == 2 else nb * x_ref.shape[1]
    ngrp = len(POOL_WINDOWS)
    gw = d // ngrp

    @pl.when(pl.program_id(0) == 0)
    def _():
        carry[...] = buf_ref[...]

    x = _rows_in(x_ref, xs_scr, nb)
    h = _rmsnorm(x, gmix_ref[...])
    ext = jnp.concatenate([carry[...], h], axis=0)
    carry[...] = ext[tm:]
    buf_out[...] = ext[tm:]

    rowi = lax.broadcasted_iota(jnp.int32, (tm, gw), 0) + pl.program_id(0) * tm
    pos = start_pos + jnp.right_shift(rowi, int(math.log2(nb)))
    ys = []
    for gi, w in enumerate(POOL_WINDOWS):
        sl = slice(gi * gw, (gi + 1) * gw)
        s = ext[:, sl]
        span = 1
        while span < w:
            s = s[span * nb:] + s[: s.shape[0] - span * nb]
            span *= 2
        s = s[s.shape[0] - tm:]
        cnt = jnp.minimum(pos + 1, w).astype(F32)
        diff = (s / cnt - h[:, sl]).astype(BF16)
        ys.append(_dot(diff, pw_ref[gi]))
    x = x + jnp.concatenate(ys, axis=1) * ps_ref[...]
    out = _ffn(x, gffn_ref[...], w1_ref, w2_ref)
    if final_norm:
        out = _rmsnorm(out, gfin_ref[...])
    _rows_out(o_ref, os_scr, out, nb)


def _pool_ffn_call(x, buf, p, i, j, final_norm, batch_major_out, nb, start_pos, tm):
    d = x.shape[-1]
    rows = x.shape[0] if x.ndim == 2 else x.shape[0] * x.shape[1]
    assert nb & (nb - 1) == 0
    out_sds, out_spec = _out_x(rows, d, tm, nb, batch_major_out)
    ops = [(x, _x_spec(x, tm, nb)), (buf, _const_spec(buf.shape)), _pick(p['norm_mix'], i),
           _pick(p['pool_w'], j), _pick(p['pool_scale'], j), _pick(p['norm_ffn'], i),
           _pick(p['ffn_w1'], i), _pick(p['ffn_w2'], i), _pick(p['norm_final'])]
    return pl.pallas_call(
        functools.partial(_pool_ffn_kernel, final_norm, nb, start_pos),
        grid=(rows // tm,),
        in_specs=[o[1] for o in ops],
        out_specs=[out_spec, _const_spec(buf.shape)],
        out_shape=[out_sds, jax.ShapeDtypeStruct(buf.shape, F32)],
        scratch_shapes=[pltpu.VMEM(buf.shape, F32)] + _io_scratch(tm, d, x, batch_major_out),
        compiler_params=pltpu.CompilerParams(
            dimension_semantics=("arbitrary",), vmem_limit_bytes=VMEM_LIMIT),
        name="pool_ffn",
    )(*[o[0] for o in ops])


def _trunk(x, start_pos, shift_states, wkv_states, pool_bufs, p):
    nb, nt, d = x.shape
    depth = p['norm_mix'].shape[0]
    rows = nb * nt
    tm = min(ROW_BLOCK, rows)
    assert rows % tm == 0 and tm % nb == 0
    batch_major = nb == SUBLANES
    xt = x if batch_major else x.transpose(1, 0, 2).reshape(rows, d)
    new_shift, new_wkv, new_pool = [], None, []
    v_first = None
    for i in range(depth):
        j = i // 2
        last = i == depth - 1
        bm_out = batch_major and last
        if i % 2 == 0:
            r, lw, k, v, kk, b, g, s_shift = _pre_call(xt, p, i, j, shift_states, v_first, nb, tm)
            if j == 0:
                v_first = v
            y, new_wkv = _wkv_call(r, lw, k, v, kk, b, wkv_states, new_wkv, j, p, nb, nt)
            new_shift.append(s_shift)
            xt = _attn_ffn_call(xt, y, g, p, i, j, last, bm_out, nb, tm)
        else:
            buf = pool_bufs[j].transpose(1, 0, 2).reshape(POOL_BUF * nb, d)
            xt, s_buf = _pool_ffn_call(xt, buf, p, i, j, last, bm_out, nb, start_pos, tm)
            new_pool.append(s_buf.reshape(POOL_BUF, nb, d).transpose(1, 0, 2))
    y = xt if batch_major else xt.reshape(nt, nb, d).transpose(1, 0, 2)
    return y, new_wkv, jnp.stack(new_shift), jnp.stack(new_pool)


def kernel(x_prompt, x_sample, state_wkv, state_shift, state_pool, norm_mix, norm_ffn, norm_final,
           rwkv_mix, rwkv_w_rkv, rwkv_w_o, rwkv_w0, rwkv_w1, rwkv_w2, rwkv_a0, rwkv_a1, rwkv_a2,
           rwkv_v0, rwkv_v1, rwkv_v2, rwkv_g1, rwkv_g2, rwkv_k_k, rwkv_k_a, rwkv_r_k,
           rwkv_gn_w, rwkv_gn_b, pool_w, pool_scale, ffn_w1, ffn_w2):
    bf = lambda a: a.astype(BF16)
    vec = lambda a: a.reshape(a.shape[0], 1, -1)
    p = {'norm_mix': vec(norm_mix), 'norm_ffn': vec(norm_ffn), 'norm_final': norm_final.reshape(1, -1),
         'mix': rwkv_mix, 'w_rkv': bf(rwkv_w_rkv), 'w_o': bf(rwkv_w_o), 'w0': vec(rwkv_w0),
         'w1': bf(rwkv_w1), 'w2': bf(rwkv_w2), 'a0': vec(rwkv_a0), 'a1': bf(rwkv_a1),
         'a2': bf(rwkv_a2), 'v0': vec(rwkv_v0), 'v1': bf(rwkv_v1), 'v2': bf(rwkv_v2),
         'g1': bf(rwkv_g1), 'g2': bf(rwkv_g2), 'k_k': vec(rwkv_k_k), 'k_a': vec(rwkv_k_a),
         'r_k': vec(rwkv_r_k), 'gn_w': vec(rwkv_gn_w), 'gn_b': vec(rwkv_gn_b), 'pool_w': bf(pool_w),
         'pool_scale': vec(pool_scale), 'ffn_w1': bf(ffn_w1), 'ffn_w2': bf(ffn_w2)}
    dt = x_prompt.dtype
    nb, _, d = x_prompt.shape
    n_rwkv = state_wkv.shape[0]
    n_pool = state_pool.shape[0]
    nh = d // HEAD_DIM
    z_shift = jnp.zeros((n_rwkv, nb, d), dt)
    z_wkv = jnp.zeros((n_rwkv, nb, nh, HEAD_DIM, HEAD_DIM), dt)
    z_pool = jnp.zeros((n_pool, nb, POOL_BUF, d), dt)
    y_p, wkv_p, shift_p, pool_p = _trunk(x_prompt, 0, z_shift, z_wkv, z_pool, p)
    y_s, wkv_s, shift_s, pool_s = _trunk(x_sample, PAST_LEN, state_shift, state_wkv, state_pool, p)
    return (y_p, y_s, wkv_p, shift_p, pool_p, wkv_s, shift_s, pool_s)
```

```python
import functools
import math

import jax
import jax.numpy as jnp
from jax import lax
from jax.experimental import pallas as pl
from jax.experimental.pallas import tpu as pltpu

HEAD_DIM = 64
LANES = 128
SUBLANES = 8
WKV_LANES = 512
WKV_BLOCK_ELEMS = 8 * 128 * 512
POOL_WINDOWS = (2, 4, 8, 16)
POOL_BUF = max(POOL_WINDOWS) - 1
PAST_LEN = 16384
NORM_EPS = 1e-6
GN_EPS = 64e-5
L2_EPS = 1e-12
ROW_BLOCK = 512
FFN_CHUNK = 1024
WKV_CHUNK = 16
RELAYOUT_PITCH = 136
VMEM_LIMIT = 56 * 1024 * 1024

BF16 = jnp.bfloat16
F32 = jnp.float32


def _dot(a, b):
    return jnp.dot(a, b, preferred_element_type=F32)


def _dot_nt(a, b):
    return lax.dot_general(a, b, (((1,), (1,)), ((), ())), preferred_element_type=F32)


def _sigmoid(x):
    return 0.5 * jnp.tanh(0.5 * x) + 0.5


def _rmsnorm(x, g):
    ms = jnp.mean(x * x, axis=-1, keepdims=True)
    return x * lax.rsqrt(ms + NORM_EPS) * g


def _head_ones():
    r = lax.broadcasted_iota(jnp.int32, (LANES, LANES), 0) // HEAD_DIM
    c = lax.broadcasted_iota(jnp.int32, (LANES, LANES), 1) // HEAD_DIM
    return jnp.where(r == c, 1.0, 0.0).astype(BF16)


def _head_sum_bf16(p, ones):
    return _dot(p.astype(BF16), ones)


def _const_spec(shape):
    n = len(shape)
    return pl.BlockSpec(shape, lambda *_: (0,) * n)


def _pick(arr, *idx):
    shape = (None,) * len(idx) + arr.shape[len(idx):]
    at = tuple(idx) + (0,) * (arr.ndim - len(idx))
    return arr, pl.BlockSpec(shape, lambda *_: at)


def _rows_in(x_ref, nb):
    if len(x_ref.shape) == 2:
        return x_ref[...]
    return jnp.swapaxes(x_ref[...], 0, 1).reshape(nb * x_ref.shape[1], x_ref.shape[2])


def _rows_out(o_ref, val, nb):
    if len(o_ref.shape) == 2:
        o_ref[...] = val
    else:
        o_ref[...] = jnp.swapaxes(val.reshape(o_ref.shape[1], nb, val.shape[1]), 0, 1)


def _x_spec(x, tm, nb):
    if x.ndim == 2:
        return pl.BlockSpec((tm, x.shape[1]), lambda i: (i, 0))
    return pl.BlockSpec((nb, tm // nb, x.shape[2]), lambda i: (0, i, 0))


def _slab_lanes(d):
    return WKV_LANES if d % WKV_LANES == 0 else LANES


def _pre_kernel(has_vres, nb, *refs):
    it = iter(refs)
    x_ref = next(it); shift_ref = next(it); gmix_ref = next(it); mix_ref = next(it)
    wr_ref = next(it); wk_ref = next(it); wv_ref = next(it)
    w0_ref = next(it); w1_ref = next(it); w2_ref = next(it)
    a0_ref = next(it); a1_ref = next(it); a2_ref = next(it)
    if has_vres:
        v0_ref = next(it); v1_ref = next(it); v2_ref = next(it); vfirst_ref = next(it)
    g1_ref = next(it); g2_ref = next(it); kk_ref = next(it); ka_ref = next(it)
    r_out = next(it); lw_out = next(it); k_out = next(it); v_out = next(it)
    kk_out = next(it); b_out = next(it); g_out = next(it); shift_out = next(it)
    carry = next(it)

    tm, d = g_out.shape
    nslab, _, slab = r_out.shape

    @pl.when(pl.program_id(0) == 0)
    def _():
        carry[...] = shift_ref[...]

    h = _rmsnorm(_rows_in(x_ref, nb), gmix_ref[...])
    if tm > nb:
        hp = jnp.concatenate([carry[...], h[: tm - nb]], axis=0)
    else:
        hp = carry[...]
    carry[...] = h[tm - nb:]
    shift_out[...] = h[tm - nb:]
    dx = hp - h

    def mixed(i):
        return (h + dx * mix_ref[i:i + 1, :]).astype(BF16)

    xv = mixed(2)
    lora_w = _dot(mixed(3), w1_ref[...])
    lora_a = _dot(mixed(4), a1_ref[...])
    if has_vres:
        lora_v = _dot(xv, v1_ref[...])
    lora_g = _dot(mixed(5), g1_ref[...])
    k = _dot(mixed(1), wk_ref[...])
    wpre = w0_ref[...] + _dot(jnp.tanh(lora_w).astype(BF16), w2_ref[...])
    a = _sigmoid(a0_ref[...] + _dot(lora_a.astype(BF16), a2_ref[...]))
    if has_vres:
        gate = _sigmoid(v0_ref[...] + _dot(lora_v.astype(BF16), v2_ref[...]))
    g = _dot(_sigmoid(lora_g).astype(BF16), g2_ref[...])
    g_out[...] = g
    lw = -math.exp(-0.5) * _sigmoid(wpre)

    ones = _head_ones()
    kk = k * kk_ref[...]
    k2 = k * (1.0 + (a - 1.0) * ka_ref[...])
    kkn = []
    for p in range(d // LANES):
        kkp = kk[:, p * LANES:(p + 1) * LANES]
        kkn.append(kkp * lax.rsqrt(jnp.maximum(_head_sum_bf16(kkp * kkp, ones), L2_EPS * L2_EPS)))
    kkn = jnp.concatenate(kkn, axis=1)
    bb = kkn * a
    v = _dot(xv, wv_ref[...])
    if has_vres:
        vfirst = jnp.concatenate([vfirst_ref[p] for p in range(nslab)], axis=1)
        v = v + (vfirst - v) * gate
    r = _dot(mixed(0), wr_ref[...])
    for p in range(nslab):
        sl = slice(p * slab, (p + 1) * slab)
        r_out[p] = r[:, sl]
        lw_out[p] = lw[:, sl]
        k_out[p] = k2[:, sl]
        v_out[p] = v[:, sl]
        kk_out[p] = kkn[:, sl]
        b_out[p] = bb[:, sl]


def _pre_call(x, p, i, j, shift_states, vfirst, nb, tm):
    d = x.shape[-1]
    rows = x.shape[0] if x.ndim == 2 else x.shape[0] * x.shape[1]
    slab = _slab_lanes(d)
    nslab = d // slab
    has_vres = j > 0
    sm_spec = pl.BlockSpec((nslab, tm, slab), lambda i_: (0, i_, 0))
    ops = [(x, _x_spec(x, tm, nb)), _pick(shift_states, j), _pick(p['norm_mix'], i), _pick(p['mix'], j),
           _pick(p['w_rkv'], j, 0), _pick(p['w_rkv'], j, 1), _pick(p['w_rkv'], j, 2),
           _pick(p['w0'], j), _pick(p['w1'], j), _pick(p['w2'], j),
           _pick(p['a0'], j), _pick(p['a1'], j), _pick(p['a2'], j)]
    if has_vres:
        ops += [_pick(p['v0'], j - 1), _pick(p['v1'], j - 1), _pick(p['v2'], j - 1), (vfirst, sm_spec)]
    ops += [_pick(p['g1'], j), _pick(p['g2'], j), _pick(p['k_k'], j), _pick(p['k_a'], j)]
    sm = jax.ShapeDtypeStruct((nslab, rows, slab), F32)
    out_shape = [sm] * 6 + [jax.ShapeDtypeStruct((rows, d), F32), jax.ShapeDtypeStruct((nb, d), F32)]
    out_specs = [sm_spec] * 6 + [pl.BlockSpec((tm, d), lambda i_: (i_, 0)), _const_spec((nb, d))]
    return pl.pallas_call(
        functools.partial(_pre_kernel, has_vres, nb),
        grid=(rows // tm,),
        in_specs=[o[1] for o in ops],
        out_specs=out_specs,
        out_shape=out_shape,
        scratch_shapes=[pltpu.VMEM((nb, d), F32)],
        compiler_params=pltpu.CompilerParams(
            dimension_semantics=("arbitrary",), vmem_limit_bytes=VMEM_LIMIT),
        name="rwkv_pre",
    )(*[o[0] for o in ops])


_WKV_SLOTS = ("at", "rt", "u", "v", "bh", "kh", "x0", "y0")


def _wkv_kernel(tchunk, nchunks, has_alias, layer, batch_minor, r_ref, lw_ref, k_ref, v_ref, kk_ref, b_ref,
                s0_ref, rk_ref, gnw_ref, gnb_ref, *rest):
    y_ref, sout_ref, s_scr, *work = rest[1:] if has_alias else rest
    if batch_minor:
        t_scr, *work = work
    nseq = SUBLANES
    npl = s_scr.shape[1]
    cp = max(tchunk, 8)
    n = cp * nseq
    nv = tchunk * nseq
    grp = LANES // (2 * cp)
    nslot = len(_WKV_SLOTS)
    scr = [dict(zip(_WKV_SLOTS, work[q * nslot:(q + 1) * nslot])) for q in range(npl)]

    if batch_minor:
        group = pl.program_id(1)

        @pl.when((group == 0) & (pl.program_id(2) == 0))
        def _():
            def relayout(i, carry):
                for q in range(npl):
                    for e in range(SUBLANES):
                        v = i * SUBLANES + e
                        tile = jnp.concatenate([s0_ref[2 * q, v], s0_ref[2 * q + 1, v]], axis=0)
                        t_scr[q, pl.ds(pl.multiple_of(v * RELAYOUT_PITCH, SUBLANES), LANES), :] = tile.T
                return carry
            lax.fori_loop(0, HEAD_DIM // SUBLANES, relayout, 0)

        @pl.when(pl.program_id(2) == 0)
        def _():
            lane = lax.broadcasted_iota(jnp.int32, (HEAD_DIM, LANES), 1)
            for s in range(nseq):
                for q in range(npl):
                    side = t_scr[q, pl.ds(group * nseq + s, HEAD_DIM, stride=RELAYOUT_PITCH), :]
                    s_scr[s, q] = jnp.concatenate([jnp.where(lane < HEAD_DIM, side, 0.0),
                                                   jnp.where(lane >= HEAD_DIM, side, 0.0)], axis=0)
    else:
        @pl.when(pl.program_id(2) == 0)
        def _():
            zero = jnp.zeros((HEAD_DIM, HEAD_DIM), F32)
            for s in range(nseq):
                for q in range(npl):
                    top = jnp.concatenate([s0_ref[s, 2 * q], zero], axis=1)
                    bot = jnp.concatenate([zero, s0_ref[s, 2 * q + 1]], axis=1)
                    s_scr[s, q] = jnp.concatenate([top, bot], axis=0)

    ones = _head_ones()
    hr = lax.broadcasted_iota(jnp.int32, (LANES, LANES), 0) // HEAD_DIM
    hc = lax.broadcasted_iota(jnp.int32, (LANES, LANES), 1) // HEAD_DIM
    same_head = hr == hc
    lane_head = lax.broadcasted_iota(jnp.int32, (1, LANES), 1) // HEAD_DIM
    head_f32 = [jnp.where(lane_head == hh, 1.0, 0.0) for hh in range(2)]
    head_bf16 = [m.astype(BF16) for m in head_f32]
    ri = lax.broadcasted_iota(jnp.int32, (n, 2 * n), 0)
    ci = lax.broadcasted_iota(jnp.int32, (n, 2 * n), 1) % n
    same_seq = (ri % nseq) == (ci % nseq)
    tri_incl = jnp.where(same_seq & (ci // nseq <= ri // nseq), 1.0, 0.0)
    tri_strict = jnp.where(same_seq & (ci // nseq < ri // nseq), 1.0, 0.0)[:, :n]
    half = cp // 2 if cp >= 16 else cp
    nh = half * nseq
    if half < cp:
        rh = lax.broadcasted_iota(jnp.int32, (n, nh), 0) % nseq
        ch = lax.broadcasted_iota(jnp.int32, (n, nh), 1) % nseq
        same_seq_half = jnp.where(rh == ch, 1.0, 0.0).astype(BF16)

    def load(ref, c, q):
        val = ref[pl.ds(c * tchunk, tchunk), :, q * LANES:(q + 1) * LANES].reshape(nv, LANES)
        if n > nv:
            val = jnp.concatenate([val, jnp.zeros((n - nv, LANES), F32)], axis=0)
        return val

    def tile_rows(slab, reps):
        return slab if reps == 1 else jnp.concatenate([slab] * reps, axis=0)

    def seq_rows(ref, s):
        return ref[pl.ds(s, cp, stride=nseq), :]

    def phase_a(c, q, st):
        r = load(r_ref, c, q); lw = load(lw_ref, c, q); k = load(k_ref, c, q)
        v = load(v_ref, c, q); kk = load(kk_ref, c, q); b = load(b_ref, c, q)

        acc = jnp.zeros((nseq, LANES), F32)
        cums = []
        for t in range(cp):
            acc = acc + lw[t * nseq:(t + 1) * nseq]
            cums.append(acc)
        cum = jnp.concatenate(cums, axis=0)
        w_inc = jnp.exp(cum)
        w_inv = jnp.exp(-cum)
        w_exc = jnp.concatenate([jnp.ones((nseq, LANES), F32), w_inc[: n - nseq]], axis=0)
        w_tot = w_inc[n - nseq:]
        at = -kk * w_exc
        rt = r * w_inc
        kt = k * w_inv
        bt = b * w_inv
        w_tot_rows = tile_rows(w_tot, cp)
        st.update(r=r, k=k, v=v, at=at, rt=rt, w_tot=w_tot, bh=bt * w_tot_rows, kh=kt * w_tot_rows)

        at16 = at.astype(BF16)
        rt16 = rt.astype(BF16)
        kt16 = kt.astype(BF16)
        kb16 = jnp.concatenate([kt16, bt.astype(BF16)], axis=0)
        gram_a = _dot_nt(jnp.concatenate([at16 * head_bf16[0], at16 * head_bf16[1]], axis=0), kt16)
        gram_r = _dot_nt(jnp.concatenate([rt16 * head_bf16[0], rt16 * head_bf16[1]], axis=0), kb16)
        st["ga"] = jnp.concatenate([gram_a[:n] * tri_strict, gram_a[n:] * tri_strict], axis=1).astype(BF16)
        st["gr"] = jnp.concatenate([gram_r[:n] * tri_incl, gram_r[n:] * tri_incl], axis=1).astype(BF16)
        st["vm"] = jnp.concatenate([v * head_f32[0], v * head_f32[1]], axis=0)
        yield

        prods = []
        for t in range(1, cp):
            j0 = (t // half) * half
            if t > j0:
                prods.append(tile_rows(at[t * nseq:(t + 1) * nseq], t - j0) * bt[j0 * nseq:t * nseq])
        st["coef"] = _head_sum_bf16(jnp.concatenate(prods, axis=0), ones)
        if half < cp:
            lhs = jnp.concatenate([at16[nh:] * head_bf16[0], at16[nh:] * head_bf16[1]], axis=0)
            n21 = _dot_nt(lhs, bt.astype(BF16)[:nh]).astype(BF16) * same_seq_half
            st["n21"] = jnp.concatenate([n21[:nh], n21[nh:]], axis=1)

    def phase_b(c, q, st):
        w = scr[q]
        w["at"][...] = st["at"]
        w["rt"][...] = st["rt"]
        w["v"][...] = st["v"]
        w["bh"][...] = st["bh"]
        w["kh"][...] = st["kh"]
        for s in range(nseq):
            lhs = jnp.concatenate([seq_rows(w["at"], s), seq_rows(w["rt"], s)], axis=0).astype(BF16)
            out = _dot_nt(lhs, s_scr[s, q].astype(BF16))
            w["x0"][pl.ds(s, cp, stride=nseq), :] = out[:cp]
            w["y0"][pl.ds(s, cp, stride=nseq), :] = out[cp:]
        yield

        x = w["x0"][...] + _dot(st["ga"], st["vm"].astype(BF16))
        yield

        coef = st["coef"]
        us = []
        off = 0
        for t in range(cp):
            j0 = (t // half) * half
            if t == half:
                u1 = jnp.concatenate(us, axis=0).astype(BF16)
                x = jnp.concatenate([x[:nh], x[nh:] + _dot(
                    st["n21"], jnp.concatenate([u1 * head_bf16[0], u1 * head_bf16[1]], axis=0))], axis=0)
            u_t = x[t * nseq:(t + 1) * nseq]
            for j in range(j0, t):
                u_t = u_t + coef[off + (j - j0) * nseq:off + (j - j0 + 1) * nseq] * us[j]
            off += (t - j0) * nseq
            us.append(u_t)
        u = jnp.concatenate(us, axis=0)
        w["u"][...] = u
        yield

        vm = st["vm"]
        um = jnp.concatenate([u * head_f32[0], u * head_f32[1]], axis=0)
        vum = jnp.concatenate([vm[:n], um[:n], vm[n:], um[n:]], axis=0)
        st["y"] = w["y0"][...] + _dot(st["gr"], vum.astype(BF16))
        yield

        w_tot = st["w_tot"]
        zp = []
        for s in range(nseq):
            zp += [seq_rows(w["u"], s), seq_rows(w["v"], s)]
        z_t = jnp.concatenate(zp, axis=0).T.astype(BF16)
        for g0 in range(0, nseq, grp):
            cols = []
            for e in range(grp):
                gs = jnp.concatenate([seq_rows(w["bh"], g0 + e), seq_rows(w["kh"], g0 + e)],
                                     axis=0).astype(BF16)
                blk = [gs]
                if e > 0:
                    blk = [jnp.zeros((2 * cp * e, LANES), BF16)] + blk
                if e < grp - 1:
                    blk = blk + [jnp.zeros((2 * cp * (grp - 1 - e), LANES), BF16)]
                cols.append(jnp.concatenate(blk, axis=0))
            lhs = z_t[:, (g0 // grp) * LANES:(g0 // grp + 1) * LANES]
            delta = _dot(lhs, jnp.concatenate(cols, axis=1))
            for e in range(grp):
                s = g0 + e
                dl = delta[:, e * LANES:(e + 1) * LANES]
                s_scr[s, q] = s_scr[s, q] * w_tot[s:s + 1, :] + jnp.where(same_head, dl, 0.0)

    def phase_c(c, q, st):
        ql = slice(q * LANES, (q + 1) * LANES)
        yv, rv, kv, vv = st["y"][:nv], st["r"][:nv], st["k"][:nv], st["v"][:nv]
        hs = _head_sum_bf16(jnp.concatenate([yv, rv * kv * rk_ref[:, ql]], axis=0), ones)
        dlt = yv - hs[:nv] * (1.0 / HEAD_DIM)
        bonus = hs[nv:] * vv
        yield
        var = _head_sum_bf16(dlt * dlt, ones) * (1.0 / HEAD_DIM)
        yn = dlt * lax.rsqrt(var + GN_EPS) * gnw_ref[:, ql] + gnb_ref[:, ql]
        y_ref[pl.ds(c * tchunk, tchunk), :, ql] = (yn + bonus).reshape(tchunk, nseq, LANES)

    def advance(gens):
        return [next(g, "done") is None for g in gens]

    def drain(gens):
        while any(advance(gens)):
            pass

    def chunk_group(cs):
        sts = [[{} for _ in range(npl)] for _ in cs]
        ga = [[phase_a(c, q, sts[i][q]) for q in range(npl)] for i, c in enumerate(cs)]
        gb = [[phase_b(c, q, sts[i][q]) for q in range(npl)] for i, c in enumerate(cs)]
        gc = [[phase_c(c, q, sts[i][q]) for q in range(npl)] for i, c in enumerate(cs)]
        drain(ga[0])
        for i in range(len(cs)):
            fill = []
            if i + 1 < len(cs):
                fill.append(ga[i + 1])
            if i > 0:
                fill.append(gc[i - 1])
            while any(advance(gb[i])):
                for f in fill:
                    advance(f)
            for f in fill:
                drain(f)
        drain(gc[-1])

    unroll = 2 if nchunks % 2 == 0 else 1

    def chunk(i, carry):
        chunk_group([i * unroll + e for e in range(unroll)])
        return carry

    lax.fori_loop(0, nchunks // unroll, chunk, 0)

    @pl.when(pl.program_id(2) == pl.num_programs(2) - 1)
    def _():
        out = sout_ref if has_alias else sout_ref.at[layer]
        if not has_alias:
            for other in range(sout_ref.shape[0]):
                if other != layer:
                    sout_ref[other] = jnp.zeros(sout_ref.shape[1:], F32)
        for s in range(nseq):
            for q in range(npl):
                out[s, 2 * q] = s_scr[s, q, :HEAD_DIM, :HEAD_DIM]
                out[s, 2 * q + 1] = s_scr[s, q, HEAD_DIM:, HEAD_DIM:]


def _wkv_call(r, lw, k, v, kk, b, states, new_states, layer, p, nb, nt):
    nslab, _, slab = r.shape
    npl = slab // LANES
    ngroup = nb // SUBLANES
    tchunk = min(WKV_CHUNK, nt)
    tb = min(WKV_BLOCK_ELEMS // (SUBLANES * slab), nt)
    nchunks = tb // tchunk
    assert nt % tb == 0 and tb % tchunk == 0 and nb % SUBLANES == 0
    view = lambda a: a.reshape(nslab, nt, ngroup, SUBLANES, slab)
    hps = 2 * npl
    batch_minor = nb == LANES
    if batch_minor:
        order = lambda f: (lambda p_, g, t: f(g, p_, t))
        grid = (nslab, ngroup, nt // tb)
        states_in = states.transpose(0, 2, 3, 4, 1)
        st_in_spec = pl.BlockSpec((None, hps, HEAD_DIM, HEAD_DIM, nb), lambda p_, g, t: (layer, p_, 0, 0, 0),
                                  pipeline_mode=pl.Buffered(1))
        relayout_scr = [pltpu.VMEM((npl, HEAD_DIM * RELAYOUT_PITCH, LANES), F32)]
    else:
        order = lambda f: f
        grid = (ngroup, nslab, nt // tb)
        states_in = states
        st_in_spec = pl.BlockSpec((None, SUBLANES, hps, HEAD_DIM, HEAD_DIM),
                                  lambda g, p_, t: (layer, g, p_, 0, 0))
        relayout_scr = []
    act_spec = pl.BlockSpec((None, tb, None, SUBLANES, slab), order(lambda g, p_, t: (p_, t, g, 0, 0)))
    has_alias = new_states is not None
    if has_alias:
        st_out_spec = pl.BlockSpec((None, SUBLANES, hps, HEAD_DIM, HEAD_DIM),
                                   order(lambda g, p_, t: (layer, g, p_, 0, 0)))
    else:
        st_out_spec = pl.BlockSpec((states.shape[0], SUBLANES, hps, HEAD_DIM, HEAD_DIM),
                                   order(lambda g, p_, t: (0, g, p_, 0, 0)))
    extra_in = [new_states] if has_alias else []
    extra_spec = [pl.BlockSpec(memory_space=pl.ANY)] if has_alias else []
    par_spec = pl.BlockSpec((None, 1, slab), order(lambda g, p_, t: (layer, 0, p_)))
    n = max(tchunk, 8) * SUBLANES
    y, s_out = pl.pallas_call(
        functools.partial(_wkv_kernel, tchunk, nchunks, has_alias, layer, batch_minor),
        grid=grid,
        in_specs=[act_spec] * 6 + [st_in_spec] + [par_spec] * 3 + extra_spec,
        out_specs=[act_spec, st_out_spec],
        out_shape=[jax.ShapeDtypeStruct((nslab, nt, ngroup, SUBLANES, slab), F32),
                   jax.ShapeDtypeStruct(states.shape, F32)],
        input_output_aliases={10: 1} if has_alias else {},
        scratch_shapes=[pltpu.VMEM((SUBLANES, npl, LANES, LANES), F32)] + relayout_scr
        + [pltpu.VMEM((n, LANES), F32)] * (len(_WKV_SLOTS) * npl),
        compiler_params=pltpu.CompilerParams(
            dimension_semantics=("arbitrary", "arbitrary", "arbitrary"),
            vmem_limit_bytes=VMEM_LIMIT),
        name="wkv7",
    )(view(r), view(lw), view(k), view(v), view(kk), view(b), states_in, p['r_k'], p['gn_w'], p['gn_b'],
      *extra_in)
    return y.reshape(nslab, nt * nb, slab), s_out


def _ffn(x, gffn, w1_ref, w2_ref):
    h = _rmsnorm(x, gffn).astype(BF16)
    dff = w1_ref.shape[1]
    step = min(dff, FFN_CHUNK)
    acc = x
    for c in range(dff // step):
        hh = _dot(h, w1_ref[:, c * step:(c + 1) * step])
        hh = jnp.square(jnp.maximum(hh, 0.0)).astype(BF16)
        acc = acc + _dot(hh, w2_ref[c * step:(c + 1) * step, :])
    return acc


def _attn_ffn_kernel(final_norm, nb, x_ref, y_ref, g_ref, wo_ref, gffn_ref, w1_ref, w2_ref, gfin_ref,
                     o_ref):
    nslab = y_ref.shape[0]
    y = jnp.concatenate([y_ref[q] for q in range(nslab)], axis=1)
    x = _rows_in(x_ref, nb) + _dot((y * g_ref[...]).astype(BF16), wo_ref[...])
    out = _ffn(x, gffn_ref[...], w1_ref, w2_ref)
    if final_norm:
        out = _rmsnorm(out, gfin_ref[...])
    _rows_out(o_ref, out, nb)


def _out_x(rows, d, tm, nb, batch_major_out):
    if batch_major_out:
        return (jax.ShapeDtypeStruct((nb, rows // nb, d), F32),
                pl.BlockSpec((nb, tm // nb, d), lambda i: (0, i, 0)))
    return jax.ShapeDtypeStruct((rows, d), F32), pl.BlockSpec((tm, d), lambda i: (i, 0))


def _attn_ffn_call(x, y, g, p, i, j, final_norm, batch_major_out, nb, tm):
    rows, d = g.shape
    nslab, _, slab = y.shape
    out_sds, out_spec = _out_x(rows, d, tm, nb, batch_major_out)
    ops = [(x, _x_spec(x, tm, nb)), (y, pl.BlockSpec((nslab, tm, slab), lambda i_: (0, i_, 0))),
           (g, pl.BlockSpec((tm, d), lambda i_: (i_, 0))), _pick(p['w_o'], j), _pick(p['norm_ffn'], i),
           _pick(p['ffn_w1'], i), _pick(p['ffn_w2'], i), _pick(p['norm_final'])]
    return pl.pallas_call(
        functools.partial(_attn_ffn_kernel, final_norm, nb),
        grid=(rows // tm,),
        in_specs=[o[1] for o in ops],
        out_specs=out_spec,
        out_shape=out_sds,
        compiler_params=pltpu.CompilerParams(
            dimension_semantics=("arbitrary",), vmem_limit_bytes=VMEM_LIMIT),
        name="attn_out_ffn",
    )(*[o[0] for o in ops])


def _pool_ffn_kernel(final_norm, nb, start_pos, x_ref, buf_ref, gmix_ref, pw_ref, ps_ref,
                     gffn_ref, w1_ref, w2_ref, gfin_ref, o_ref, buf_out, carry):
    d = gmix_ref.shape[-1]
    tm = x_ref.shape[0] if len(x_ref.shape) == 2 else nb * x_ref.shape[1]
    ngrp = len(POOL_WINDOWS)
    gw = d // ngrp

    @pl.when(pl.program_id(0) == 0)
    def _():
        carry[...] = buf_ref[...]

    x = _rows_in(x_ref, nb)
    h = _rmsnorm(x, gmix_ref[...])
    ext = jnp.concatenate([carry[...], h], axis=0)
    carry[...] = ext[tm:]
    buf_out[...] = ext[tm:]

    rowi = lax.broadcasted_iota(jnp.int32, (tm, gw), 0) + pl.program_id(0) * tm
    pos = start_pos + jnp.right_shift(rowi, int(math.log2(nb)))
    ys = []
    for gi, w in enumerate(POOL_WINDOWS):
        sl = slice(gi * gw, (gi + 1) * gw)
        s = ext[:, sl]
        span = 1
        while span < w:
            s = s[span * nb:] + s[: s.shape[0] - span * nb]
            span *= 2
        s = s[s.shape[0] - tm:]
        cnt = jnp.minimum(pos + 1, w).astype(F32)
        diff = (s / cnt - h[:, sl]).astype(BF16)
        ys.append(_dot(diff, pw_ref[gi]))
    x = x + jnp.concatenate(ys, axis=1) * ps_ref[...]
    out = _ffn(x, gffn_ref[...], w1_ref, w2_ref)
    if final_norm:
        out = _rmsnorm(out, gfin_ref[...])
    _rows_out(o_ref, out, nb)


def _pool_ffn_call(x, buf, p, i, j, final_norm, batch_major_out, nb, start_pos, tm):
    d = x.shape[-1]
    rows = x.shape[0] if x.ndim == 2 else x.shape[0] * x.shape[1]
    assert nb & (nb - 1) == 0
    out_sds, out_spec = _out_x(rows, d, tm, nb, batch_major_out)
    ops = [(x, _x_spec(x, tm, nb)), (buf, _const_spec(buf.shape)), _pick(p['norm_mix'], i),
           _pick(p['pool_w'], j), _pick(p['pool_scale'], j), _pick(p['norm_ffn'], i),
           _pick(p['ffn_w1'], i), _pick(p['ffn_w2'], i), _pick(p['norm_final'])]
    return pl.pallas_call(
        functools.partial(_pool_ffn_kernel, final_norm, nb, start_pos),
        grid=(rows // tm,),
        in_specs=[o[1] for o in ops],
        out_specs=[out_spec, _const_spec(buf.shape)],
        out_shape=[out_sds, jax.ShapeDtypeStruct(buf.shape, F32)],
        scratch_shapes=[pltpu.VMEM(buf.shape, F32)],
        compiler_params=pltpu.CompilerParams(
            dimension_semantics=("arbitrary",), vmem_limit_bytes=VMEM_LIMIT),
        name="pool_ffn",
    )(*[o[0] for o in ops])


def _trunk(x, start_pos, shift_states, wkv_states, pool_bufs, p):
    nb, nt, d = x.shape
    depth = p['norm_mix'].shape[0]
    rows = nb * nt
    tm = min(ROW_BLOCK, rows)
    assert rows % tm == 0 and tm % nb == 0
    batch_major = nb == SUBLANES
    xt = x if batch_major else x.transpose(1, 0, 2).reshape(rows, d)
    new_shift, new_wkv, new_pool = [], None, []
    v_first = None
    for i in range(depth):
        j = i // 2
        last = i == depth - 1
        bm_out = batch_major and last
        if i % 2 == 0:
            r, lw, k, v, kk, b, g, s_shift = _pre_call(xt, p, i, j, shift_states, v_first, nb, tm)
            if j == 0:
                v_first = v
            y, new_wkv = _wkv_call(r, lw, k, v, kk, b, wkv_states, new_wkv, j, p, nb, nt)
            new_shift.append(s_shift)
            xt = _attn_ffn_call(xt, y, g, p, i, j, last, bm_out, nb, tm)
        else:
            buf = pool_bufs[j].transpose(1, 0, 2).reshape(POOL_BUF * nb, d)
            xt, s_buf = _pool_ffn_call(xt, buf, p, i, j, last, bm_out, nb, start_pos, tm)
            new_pool.append(s_buf.reshape(POOL_BUF, nb, d).transpose(1, 0, 2))
    y = xt if batch_major else xt.reshape(nt, nb, d).transpose(1, 0, 2)
    return y, new_wkv, jnp.stack(new_shift), jnp.stack(new_pool)


def kernel(x_prompt, x_sample, state_wkv, state_shift, state_pool, norm_mix, norm_ffn, norm_final,
           rwkv_mix, rwkv_w_rkv, rwkv_w_o, rwkv_w0, rwkv_w1, rwkv_w2, rwkv_a0, rwkv_a1, rwkv_a2,
           rwkv_v0, rwkv_v1, rwkv_v2, rwkv_g1, rwkv_g2, rwkv_k_k, rwkv_k_a, rwkv_r_k,
           rwkv_gn_w, rwkv_gn_b, pool_w, pool_scale, ffn_w1, ffn_w2):
    bf = lambda a: a.astype(BF16)
    vec = lambda a: a.reshape(a.shape[0], 1, -1)
    p = {'norm_mix': vec(norm_mix), 'norm_ffn': vec(norm_ffn), 'norm_final': norm_final.reshape(1, -1),
         'mix': rwkv_mix, 'w_rkv': bf(rwkv_w_rkv), 'w_o': bf(rwkv_w_o), 'w0': vec(rwkv_w0),
         'w1': bf(rwkv_w1), 'w2': bf(rwkv_w2), 'a0': vec(rwkv_a0), 'a1': bf(rwkv_a1),
         'a2': bf(rwkv_a2), 'v0': vec(rwkv_v0), 'v1': bf(rwkv_v1), 'v2': bf(rwkv_v2),
         'g1': bf(rwkv_g1), 'g2': bf(rwkv_g2), 'k_k': vec(rwkv_k_k), 'k_a': vec(rwkv_k_a),
         'r_k': vec(rwkv_r_k), 'gn_w': vec(rwkv_gn_w), 'gn_b': vec(rwkv_gn_b), 'pool_w': bf(pool_w),
         'pool_scale': vec(pool_scale), 'ffn_w1': bf(ffn_w1), 'ffn_w2': bf(ffn_w2)}
    dt = x_prompt.dtype
    nb, _, d = x_prompt.shape
    n_rwkv = state_wkv.shape[0]
    n_pool = state_pool.shape[0]
    nh = d // HEAD_DIM
    z_shift = jnp.zeros((n_rwkv, nb, d), dt)
    z_wkv = jnp.zeros((n_rwkv, nb, nh, HEAD_DIM, HEAD_DIM), dt)
    z_pool = jnp.zeros((n_pool, nb, POOL_BUF, d), dt)
    y_p, wkv_p, shift_p, pool_p = _trunk(x_prompt, 0, z_shift, z_wkv, z_pool, p)
    y_s, wkv_s, shift_s, pool_s = _trunk(x_sample, PAST_LEN, state_shift, state_wkv, state_pool, p)
    return (y_p, y_s, wkv_p, shift_p, pool_p, wkv_s, shift_s, pool_s)
```

```python
import functools
import math

import jax
import jax.numpy as jnp
from jax import lax
from jax.experimental import pallas as pl
from jax.experimental.pallas import tpu as pltpu

HEAD_DIM = 64
LANES = 128
SUBLANES = 8
WKV_LANES = 512
WKV_BLOCK_ELEMS = 8 * 128 * 512
POOL_WINDOWS = (2, 4, 8, 16)
POOL_BUF = max(POOL_WINDOWS) - 1
PAST_LEN = 16384
NORM_EPS = 1e-6
GN_EPS = 64e-5
L2_EPS = 1e-12
ROW_BLOCK = 512
FFN_CHUNK = 1024
WKV_CHUNK = 16
RELAYOUT_PITCH = 136
VMEM_LIMIT = 56 * 1024 * 1024

BF16 = jnp.bfloat16
F32 = jnp.float32


def _dot(a, b):
    return jnp.dot(a, b, preferred_element_type=F32)


def _dot_nt(a, b):
    return lax.dot_general(a, b, (((1,), (1,)), ((), ())), preferred_element_type=F32)


def _sigmoid(x):
    return 0.5 * jnp.tanh(0.5 * x) + 0.5


def _rmsnorm(x, g):
    ms = jnp.mean(x * x, axis=-1, keepdims=True)
    return x * lax.rsqrt(ms + NORM_EPS) * g


def _head_ones():
    r = lax.broadcasted_iota(jnp.int32, (LANES, LANES), 0) // HEAD_DIM
    c = lax.broadcasted_iota(jnp.int32, (LANES, LANES), 1) // HEAD_DIM
    return jnp.where(r == c, 1.0, 0.0).astype(BF16)


def _head_sum_bf16(p, ones):
    return _dot(p.astype(BF16), ones)


def _const_spec(shape):
    n = len(shape)
    return pl.BlockSpec(shape, lambda *_: (0,) * n)


def _pick(arr, *idx):
    shape = (None,) * len(idx) + arr.shape[len(idx):]
    at = tuple(idx) + (0,) * (arr.ndim - len(idx))
    return arr, pl.BlockSpec(shape, lambda *_: at)


def _rows_in(x_ref, nb):
    if len(x_ref.shape) == 2:
        return x_ref[...]
    return jnp.swapaxes(x_ref[...], 0, 1).reshape(nb * x_ref.shape[1], x_ref.shape[2])


def _rows_out(o_ref, val, nb):
    if len(o_ref.shape) == 2:
        o_ref[...] = val
    else:
        o_ref[...] = jnp.swapaxes(val.reshape(o_ref.shape[1], nb, val.shape[1]), 0, 1)


def _x_spec(x, tm, nb):
    if x.ndim == 2:
        return pl.BlockSpec((tm, x.shape[1]), lambda i: (i, 0))
    return pl.BlockSpec((nb, tm // nb, x.shape[2]), lambda i: (0, i, 0))


def _slab_lanes(d):
    return WKV_LANES if d % WKV_LANES == 0 else LANES


def _pre_kernel(has_vres, nb, *refs):
    it = iter(refs)
    x_ref = next(it); shift_ref = next(it); gmix_ref = next(it); mix_ref = next(it)
    wr_ref = next(it); wk_ref = next(it); wv_ref = next(it)
    w0_ref = next(it); w1_ref = next(it); w2_ref = next(it)
    a0_ref = next(it); a1_ref = next(it); a2_ref = next(it)
    if has_vres:
        v0_ref = next(it); v1_ref = next(it); v2_ref = next(it); vfirst_ref = next(it)
    g1_ref = next(it); g2_ref = next(it); kk_ref = next(it); ka_ref = next(it)
    r_out = next(it); lw_out = next(it); k_out = next(it); v_out = next(it)
    kk_out = next(it); b_out = next(it); g_out = next(it); shift_out = next(it)
    carry = next(it)

    tm, d = g_out.shape
    nslab, _, slab = r_out.shape

    @pl.when(pl.program_id(0) == 0)
    def _():
        carry[...] = shift_ref[...]

    h = _rmsnorm(_rows_in(x_ref, nb), gmix_ref[...])
    if tm > nb:
        hp = jnp.concatenate([carry[...], h[: tm - nb]], axis=0)
    else:
        hp = carry[...]
    carry[...] = h[tm - nb:]
    shift_out[...] = h[tm - nb:]
    dx = hp - h

    def mixed(i):
        return (h + dx * mix_ref[i:i + 1, :]).astype(BF16)

    xv = mixed(2)
    lora_w = _dot(mixed(3), w1_ref[...])
    lora_a = _dot(mixed(4), a1_ref[...])
    if has_vres:
        lora_v = _dot(xv, v1_ref[...])
    lora_g = _dot(mixed(5), g1_ref[...])
    k = _dot(mixed(1), wk_ref[...])
    wpre = w0_ref[...] + _dot(jnp.tanh(lora_w).astype(BF16), w2_ref[...])
    a = _sigmoid(a0_ref[...] + _dot(lora_a.astype(BF16), a2_ref[...]))
    if has_vres:
        gate = _sigmoid(v0_ref[...] + _dot(lora_v.astype(BF16), v2_ref[...]))
    g = _dot(_sigmoid(lora_g).astype(BF16), g2_ref[...])
    g_out[...] = g
    lw = -math.exp(-0.5) * _sigmoid(wpre)

    ones = _head_ones()
    kk = k * kk_ref[...]
    k2 = k * (1.0 + (a - 1.0) * ka_ref[...])
    kkn = []
    for p in range(d // LANES):
        kkp = kk[:, p * LANES:(p + 1) * LANES]
        kkn.append(kkp * lax.rsqrt(jnp.maximum(_head_sum_bf16(kkp * kkp, ones), L2_EPS * L2_EPS)))
    kkn = jnp.concatenate(kkn, axis=1)
    bb = kkn * a
    v = _dot(xv, wv_ref[...])
    if has_vres:
        vfirst = jnp.concatenate([vfirst_ref[p] for p in range(nslab)], axis=1).astype(F32)
        v = v + (vfirst - v) * gate
    r = _dot(mixed(0), wr_ref[...])
    for p in range(nslab):
        sl = slice(p * slab, (p + 1) * slab)
        r_out[p] = r[:, sl].astype(BF16)
        lw_out[p] = lw[:, sl]
        k_out[p] = k2[:, sl].astype(BF16)
        v_out[p] = v[:, sl].astype(BF16)
        kk_out[p] = kkn[:, sl].astype(BF16)
        b_out[p] = bb[:, sl].astype(BF16)


def _pre_call(x, p, i, j, shift_states, vfirst, nb, tm):
    d = x.shape[-1]
    rows = x.shape[0] if x.ndim == 2 else x.shape[0] * x.shape[1]
    slab = _slab_lanes(d)
    nslab = d // slab
    has_vres = j > 0
    sm_spec = pl.BlockSpec((nslab, tm, slab), lambda i_: (0, i_, 0))
    ops = [(x, _x_spec(x, tm, nb)), _pick(shift_states, j), _pick(p['norm_mix'], i), _pick(p['mix'], j),
           _pick(p['w_rkv'], j, 0), _pick(p['w_rkv'], j, 1), _pick(p['w_rkv'], j, 2),
           _pick(p['w0'], j), _pick(p['w1'], j), _pick(p['w2'], j),
           _pick(p['a0'], j), _pick(p['a1'], j), _pick(p['a2'], j)]
    if has_vres:
        ops += [_pick(p['v0'], j - 1), _pick(p['v1'], j - 1), _pick(p['v2'], j - 1), (vfirst, sm_spec)]
    ops += [_pick(p['g1'], j), _pick(p['g2'], j), _pick(p['k_k'], j), _pick(p['k_a'], j)]
    sm = jax.ShapeDtypeStruct((nslab, rows, slab), F32)
    sm16 = jax.ShapeDtypeStruct((nslab, rows, slab), BF16)
    out_shape = [sm16, sm, sm16, sm16, sm16, sm16] + [jax.ShapeDtypeStruct((rows, d), F32), jax.ShapeDtypeStruct((nb, d), F32)]
    out_specs = [sm_spec] * 6 + [pl.BlockSpec((tm, d), lambda i_: (i_, 0)), _const_spec((nb, d))]
    return pl.pallas_call(
        functools.partial(_pre_kernel, has_vres, nb),
        grid=(rows // tm,),
        in_specs=[o[1] for o in ops],
        out_specs=out_specs,
        out_shape=out_shape,
        scratch_shapes=[pltpu.VMEM((nb, d), F32)],
        compiler_params=pltpu.CompilerParams(
            dimension_semantics=("arbitrary",), vmem_limit_bytes=VMEM_LIMIT),
        name="rwkv_pre",
    )(*[o[0] for o in ops])


_WKV_SLOTS = ("at", "rt", "u", "v", "bh", "kh", "x0", "y0")


def _wkv_kernel(tchunk, nchunks, has_alias, layer, batch_minor, r_ref, lw_ref, k_ref, v_ref, kk_ref, b_ref,
                s0_ref, rk_ref, gnw_ref, gnb_ref, *rest):
    y_ref, sout_ref, s_scr, *work = rest[1:] if has_alias else rest
    if batch_minor:
        t_scr, *work = work
    nseq = SUBLANES
    npl = s_scr.shape[1]
    cp = max(tchunk, 8)
    n = cp * nseq
    nv = tchunk * nseq
    grp = LANES // (2 * cp)
    nslot = len(_WKV_SLOTS)
    scr = [dict(zip(_WKV_SLOTS, work[q * nslot:(q + 1) * nslot])) for q in range(npl)]

    if batch_minor:
        group = pl.program_id(1)

        @pl.when((group == 0) & (pl.program_id(2) == 0))
        def _():
            def relayout(i, carry):
                for q in range(npl):
                    for e in range(SUBLANES):
                        v = i * SUBLANES + e
                        tile = jnp.concatenate([s0_ref[2 * q, v], s0_ref[2 * q + 1, v]], axis=0)
                        t_scr[q, pl.ds(pl.multiple_of(v * RELAYOUT_PITCH, SUBLANES), LANES), :] = tile.T
                return carry
            lax.fori_loop(0, HEAD_DIM // SUBLANES, relayout, 0)

        @pl.when(pl.program_id(2) == 0)
        def _():
            lane = lax.broadcasted_iota(jnp.int32, (HEAD_DIM, LANES), 1)
            for s in range(nseq):
                for q in range(npl):
                    side = t_scr[q, pl.ds(group * nseq + s, HEAD_DIM, stride=RELAYOUT_PITCH), :]
                    s_scr[s, q] = jnp.concatenate([jnp.where(lane < HEAD_DIM, side, 0.0),
                                                   jnp.where(lane >= HEAD_DIM, side, 0.0)], axis=0)
    else:
        @pl.when(pl.program_id(2) == 0)
        def _():
            zero = jnp.zeros((HEAD_DIM, HEAD_DIM), F32)
            for s in range(nseq):
                for q in range(npl):
                    top = jnp.concatenate([s0_ref[s, 2 * q], zero], axis=1)
                    bot = jnp.concatenate([zero, s0_ref[s, 2 * q + 1]], axis=1)
                    s_scr[s, q] = jnp.concatenate([top, bot], axis=0)

    ones = _head_ones()
    hr = lax.broadcasted_iota(jnp.int32, (LANES, LANES), 0) // HEAD_DIM
    hc = lax.broadcasted_iota(jnp.int32, (LANES, LANES), 1) // HEAD_DIM
    same_head = hr == hc
    lane_head = lax.broadcasted_iota(jnp.int32, (1, LANES), 1) // HEAD_DIM
    head_f32 = [jnp.where(lane_head == hh, 1.0, 0.0) for hh in range(2)]
    head_bf16 = [m.astype(BF16) for m in head_f32]
    ri = lax.broadcasted_iota(jnp.int32, (n, 2 * n), 0)
    ci = lax.broadcasted_iota(jnp.int32, (n, 2 * n), 1) % n
    same_seq = (ri % nseq) == (ci % nseq)
    tri_incl = jnp.where(same_seq & (ci // nseq <= ri // nseq), 1.0, 0.0)
    tri_strict = jnp.where(same_seq & (ci // nseq < ri // nseq), 1.0, 0.0)[:, :n]
    half = cp // 2 if cp >= 16 else cp
    nh = half * nseq
    if half < cp:
        rh = lax.broadcasted_iota(jnp.int32, (n, nh), 0) % nseq
        ch = lax.broadcasted_iota(jnp.int32, (n, nh), 1) % nseq
        same_seq_half = jnp.where(rh == ch, 1.0, 0.0).astype(BF16)

    def load(ref, c, q):
        val = ref[pl.ds(c * tchunk, tchunk), :, q * LANES:(q + 1) * LANES].astype(F32).reshape(nv, LANES)
        if n > nv:
            val = jnp.concatenate([val, jnp.zeros((n - nv, LANES), F32)], axis=0)
        return val

    def tile_rows(slab, reps):
        return slab if reps == 1 else jnp.concatenate([slab] * reps, axis=0)

    def seq_rows(ref, s):
        return ref[pl.ds(s, cp, stride=nseq), :]

    def phase_a(c, q, st):
        r = load(r_ref, c, q); lw = load(lw_ref, c, q); k = load(k_ref, c, q)
        v = load(v_ref, c, q); kk = load(kk_ref, c, q); b = load(b_ref, c, q)

        acc = jnp.zeros((nseq, LANES), F32)
        cums = []
        for t in range(cp):
            acc = acc + lw[t * nseq:(t + 1) * nseq]
            cums.append(acc)
        cum = jnp.concatenate(cums, axis=0)
        w_inc = jnp.exp(cum)
        w_inv = jnp.exp(-cum)
        w_exc = jnp.concatenate([jnp.ones((nseq, LANES), F32), w_inc[: n - nseq]], axis=0)
        w_tot = w_inc[n - nseq:]
        at = -kk * w_exc
        rt = r * w_inc
        kt = k * w_inv
        bt = b * w_inv
        w_tot_rows = tile_rows(w_tot, cp)
        st.update(r=r, k=k, v=v, at=at, rt=rt, w_tot=w_tot, bh=bt * w_tot_rows, kh=kt * w_tot_rows)

        at16 = at.astype(BF16)
        rt16 = rt.astype(BF16)
        kt16 = kt.astype(BF16)
        kb16 = jnp.concatenate([kt16, bt.astype(BF16)], axis=0)
        gram_a = _dot_nt(jnp.concatenate([at16 * head_bf16[0], at16 * head_bf16[1]], axis=0), kt16)
        gram_r = _dot_nt(jnp.concatenate([rt16 * head_bf16[0], rt16 * head_bf16[1]], axis=0), kb16)
        st["ga"] = jnp.concatenate([gram_a[:n] * tri_strict, gram_a[n:] * tri_strict], axis=1).astype(BF16)
        st["gr"] = jnp.concatenate([gram_r[:n] * tri_incl, gram_r[n:] * tri_incl], axis=1).astype(BF16)
        st["vm"] = jnp.concatenate([v * head_f32[0], v * head_f32[1]], axis=0)
        yield

        prods = []
        for t in range(1, cp):
            j0 = (t // half) * half
            if t > j0:
                prods.append(tile_rows(at[t * nseq:(t + 1) * nseq], t - j0) * bt[j0 * nseq:t * nseq])
        st["coef"] = _head_sum_bf16(jnp.concatenate(prods, axis=0), ones)
        if half < cp:
            lhs = jnp.concatenate([at16[nh:] * head_bf16[0], at16[nh:] * head_bf16[1]], axis=0)
            n21 = _dot_nt(lhs, bt.astype(BF16)[:nh]).astype(BF16) * same_seq_half
            st["n21"] = jnp.concatenate([n21[:nh], n21[nh:]], axis=1)

    def phase_b(c, q, st):
        w = scr[q]
        w["at"][...] = st["at"]
        w["rt"][...] = st["rt"]
        w["v"][...] = st["v"]
        w["bh"][...] = st["bh"]
        w["kh"][...] = st["kh"]
        for s in range(nseq):
            lhs = jnp.concatenate([seq_rows(w["at"], s), seq_rows(w["rt"], s)], axis=0).astype(BF16)
            out = _dot_nt(lhs, s_scr[s, q].astype(BF16))
            w["x0"][pl.ds(s, cp, stride=nseq), :] = out[:cp]
            w["y0"][pl.ds(s, cp, stride=nseq), :] = out[cp:]
        yield

        x = w["x0"][...] + _dot(st["ga"], st["vm"].astype(BF16))
        yield

        coef = st["coef"]
        us = []
        off = 0
        for t in range(cp):
            j0 = (t // half) * half
            if t == half:
                u1 = jnp.concatenate(us, axis=0).astype(BF16)
                x = jnp.concatenate([x[:nh], x[nh:] + _dot(
                    st["n21"], jnp.concatenate([u1 * head_bf16[0], u1 * head_bf16[1]], axis=0))], axis=0)
            u_t = x[t * nseq:(t + 1) * nseq]
            for j in range(j0, t):
                u_t = u_t + coef[off + (j - j0) * nseq:off + (j - j0 + 1) * nseq] * us[j]
            off += (t - j0) * nseq
            us.append(u_t)
        u = jnp.concatenate(us, axis=0)
        w["u"][...] = u
        yield

        vm = st["vm"]
        um = jnp.concatenate([u * head_f32[0], u * head_f32[1]], axis=0)
        vum = jnp.concatenate([vm[:n], um[:n], vm[n:], um[n:]], axis=0)
        st["y"] = w["y0"][...] + _dot(st["gr"], vum.astype(BF16))
        yield

        w_tot = st["w_tot"]
        zp = []
        for s in range(nseq):
            zp += [seq_rows(w["u"], s), seq_rows(w["v"], s)]
        z_t = jnp.concatenate(zp, axis=0).T.astype(BF16)
        for g0 in range(0, nseq, grp):
            cols = []
            for e in range(grp):
                gs = jnp.concatenate([seq_rows(w["bh"], g0 + e), seq_rows(w["kh"], g0 + e)],
                                     axis=0).astype(BF16)
                blk = [gs]
                if e > 0:
                    blk = [jnp.zeros((2 * cp * e, LANES), BF16)] + blk
                if e < grp - 1:
                    blk = blk + [jnp.zeros((2 * cp * (grp - 1 - e), LANES), BF16)]
                cols.append(jnp.concatenate(blk, axis=0))
            lhs = z_t[:, (g0 // grp) * LANES:(g0 // grp + 1) * LANES]
            delta = _dot(lhs, jnp.concatenate(cols, axis=1))
            for e in range(grp):
                s = g0 + e
                dl = delta[:, e * LANES:(e + 1) * LANES]
                s_scr[s, q] = s_scr[s, q] * w_tot[s:s + 1, :] + jnp.where(same_head, dl, 0.0)

    def phase_c(c, q, st):
        ql = slice(q * LANES, (q + 1) * LANES)
        yv, rv, kv, vv = st["y"][:nv], st["r"][:nv], st["k"][:nv], st["v"][:nv]
        hs = _head_sum_bf16(jnp.concatenate([yv, rv * kv * rk_ref[:, ql]], axis=0), ones)
        dlt = yv - hs[:nv] * (1.0 / HEAD_DIM)
        bonus = hs[nv:] * vv
        yield
        var = _head_sum_bf16(dlt * dlt, ones) * (1.0 / HEAD_DIM)
        yn = dlt * lax.rsqrt(var + GN_EPS) * gnw_ref[:, ql] + gnb_ref[:, ql]
        y_ref[pl.ds(c * tchunk, tchunk), :, ql] = (yn + bonus).reshape(tchunk, nseq, LANES)

    def advance(gens):
        return [next(g, "done") is None for g in gens]

    def drain(gens):
        while any(advance(gens)):
            pass

    def chunk_group(cs):
        sts = [[{} for _ in range(npl)] for _ in cs]
        ga = [[phase_a(c, q, sts[i][q]) for q in range(npl)] for i, c in enumerate(cs)]
        gb = [[phase_b(c, q, sts[i][q]) for q in range(npl)] for i, c in enumerate(cs)]
        gc = [[phase_c(c, q, sts[i][q]) for q in range(npl)] for i, c in enumerate(cs)]
        drain(ga[0])
        for i in range(len(cs)):
            fill = []
            if i + 1 < len(cs):
                fill.append(ga[i + 1])
            if i > 0:
                fill.append(gc[i - 1])
            while any(advance(gb[i])):
                for f in fill:
                    advance(f)
            for f in fill:
                drain(f)
        drain(gc[-1])

    unroll = 2 if nchunks % 2 == 0 else 1

    def chunk(i, carry):
        chunk_group([i * unroll + e for e in range(unroll)])
        return carry

    lax.fori_loop(0, nchunks // unroll, chunk, 0)

    @pl.when(pl.program_id(2) == pl.num_programs(2) - 1)
    def _():
        out = sout_ref if has_alias else sout_ref.at[layer]
        if not has_alias:
            for other in range(sout_ref.shape[0]):
                if other != layer:
                    sout_ref[other] = jnp.zeros(sout_ref.shape[1:], F32)
        for s in range(nseq):
            for q in range(npl):
                out[s, 2 * q] = s_scr[s, q, :HEAD_DIM, :HEAD_DIM]
                out[s, 2 * q + 1] = s_scr[s, q, HEAD_DIM:, HEAD_DIM:]


def _wkv_call(r, lw, k, v, kk, b, states, new_states, layer, p, nb, nt):
    nslab, _, slab = r.shape
    npl = slab // LANES
    ngroup = nb // SUBLANES
    tchunk = min(WKV_CHUNK, nt)
    tb = min(WKV_BLOCK_ELEMS // (SUBLANES * slab), nt)
    nchunks = tb // tchunk
    assert nt % tb == 0 and tb % tchunk == 0 and nb % SUBLANES == 0
    view = lambda a: a.reshape(nslab, nt, ngroup, SUBLANES, slab)
    hps = 2 * npl
    batch_minor = nb == LANES
    if batch_minor:
        order = lambda f: (lambda p_, g, t: f(g, p_, t))
        grid = (nslab, ngroup, nt // tb)
        states_in = states.transpose(0, 2, 3, 4, 1)
        st_in_spec = pl.BlockSpec((None, hps, HEAD_DIM, HEAD_DIM, nb), lambda p_, g, t: (layer, p_, 0, 0, 0),
                                  pipeline_mode=pl.Buffered(1))
        relayout_scr = [pltpu.VMEM((npl, HEAD_DIM * RELAYOUT_PITCH, LANES), F32)]
    else:
        order = lambda f: f
        grid = (ngroup, nslab, nt // tb)
        states_in = states
        st_in_spec = pl.BlockSpec((None, SUBLANES, hps, HEAD_DIM, HEAD_DIM),
                                  lambda g, p_, t: (layer, g, p_, 0, 0))
        relayout_scr = []
    act_spec = pl.BlockSpec((None, tb, None, SUBLANES, slab), order(lambda g, p_, t: (p_, t, g, 0, 0)))
    has_alias = new_states is not None
    if has_alias:
        st_out_spec = pl.BlockSpec((None, SUBLANES, hps, HEAD_DIM, HEAD_DIM),
                                   order(lambda g, p_, t: (layer, g, p_, 0, 0)))
    else:
        st_out_spec = pl.BlockSpec((states.shape[0], SUBLANES, hps, HEAD_DIM, HEAD_DIM),
                                   order(lambda g, p_, t: (0, g, p_, 0, 0)))
    extra_in = [new_states] if has_alias else []
    extra_spec = [pl.BlockSpec(memory_space=pl.ANY)] if has_alias else []
    par_spec = pl.BlockSpec((None, 1, slab), order(lambda g, p_, t: (layer, 0, p_)))
    n = max(tchunk, 8) * SUBLANES
    y, s_out = pl.pallas_call(
        functools.partial(_wkv_kernel, tchunk, nchunks, has_alias, layer, batch_minor),
        grid=grid,
        in_specs=[act_spec] * 6 + [st_in_spec] + [par_spec] * 3 + extra_spec,
        out_specs=[act_spec, st_out_spec],
        out_shape=[jax.ShapeDtypeStruct((nslab, nt, ngroup, SUBLANES, slab), F32),
                   jax.ShapeDtypeStruct(states.shape, F32)],
        input_output_aliases={10: 1} if has_alias else {},
        scratch_shapes=[pltpu.VMEM((SUBLANES, npl, LANES, LANES), F32)] + relayout_scr
        + [pltpu.VMEM((n, LANES), F32)] * (len(_WKV_SLOTS) * npl),
        compiler_params=pltpu.CompilerParams(
            dimension_semantics=("arbitrary", "arbitrary", "arbitrary"),
            vmem_limit_bytes=VMEM_LIMIT),
        name="wkv7",
    )(view(r), view(lw), view(k), view(v), view(kk), view(b), states_in, p['r_k'], p['gn_w'], p['gn_b'],
      *extra_in)
    return y.reshape(nslab, nt * nb, slab), s_out


def _ffn(x, gffn, w1_ref, w2_ref):
    h = _rmsnorm(x, gffn).astype(BF16)
    dff = w1_ref.shape[1]
    step = min(dff, FFN_CHUNK)
    acc = x
    for c in range(dff // step):
        hh = _dot(h, w1_ref[:, c * step:(c + 1) * step])
        hh = jnp.square(jnp.maximum(hh, 0.0)).astype(BF16)
        acc = acc + _dot(hh, w2_ref[c * step:(c + 1) * step, :])
    return acc


def _attn_ffn_kernel(final_norm, nb, x_ref, y_ref, g_ref, wo_ref, gffn_ref, w1_ref, w2_ref, gfin_ref,
                     o_ref):
    nslab = y_ref.shape[0]
    y = jnp.concatenate([y_ref[q] for q in range(nslab)], axis=1)
    x = _rows_in(x_ref, nb) + _dot((y * g_ref[...]).astype(BF16), wo_ref[...])
    out = _ffn(x, gffn_ref[...], w1_ref, w2_ref)
    if final_norm:
        out = _rmsnorm(out, gfin_ref[...])
    _rows_out(o_ref, out, nb)


def _out_x(rows, d, tm, nb, batch_major_out):
    if batch_major_out:
        return (jax.ShapeDtypeStruct((nb, rows // nb, d), F32),
                pl.BlockSpec((nb, tm // nb, d), lambda i: (0, i, 0)))
    return jax.ShapeDtypeStruct((rows, d), F32), pl.BlockSpec((tm, d), lambda i: (i, 0))


def _attn_ffn_call(x, y, g, p, i, j, final_norm, batch_major_out, nb, tm):
    rows, d = g.shape
    nslab, _, slab = y.shape
    out_sds, out_spec = _out_x(rows, d, tm, nb, batch_major_out)
    ops = [(x, _x_spec(x, tm, nb)), (y, pl.BlockSpec((nslab, tm, slab), lambda i_: (0, i_, 0))),
           (g, pl.BlockSpec((tm, d), lambda i_: (i_, 0))), _pick(p['w_o'], j), _pick(p['norm_ffn'], i),
           _pick(p['ffn_w1'], i), _pick(p['ffn_w2'], i), _pick(p['norm_final'])]
    return pl.pallas_call(
        functools.partial(_attn_ffn_kernel, final_norm, nb),
        grid=(rows // tm,),
        in_specs=[o[1] for o in ops],
        out_specs=out_spec,
        out_shape=out_sds,
        compiler_params=pltpu.CompilerParams(
            dimension_semantics=("arbitrary",), vmem_limit_bytes=VMEM_LIMIT),
        name="attn_out_ffn",
    )(*[o[0] for o in ops])


def _pool_ffn_kernel(final_norm, nb, start_pos, x_ref, buf_ref, gmix_ref, pw_ref, ps_ref,
                     gffn_ref, w1_ref, w2_ref, gfin_ref, o_ref, buf_out, carry):
    d = gmix_ref.shape[-1]
    tm = x_ref.shape[0] if len(x_ref.shape) == 2 else nb * x_ref.shape[1]
    ngrp = len(POOL_WINDOWS)
    gw = d // ngrp

    @pl.when(pl.program_id(0) == 0)
    def _():
        carry[...] = buf_ref[...]

    x = _rows_in(x_ref, nb)
    h = _rmsnorm(x, gmix_ref[...])
    ext = jnp.concatenate([carry[...], h], axis=0)
    carry[...] = ext[tm:]
    buf_out[...] = ext[tm:]

    rowi = lax.broadcasted_iota(jnp.int32, (tm, gw), 0) + pl.program_id(0) * tm
    pos = start_pos + jnp.right_shift(rowi, int(math.log2(nb)))
    ys = []
    for gi, w in enumerate(POOL_WINDOWS):
        sl = slice(gi * gw, (gi + 1) * gw)
        s = ext[:, sl]
        span = 1
        while span < w:
            s = s[span * nb:] + s[: s.shape[0] - span * nb]
            span *= 2
        s = s[s.shape[0] - tm:]
        cnt = jnp.minimum(pos + 1, w).astype(F32)
        diff = (s / cnt - h[:, sl]).astype(BF16)
        ys.append(_dot(diff, pw_ref[gi]))
    x = x + jnp.concatenate(ys, axis=1) * ps_ref[...]
    out = _ffn(x, gffn_ref[...], w1_ref, w2_ref)
    if final_norm:
        out = _rmsnorm(out, gfin_ref[...])
    _rows_out(o_ref, out, nb)


def _pool_ffn_call(x, buf, p, i, j, final_norm, batch_major_out, nb, start_pos, tm):
    d = x.shape[-1]
    rows = x.shape[0] if x.ndim == 2 else x.shape[0] * x.shape[1]
    assert nb & (nb - 1) == 0
    out_sds, out_spec = _out_x(rows, d, tm, nb, batch_major_out)
    ops = [(x, _x_spec(x, tm, nb)), (buf, _const_spec(buf.shape)), _pick(p['norm_mix'], i),
           _pick(p['pool_w'], j), _pick(p['pool_scale'], j), _pick(p['norm_ffn'], i),
           _pick(p['ffn_w1'], i), _pick(p['ffn_w2'], i), _pick(p['norm_final'])]
    return pl.pallas_call(
        functools.partial(_pool_ffn_kernel, final_norm, nb, start_pos),
        grid=(rows // tm,),
        in_specs=[o[1] for o in ops],
        out_specs=[out_spec, _const_spec(buf.shape)],
        out_shape=[out_sds, jax.ShapeDtypeStruct(buf.shape, F32)],
        scratch_shapes=[pltpu.VMEM(buf.shape, F32)],
        compiler_params=pltpu.CompilerParams(
            dimension_semantics=("arbitrary",), vmem_limit_bytes=VMEM_LIMIT),
        name="pool_ffn",
    )(*[o[0] for o in ops])


def _trunk(x, start_pos, shift_states, wkv_states, pool_bufs, p):
    nb, nt, d = x.shape
    depth = p['norm_mix'].shape[0]
    rows = nb * nt
    tm = min(ROW_BLOCK, rows)
    assert rows % tm == 0 and tm % nb == 0
    batch_major = nb == SUBLANES
    xt = x if batch_major else x.transpose(1, 0, 2).reshape(rows, d)
    new_shift, new_wkv, new_pool = [], None, []
    v_first = None
    for i in range(depth):
        j = i // 2
        last = i == depth - 1
        bm_out = batch_major and last
        if i % 2 == 0:
            r, lw, k, v, kk, b, g, s_shift = _pre_call(xt, p, i, j, shift_states, v_first, nb, tm)
            if j == 0:
                v_first = v
            y, new_wkv = _wkv_call(r, lw, k, v, kk, b, wkv_states, new_wkv, j, p, nb, nt)
            new_shift.append(s_shift)
            xt = _attn_ffn_call(xt, y, g, p, i, j, last, bm_out, nb, tm)
        else:
            buf = pool_bufs[j].transpose(1, 0, 2).reshape(POOL_BUF * nb, d)
            xt, s_buf = _pool_ffn_call(xt, buf, p, i, j, last, bm_out, nb, start_pos, tm)
            new_pool.append(s_buf.reshape(POOL_BUF, nb, d).transpose(1, 0, 2))
    y = xt if batch_major else xt.reshape(nt, nb, d).transpose(1, 0, 2)
    return y, new_wkv, jnp.stack(new_shift), jnp.stack(new_pool)


def kernel(x_prompt, x_sample, state_wkv, state_shift, state_pool, norm_mix, norm_ffn, norm_final,
           rwkv_mix, rwkv_w_rkv, rwkv_w_o, rwkv_w0, rwkv_w1, rwkv_w2, rwkv_a0, rwkv_a1, rwkv_a2,
           rwkv_v0, rwkv_v1, rwkv_v2, rwkv_g1, rwkv_g2, rwkv_k_k, rwkv_k_a, rwkv_r_k,
           rwkv_gn_w, rwkv_gn_b, pool_w, pool_scale, ffn_w1, ffn_w2):
    bf = lambda a: a.astype(BF16)
    vec = lambda a: a.reshape(a.shape[0], 1, -1)
    p = {'norm_mix': vec(norm_mix), 'norm_ffn': vec(norm_ffn), 'norm_final': norm_final.reshape(1, -1),
         'mix': rwkv_mix, 'w_rkv': bf(rwkv_w_rkv), 'w_o': bf(rwkv_w_o), 'w0': vec(rwkv_w0),
         'w1': bf(rwkv_w1), 'w2': bf(rwkv_w2), 'a0': vec(rwkv_a0), 'a1': bf(rwkv_a1),
         'a2': bf(rwkv_a2), 'v0': vec(rwkv_v0), 'v1': bf(rwkv_v1), 'v2': bf(rwkv_v2),
         'g1': bf(rwkv_g1), 'g2': bf(rwkv_g2), 'k_k': vec(rwkv_k_k), 'k_a': vec(rwkv_k_a),
         'r_k': vec(rwkv_r_k), 'gn_w': vec(rwkv_gn_w), 'gn_b': vec(rwkv_gn_b), 'pool_w': bf(pool_w),
         'pool_scale': vec(pool_scale), 'ffn_w1': bf(ffn_w1), 'ffn_w2': bf(ffn_w2)}
    dt = x_prompt.dtype
    nb, _, d = x_prompt.shape
    n_rwkv = state_wkv.shape[0]
    n_pool = state_pool.shape[0]
    nh = d // HEAD_DIM
    z_shift = jnp.zeros((n_rwkv, nb, d), dt)
    z_wkv = jnp.zeros((n_rwkv, nb, nh, HEAD_DIM, HEAD_DIM), dt)
    z_pool = jnp.zeros((n_pool, nb, POOL_BUF, d), dt)
    y_p, wkv_p, shift_p, pool_p = _trunk(x_prompt, 0, z_shift, z_wkv, z_pool, p)
    y_s, wkv_s, shift_s, pool_s = _trunk(x_sample, PAST_LEN, state_shift, state_wkv, state_pool, p)
    return (y_p, y_s, wkv_p, shift_p, pool_p, wkv_s, shift_s, pool_s)
```

```python
import functools
import math

import jax
import jax.numpy as jnp
from jax import lax
from jax.experimental import pallas as pl
from jax.experimental.pallas import tpu as pltpu

HEAD_DIM = 64
LANES = 128
SUBLANES = 8
WKV_LANES = 512
WKV_BLOCK_ELEMS = 8 * 128 * 512
POOL_WINDOWS = (2, 4, 8, 16)
POOL_BUF = max(POOL_WINDOWS) - 1
PAST_LEN = 16384
NORM_EPS = 1e-6
GN_EPS = 64e-5
L2_EPS = 1e-12
ROW_BLOCK = 512
FFN_CHUNK = 1024
WKV_CHUNK = 16
RELAYOUT_PITCH = 136
VMEM_LIMIT = 56 * 1024 * 1024

BF16 = jnp.bfloat16
F32 = jnp.float32


def _dot(a, b):
    return jnp.dot(a, b, preferred_element_type=F32)


def _dot_nt(a, b):
    return lax.dot_general(a, b, (((1,), (1,)), ((), ())), preferred_element_type=F32)


def _sigmoid(x):
    return 0.5 * jnp.tanh(0.5 * x) + 0.5


def _rmsnorm(x, g):
    ms = jnp.mean(x * x, axis=-1, keepdims=True)
    return x * lax.rsqrt(ms + NORM_EPS) * g


def _head_ones():
    r = lax.broadcasted_iota(jnp.int32, (LANES, LANES), 0) // HEAD_DIM
    c = lax.broadcasted_iota(jnp.int32, (LANES, LANES), 1) // HEAD_DIM
    return jnp.where(r == c, 1.0, 0.0).astype(BF16)


def _head_sum_bf16(p, ones):
    return _dot(p.astype(BF16), ones)


def _const_spec(shape):
    n = len(shape)
    return pl.BlockSpec(shape, lambda *_: (0,) * n)


def _pick(arr, *idx):
    shape = (None,) * len(idx) + arr.shape[len(idx):]
    at = tuple(idx) + (0,) * (arr.ndim - len(idx))
    return arr, pl.BlockSpec(shape, lambda *_: at)


def _rows_in(x_ref, nb):
    if len(x_ref.shape) == 2:
        return x_ref[...]
    return jnp.swapaxes(x_ref[...], 0, 1).reshape(nb * x_ref.shape[1], x_ref.shape[2])


def _rows_out(o_ref, val, nb):
    if len(o_ref.shape) == 2:
        o_ref[...] = val
    else:
        o_ref[...] = jnp.swapaxes(val.reshape(o_ref.shape[1], nb, val.shape[1]), 0, 1)


def _x_spec(x, tm, nb):
    if x.ndim == 2:
        return pl.BlockSpec((tm, x.shape[1]), lambda i: (i, 0))
    return pl.BlockSpec((nb, tm // nb, x.shape[2]), lambda i: (0, i, 0))


def _slab_lanes(d):
    return WKV_LANES if d % WKV_LANES == 0 else LANES


def _pre_kernel(has_vres, nb, *refs):
    it = iter(refs)
    x_ref = next(it); shift_ref = next(it); gmix_ref = next(it); mix_ref = next(it)
    wr_ref = next(it); wk_ref = next(it); wv_ref = next(it)
    w0_ref = next(it); w1_ref = next(it); w2_ref = next(it)
    a0_ref = next(it); a1_ref = next(it); a2_ref = next(it)
    if has_vres:
        v0_ref = next(it); v1_ref = next(it); v2_ref = next(it); vfirst_ref = next(it)
    g1_ref = next(it); g2_ref = next(it); kk_ref = next(it); ka_ref = next(it)
    r_out = next(it); lw_out = next(it); k_out = next(it); v_out = next(it)
    kk_out = next(it); b_out = next(it); g_out = next(it); shift_out = next(it)
    carry = next(it)

    tm, d = g_out.shape
    nslab, _, slab = r_out.shape

    @pl.when(pl.program_id(0) == 0)
    def _():
        carry[...] = shift_ref[...]

    h = _rmsnorm(_rows_in(x_ref, nb), gmix_ref[...])
    if tm > nb:
        hp = jnp.concatenate([carry[...], h[: tm - nb]], axis=0)
    else:
        hp = carry[...]
    carry[...] = h[tm - nb:]
    shift_out[...] = h[tm - nb:]
    dx = hp - h

    def mixed(i):
        return (h + dx * mix_ref[i:i + 1, :]).astype(BF16)

    xv = mixed(2)
    lora_w = _dot(mixed(3), w1_ref[...])
    lora_a = _dot(mixed(4), a1_ref[...])
    if has_vres:
        lora_v = _dot(xv, v1_ref[...])
    lora_g = _dot(mixed(5), g1_ref[...])
    k = _dot(mixed(1), wk_ref[...])
    wpre = w0_ref[...] + _dot(jnp.tanh(lora_w).astype(BF16), w2_ref[...])
    a = _sigmoid(a0_ref[...] + _dot(lora_a.astype(BF16), a2_ref[...]))
    if has_vres:
        gate = _sigmoid(v0_ref[...] + _dot(lora_v.astype(BF16), v2_ref[...]))
    g = _dot(_sigmoid(lora_g).astype(BF16), g2_ref[...])
    g_out[...] = g
    lw = -math.exp(-0.5) * _sigmoid(wpre)

    ones = _head_ones()
    kk = k * kk_ref[...]
    k2 = k * (1.0 + (a - 1.0) * ka_ref[...])
    kkn = []
    for p in range(d // LANES):
        kkp = kk[:, p * LANES:(p + 1) * LANES]
        kkn.append(kkp * lax.rsqrt(jnp.maximum(_head_sum_bf16(kkp * kkp, ones), L2_EPS * L2_EPS)))
    kkn = jnp.concatenate(kkn, axis=1)
    bb = kkn * a
    v = _dot(xv, wv_ref[...])
    if has_vres:
        vfirst = jnp.concatenate([vfirst_ref[p] for p in range(nslab)], axis=1)
        v = v + (vfirst - v) * gate
    r = _dot(mixed(0), wr_ref[...])
    for p in range(nslab):
        sl = slice(p * slab, (p + 1) * slab)
        r_out[p] = r[:, sl]
        lw_out[p] = lw[:, sl]
        k_out[p] = k2[:, sl]
        v_out[p] = v[:, sl]
        kk_out[p] = kkn[:, sl]
        b_out[p] = bb[:, sl]


def _pre_call(x, p, i, j, shift_states, vfirst, nb, tm):
    d = x.shape[-1]
    rows = x.shape[0] if x.ndim == 2 else x.shape[0] * x.shape[1]
    slab = _slab_lanes(d)
    nslab = d // slab
    has_vres = j > 0
    sm_spec = pl.BlockSpec((nslab, tm, slab), lambda i_: (0, i_, 0))
    ops = [(x, _x_spec(x, tm, nb)), _pick(shift_states, j), _pick(p['norm_mix'], i), _pick(p['mix'], j),
           _pick(p['w_rkv'], j, 0), _pick(p['w_rkv'], j, 1), _pick(p['w_rkv'], j, 2),
           _pick(p['w0'], j), _pick(p['w1'], j), _pick(p['w2'], j),
           _pick(p['a0'], j), _pick(p['a1'], j), _pick(p['a2'], j)]
    if has_vres:
        ops += [_pick(p['v0'], j - 1), _pick(p['v1'], j - 1), _pick(p['v2'], j - 1), (vfirst, sm_spec)]
    ops += [_pick(p['g1'], j), _pick(p['g2'], j), _pick(p['k_k'], j), _pick(p['k_a'], j)]
    sm = jax.ShapeDtypeStruct((nslab, rows, slab), F32)
    out_shape = [sm] * 6 + [jax.ShapeDtypeStruct((rows, d), F32), jax.ShapeDtypeStruct((nb, d), F32)]
    out_specs = [sm_spec] * 6 + [pl.BlockSpec((tm, d), lambda i_: (i_, 0)), _const_spec((nb, d))]
    return pl.pallas_call(
        functools.partial(_pre_kernel, has_vres, nb),
        grid=(rows // tm,),
        in_specs=[o[1] for o in ops],
        out_specs=out_specs,
        out_shape=out_shape,
        scratch_shapes=[pltpu.VMEM((nb, d), F32)],
        compiler_params=pltpu.CompilerParams(
            dimension_semantics=("arbitrary",), vmem_limit_bytes=VMEM_LIMIT),
        name="rwkv_pre",
    )(*[o[0] for o in ops])


_WKV_SLOTS = ("at", "rt", "u", "v", "bh", "kh", "x0", "y0")


def _wkv_kernel(tchunk, nchunks, has_alias, layer, batch_minor, r_ref, lw_ref, k_ref, v_ref, kk_ref, b_ref,
                s0_ref, rk_ref, gnw_ref, gnb_ref, *rest):
    y_ref, sout_ref, s_scr, *work = rest[1:] if has_alias else rest
    if batch_minor:
        t_scr, *work = work
    nseq = SUBLANES
    npl = s_scr.shape[1]
    cp = max(tchunk, 8)
    n = cp * nseq
    nv = tchunk * nseq
    grp = LANES // (2 * cp)
    nslot = len(_WKV_SLOTS)
    scr = [dict(zip(_WKV_SLOTS, work[q * nslot:(q + 1) * nslot])) for q in range(npl)]

    if batch_minor:
        group = pl.program_id(1)

        @pl.when((group == 0) & (pl.program_id(2) == 0))
        def _():
            def relayout(i, carry):
                for q in range(npl):
                    for e in range(SUBLANES):
                        v = i * SUBLANES + e
                        tile = jnp.concatenate([s0_ref[2 * q, v], s0_ref[2 * q + 1, v]], axis=0)
                        t_scr[q, pl.ds(pl.multiple_of(v * RELAYOUT_PITCH, SUBLANES), LANES), :] = tile.T
                return carry
            lax.fori_loop(0, HEAD_DIM // SUBLANES, relayout, 0)

        @pl.when(pl.program_id(2) == 0)
        def _():
            lane = lax.broadcasted_iota(jnp.int32, (HEAD_DIM, LANES), 1)
            for s in range(nseq):
                for q in range(npl):
                    side = t_scr[q, pl.ds(group * nseq + s, HEAD_DIM, stride=RELAYOUT_PITCH), :]
                    s_scr[s, q] = jnp.concatenate([jnp.where(lane < HEAD_DIM, side, 0.0),
                                                   jnp.where(lane >= HEAD_DIM, side, 0.0)], axis=0)
    else:
        @pl.when(pl.program_id(2) == 0)
        def _():
            zero = jnp.zeros((HEAD_DIM, HEAD_DIM), F32)
            for s in range(nseq):
                for q in range(npl):
                    top = jnp.concatenate([s0_ref[s, 2 * q], zero], axis=1)
                    bot = jnp.concatenate([zero, s0_ref[s, 2 * q + 1]], axis=1)
                    s_scr[s, q] = jnp.concatenate([top, bot], axis=0)

    ones = _head_ones()
    hr = lax.broadcasted_iota(jnp.int32, (LANES, LANES), 0) // HEAD_DIM
    hc = lax.broadcasted_iota(jnp.int32, (LANES, LANES), 1) // HEAD_DIM
    same_head = hr == hc
    lane_head = lax.broadcasted_iota(jnp.int32, (1, LANES), 1) // HEAD_DIM
    head_f32 = [jnp.where(lane_head == hh, 1.0, 0.0) for hh in range(2)]
    head_bf16 = [m.astype(BF16) for m in head_f32]
    ri = lax.broadcasted_iota(jnp.int32, (n, 2 * n), 0)
    ci = lax.broadcasted_iota(jnp.int32, (n, 2 * n), 1) % n
    same_seq = (ri % nseq) == (ci % nseq)
    tri_incl = jnp.where(same_seq & (ci // nseq <= ri // nseq), 1.0, 0.0)
    tri_strict = jnp.where(same_seq & (ci // nseq < ri // nseq), 1.0, 0.0)[:, :n]
    half = cp // 2 if cp >= 16 else cp
    nh = half * nseq
    if half < cp:
        rh = lax.broadcasted_iota(jnp.int32, (n, nh), 0) % nseq
        ch = lax.broadcasted_iota(jnp.int32, (n, nh), 1) % nseq
        same_seq_half = jnp.where(rh == ch, 1.0, 0.0).astype(BF16)

    def load(ref, c, q):
        val = ref[pl.ds(c * tchunk, tchunk), :, q * LANES:(q + 1) * LANES].reshape(nv, LANES)
        if n > nv:
            val = jnp.concatenate([val, jnp.zeros((n - nv, LANES), F32)], axis=0)
        return val

    def tile_rows(slab, reps):
        return slab if reps == 1 else jnp.concatenate([slab] * reps, axis=0)

    def seq_rows(ref, s):
        return ref[pl.ds(s, cp, stride=nseq), :]

    def phase_a(c, q, st):
        r = load(r_ref, c, q); lw = load(lw_ref, c, q); k = load(k_ref, c, q)
        v = load(v_ref, c, q); kk = load(kk_ref, c, q); b = load(b_ref, c, q)

        acc = jnp.zeros((nseq, LANES), F32)
        cums = []
        for t in range(cp):
            acc = acc + lw[t * nseq:(t + 1) * nseq]
            cums.append(acc)
        cum = jnp.concatenate(cums, axis=0)
        w_inc = jnp.exp(cum)
        w_inv = jnp.exp(-cum)
        w_exc = jnp.concatenate([jnp.ones((nseq, LANES), F32), w_inc[: n - nseq]], axis=0)
        w_tot = w_inc[n - nseq:]
        at = -kk * w_exc
        rt = r * w_inc
        kt = k * w_inv
        bt = b * w_inv
        w_tot_rows = tile_rows(w_tot, cp)
        st.update(r=r, k=k, v=v, at=at, rt=rt, w_tot=w_tot, bh=bt * w_tot_rows, kh=kt * w_tot_rows)

        at16 = at.astype(BF16)
        rt16 = rt.astype(BF16)
        kt16 = kt.astype(BF16)
        kb16 = jnp.concatenate([kt16, bt.astype(BF16)], axis=0)
        gram_a = _dot_nt(jnp.concatenate([at16 * head_bf16[0], at16 * head_bf16[1]], axis=0), kt16)
        gram_r = _dot_nt(jnp.concatenate([rt16 * head_bf16[0], rt16 * head_bf16[1]], axis=0), kb16)
        st["ga"] = jnp.concatenate([gram_a[:n] * tri_strict, gram_a[n:] * tri_strict], axis=1).astype(BF16)
        st["gr"] = jnp.concatenate([gram_r[:n] * tri_incl, gram_r[n:] * tri_incl], axis=1).astype(BF16)
        st["vm"] = jnp.concatenate([v * head_f32[0], v * head_f32[1]], axis=0)
        yield

        prods = []
        for t in range(1, cp):
            j0 = (t // half) * half
            if t > j0:
                prods.append(tile_rows(at[t * nseq:(t + 1) * nseq], t - j0) * bt[j0 * nseq:t * nseq])
        st["coef"] = _head_sum_bf16(jnp.concatenate(prods, axis=0), ones)
        if half < cp:
            lhs = jnp.concatenate([at16[nh:] * head_bf16[0], at16[nh:] * head_bf16[1]], axis=0)
            n21 = _dot_nt(lhs, bt.astype(BF16)[:nh]).astype(BF16) * same_seq_half
            st["n21"] = jnp.concatenate([n21[:nh], n21[nh:]], axis=1)

    def phase_b(c, q, st):
        w = scr[q]
        w["at"][...] = st["at"]
        w["rt"][...] = st["rt"]
        w["v"][...] = st["v"]
        w["bh"][...] = st["bh"]
        w["kh"][...] = st["kh"]
        for s in range(nseq):
            lhs = jnp.concatenate([seq_rows(w["at"], s), seq_rows(w["rt"], s)], axis=0).astype(BF16)
            out = _dot_nt(lhs, s_scr[s, q].astype(BF16))
            w["x0"][pl.ds(s, cp, stride=nseq), :] = out[:cp]
            w["y0"][pl.ds(s, cp, stride=nseq), :] = out[cp:]
        yield

        x = w["x0"][...] + _dot(st["ga"], st["vm"].astype(BF16))
        yield

        coef = st["coef"]
        us = []
        off = 0
        for t in range(cp):
            j0 = (t // half) * half
            if t == half:
                u1 = jnp.concatenate(us, axis=0).astype(BF16)
                x = jnp.concatenate([x[:nh], x[nh:] + _dot(
                    st["n21"], jnp.concatenate([u1 * head_bf16[0], u1 * head_bf16[1]], axis=0))], axis=0)
            u_t = x[t * nseq:(t + 1) * nseq]
            for j in range(j0, t):
                u_t = u_t + coef[off + (j - j0) * nseq:off + (j - j0 + 1) * nseq] * us[j]
            off += (t - j0) * nseq
            us.append(u_t)
        u = jnp.concatenate(us, axis=0)
        w["u"][...] = u
        yield

        vm = st["vm"]
        um = jnp.concatenate([u * head_f32[0], u * head_f32[1]], axis=0)
        vum = jnp.concatenate([vm[:n], um[:n], vm[n:], um[n:]], axis=0)
        st["y"] = w["y0"][...] + _dot(st["gr"], vum.astype(BF16))
        yield

        w_tot = st["w_tot"]
        zp = []
        for s in range(nseq):
            zp += [seq_rows(w["u"], s), seq_rows(w["v"], s)]
        z_t = jnp.concatenate(zp, axis=0).T.astype(BF16)
        for g0 in range(0, nseq, grp):
            cols = []
            for e in range(grp):
                gs = jnp.concatenate([seq_rows(w["bh"], g0 + e), seq_rows(w["kh"], g0 + e)],
                                     axis=0).astype(BF16)
                blk = [gs]
                if e > 0:
                    blk = [jnp.zeros((2 * cp * e, LANES), BF16)] + blk
                if e < grp - 1:
                    blk = blk + [jnp.zeros((2 * cp * (grp - 1 - e), LANES), BF16)]
                cols.append(jnp.concatenate(blk, axis=0))
            lhs = z_t[:, (g0 // grp) * LANES:(g0 // grp + 1) * LANES]
            delta = _dot(lhs, jnp.concatenate(cols, axis=1))
            for e in range(grp):
                s = g0 + e
                dl = delta[:, e * LANES:(e + 1) * LANES]
                s_scr[s, q] = s_scr[s, q] * w_tot[s:s + 1, :] + jnp.where(same_head, dl, 0.0)

    def phase_c(c, q, st):
        ql = slice(q * LANES, (q + 1) * LANES)
        yv, rv, kv, vv = st["y"][:nv], st["r"][:nv], st["k"][:nv], st["v"][:nv]
        hs = _head_sum_bf16(jnp.concatenate([yv, rv * kv * rk_ref[:, ql]], axis=0), ones)
        dlt = yv - hs[:nv] * (1.0 / HEAD_DIM)
        bonus = hs[nv:] * vv
        yield
        var = _head_sum_bf16(dlt * dlt, ones) * (1.0 / HEAD_DIM)
        yn = dlt * lax.rsqrt(var + GN_EPS) * gnw_ref[:, ql] + gnb_ref[:, ql]
        y_ref[pl.ds(c * tchunk, tchunk), :, ql] = (yn + bonus).reshape(tchunk, nseq, LANES)

    def advance(gens):
        return [next(g, "done") is None for g in gens]

    def drain(gens):
        while any(advance(gens)):
            pass

    def chunk_group(cs):
        sts = [[{} for _ in range(npl)] for _ in cs]
        ga = [[phase_a(c, q, sts[i][q]) for q in range(npl)] for i, c in enumerate(cs)]
        gb = [[phase_b(c, q, sts[i][q]) for q in range(npl)] for i, c in enumerate(cs)]
        gc = [[phase_c(c, q, sts[i][q]) for q in range(npl)] for i, c in enumerate(cs)]
        drain(ga[0])
        for i in range(len(cs)):
            fill = []
            if i + 1 < len(cs):
                fill.append(ga[i + 1])
            if i > 0:
                fill.append(gc[i - 1])
            while any(advance(gb[i])):
                for f in fill:
                    advance(f)
            for f in fill:
                drain(f)
        drain(gc[-1])

    unroll = 2 if nchunks % 2 == 0 else 1

    def chunk(i, carry):
        chunk_group([i * unroll + e for e in range(unroll)])
        return carry

    lax.fori_loop(0, nchunks // unroll, chunk, 0)

    @pl.when(pl.program_id(2) == pl.num_programs(2) - 1)
    def _():
        out = sout_ref if has_alias else sout_ref.at[layer]
        if not has_alias:
            for other in range(sout_ref.shape[0]):
                if other != layer:
                    sout_ref[other] = jnp.zeros(sout_ref.shape[1:], F32)
        for s in range(nseq):
            for q in range(npl):
                out[s, 2 * q] = s_scr[s, q, :HEAD_DIM, :HEAD_DIM]
                out[s, 2 * q + 1] = s_scr[s, q, HEAD_DIM:, HEAD_DIM:]


def _wkv_call(r, lw, k, v, kk, b, states, new_states, layer, p, nb, nt):
    nslab, _, slab = r.shape
    npl = slab // LANES
    ngroup = nb // SUBLANES
    tchunk = min(WKV_CHUNK, nt)
    tb = min(WKV_BLOCK_ELEMS // (SUBLANES * slab), nt)
    nchunks = tb // tchunk
    assert nt % tb == 0 and tb % tchunk == 0 and nb % SUBLANES == 0
    view = lambda a: a.reshape(nslab, nt, ngroup, SUBLANES, slab)
    hps = 2 * npl
    batch_minor = nb == LANES
    if batch_minor:
        order = lambda f: (lambda p_, g, t: f(g, p_, t))
        grid = (nslab, ngroup, nt // tb)
        states_in = states.transpose(0, 2, 3, 4, 1)
        st_in_spec = pl.BlockSpec((None, hps, HEAD_DIM, HEAD_DIM, nb), lambda p_, g, t: (layer, p_, 0, 0, 0),
                                  pipeline_mode=pl.Buffered(1))
        relayout_scr = [pltpu.VMEM((npl, HEAD_DIM * RELAYOUT_PITCH, LANES), F32)]
    else:
        order = lambda f: f
        grid = (ngroup, nslab, nt // tb)
        states_in = states
        st_in_spec = pl.BlockSpec((None, SUBLANES, hps, HEAD_DIM, HEAD_DIM),
                                  lambda g, p_, t: (layer, g, p_, 0, 0))
        relayout_scr = []
    act_spec = pl.BlockSpec((None, tb, None, SUBLANES, slab), order(lambda g, p_, t: (p_, t, g, 0, 0)))
    has_alias = new_states is not None
    if has_alias:
        st_out_spec = pl.BlockSpec((None, SUBLANES, hps, HEAD_DIM, HEAD_DIM),
                                   order(lambda g, p_, t: (layer, g, p_, 0, 0)))
    else:
        st_out_spec = pl.BlockSpec((states.shape[0], SUBLANES, hps, HEAD_DIM, HEAD_DIM),
                                   order(lambda g, p_, t: (0, g, p_, 0, 0)))
    extra_in = [new_states] if has_alias else []
    extra_spec = [pl.BlockSpec(memory_space=pl.ANY)] if has_alias else []
    par_spec = pl.BlockSpec((None, 1, slab), order(lambda g, p_, t: (layer, 0, p_)))
    n = max(tchunk, 8) * SUBLANES
    y, s_out = pl.pallas_call(
        functools.partial(_wkv_kernel, tchunk, nchunks, has_alias, layer, batch_minor),
        grid=grid,
        in_specs=[act_spec] * 6 + [st_in_spec] + [par_spec] * 3 + extra_spec,
        out_specs=[act_spec, st_out_spec],
        out_shape=[jax.ShapeDtypeStruct((nslab, nt, ngroup, SUBLANES, slab), F32),
                   jax.ShapeDtypeStruct(states.shape, F32)],
        input_output_aliases={10: 1} if has_alias else {},
        scratch_shapes=[pltpu.VMEM((SUBLANES, npl, LANES, LANES), F32)] + relayout_scr
        + [pltpu.VMEM((n, LANES), F32)] * (len(_WKV_SLOTS) * npl),
        compiler_params=pltpu.CompilerParams(
            dimension_semantics=("arbitrary", "arbitrary", "arbitrary"),
            vmem_limit_bytes=VMEM_LIMIT),
        name="wkv7",
    )(view(r), view(lw), view(k), view(v), view(kk), view(b), states_in, p['r_k'], p['gn_w'], p['gn_b'],
      *extra_in)
    return y.reshape(nslab, nt * nb, slab), s_out


def _ffn_stages(x, gffn, w1_ref, w2_ref, out):
    h = _rmsnorm(x, gffn).astype(BF16)
    dff = w1_ref.shape[1]
    step = min(dff, FFN_CHUNK)
    acc = x
    for c in range(dff // step):
        hh = _dot(h, w1_ref[:, c * step:(c + 1) * step])
        yield
        hh = jnp.square(jnp.maximum(hh, 0.0)).astype(BF16)
        acc = acc + _dot(hh, w2_ref[c * step:(c + 1) * step, :])
        yield
    out.append(acc)


def _ffn(x, gffn, w1_ref, w2_ref):
    out = []
    for _ in _ffn_stages(x, gffn, w1_ref, w2_ref, out):
        pass
    return out[0]


def _attn_ffn_kernel(final_norm, nb, x_ref, y_ref, g_ref, wo_ref, gffn_ref, w1_ref, w2_ref, gfin_ref,
                     o_ref):
    nslab = y_ref.shape[0]
    y = jnp.concatenate([y_ref[q] for q in range(nslab)], axis=1)
    x = _rows_in(x_ref, nb) + _dot((y * g_ref[...]).astype(BF16), wo_ref[...])
    out = _ffn(x, gffn_ref[...], w1_ref, w2_ref)
    if final_norm:
        out = _rmsnorm(out, gfin_ref[...])
    _rows_out(o_ref, out, nb)


def _out_x(rows, d, tm, nb, batch_major_out):
    if batch_major_out:
        return (jax.ShapeDtypeStruct((nb, rows // nb, d), F32),
                pl.BlockSpec((nb, tm // nb, d), lambda i: (0, i, 0)))
    return jax.ShapeDtypeStruct((rows, d), F32), pl.BlockSpec((tm, d), lambda i: (i, 0))


def _attn_ffn_call(x, y, g, p, i, j, final_norm, batch_major_out, nb, tm):
    rows, d = g.shape
    nslab, _, slab = y.shape
    out_sds, out_spec = _out_x(rows, d, tm, nb, batch_major_out)
    ops = [(x, _x_spec(x, tm, nb)), (y, pl.BlockSpec((nslab, tm, slab), lambda i_: (0, i_, 0))),
           (g, pl.BlockSpec((tm, d), lambda i_: (i_, 0))), _pick(p['w_o'], j), _pick(p['norm_ffn'], i),
           _pick(p['ffn_w1'], i), _pick(p['ffn_w2'], i), _pick(p['norm_final'])]
    return pl.pallas_call(
        functools.partial(_attn_ffn_kernel, final_norm, nb),
        grid=(rows // tm,),
        in_specs=[o[1] for o in ops],
        out_specs=out_spec,
        out_shape=out_sds,
        compiler_params=pltpu.CompilerParams(
            dimension_semantics=("arbitrary",), vmem_limit_bytes=VMEM_LIMIT),
        name="attn_out_ffn",
    )(*[o[0] for o in ops])


def _pool_mix_stages(x, prev, first_row, nb, start_pos, gmix_ref, pw_ref, ps_ref, out):
    tm, d = x.shape
    gw = d // len(POOL_WINDOWS)
    h = _rmsnorm(x, gmix_ref[...])
    ext = jnp.concatenate([prev, h], axis=0)
    rowi = lax.broadcasted_iota(jnp.int32, (tm, gw), 0) + first_row
    pos = start_pos + jnp.right_shift(rowi, int(math.log2(nb)))
    yield
    ys = []
    for gi, w in enumerate(POOL_WINDOWS):
        sl = slice(gi * gw, (gi + 1) * gw)
        s = ext[:, sl]
        span = 1
        while span < w:
            s = s[span * nb:] + s[: s.shape[0] - span * nb]
            span *= 2
        s = s[s.shape[0] - tm:]
        cnt = jnp.minimum(pos + 1, w).astype(F32)
        diff = (s / cnt - h[:, sl]).astype(BF16)
        ys.append(_dot(diff, pw_ref[gi]))
        yield
    out += [x + jnp.concatenate(ys, axis=1) * ps_ref[...], ext[tm:]]


def _pool_mix(*args, **kwargs):
    out = []
    for _ in _pool_mix_stages(*args, out=out, **kwargs):
        pass
    return out


def _pool_ffn_kernel(final_norm, nb, start_pos, lookahead, *refs):
    if lookahead:
        x_ref, xn_ref, *refs = refs
    else:
        x_ref, *refs = refs
    buf_ref, gmix_ref, pw_ref, ps_ref, gffn_ref, w1_ref, w2_ref, gfin_ref, o_ref, buf_out, carry, *scr = refs
    tm = x_ref.shape[0] if len(x_ref.shape) == 2 else nb * x_ref.shape[1]
    step = pl.program_id(0)
    mix = functools.partial(_pool_mix, nb=nb, start_pos=start_pos, gmix_ref=gmix_ref, pw_ref=pw_ref,
                            ps_ref=ps_ref)

    if lookahead:
        xm, = scr

        @pl.when(step == 0)
        def _():
            x0, tail0 = mix(_rows_in(x_ref, nb), buf_ref[...], 0)
            xm[0] = x0
            carry[...] = tail0

        res, nxt = [], []
        ffn = _ffn_stages(xm[step % 2], gffn_ref[...], w1_ref, w2_ref, res)
        ahead = _pool_mix_stages(_rows_in(xn_ref, nb), carry[...], (step + 1) * tm, nb, start_pos,
                                 gmix_ref, pw_ref, ps_ref, nxt)
        busy = True
        while busy:
            busy = next(ffn, "done") is None
            busy = (next(ahead, "done") is None) or busy
        out = res[0]
        xn, tail = nxt
    else:
        x, tail = mix(_rows_in(x_ref, nb), buf_ref[...], 0)
        buf_out[...] = tail
        out = _ffn(x, gffn_ref[...], w1_ref, w2_ref)
    if final_norm:
        out = _rmsnorm(out, gfin_ref[...])
    _rows_out(o_ref, out, nb)
    if lookahead:
        xm[(step + 1) % 2] = xn
        carry[...] = jnp.where(step < pl.num_programs(0) - 1, tail, carry[...])
        buf_out[...] = carry[...]


def _pool_ffn_call(x, buf, p, i, j, final_norm, batch_major_out, nb, start_pos, tm):
    d = x.shape[-1]
    rows = x.shape[0] if x.ndim == 2 else x.shape[0] * x.shape[1]
    assert nb & (nb - 1) == 0
    nsteps = rows // tm
    lookahead = nsteps > 1
    out_sds, out_spec = _out_x(rows, d, tm, nb, batch_major_out)
    ops = [(x, _x_spec(x, tm, nb))]
    if lookahead:
        ahead = lambda i_: jnp.minimum(i_ + 1, nsteps - 1)
        if x.ndim == 2:
            ops.append((x, pl.BlockSpec((tm, d), lambda i_: (ahead(i_), 0))))
        else:
            ops.append((x, pl.BlockSpec((nb, tm // nb, d), lambda i_: (0, ahead(i_), 0))))
    ops += [(buf, _const_spec(buf.shape)), _pick(p['norm_mix'], i),
            _pick(p['pool_w'], j), _pick(p['pool_scale'], j), _pick(p['norm_ffn'], i),
            _pick(p['ffn_w1'], i), _pick(p['ffn_w2'], i), _pick(p['norm_final'])]
    return pl.pallas_call(
        functools.partial(_pool_ffn_kernel, final_norm, nb, start_pos, lookahead),
        grid=(nsteps,),
        in_specs=[o[1] for o in ops],
        out_specs=[out_spec, _const_spec(buf.shape)],
        out_shape=[out_sds, jax.ShapeDtypeStruct(buf.shape, F32)],
        scratch_shapes=[pltpu.VMEM(buf.shape, F32)] + ([pltpu.VMEM((2, tm, d), F32)] if lookahead else []),
        compiler_params=pltpu.CompilerParams(
            dimension_semantics=("arbitrary",), vmem_limit_bytes=VMEM_LIMIT),
        name="pool_ffn",
    )(*[o[0] for o in ops])


def _trunk(x, start_pos, shift_states, wkv_states, pool_bufs, p):
    nb, nt, d = x.shape
    depth = p['norm_mix'].shape[0]
    rows = nb * nt
    tm = min(ROW_BLOCK, rows)
    assert rows % tm == 0 and tm % nb == 0
    batch_major = nb == SUBLANES
    xt = x if batch_major else x.transpose(1, 0, 2).reshape(rows, d)
    new_shift, new_wkv, new_pool = [], None, []
    v_first = None
    for i in range(depth):
        j = i // 2
        last = i == depth - 1
        bm_out = batch_major and last
        if i % 2 == 0:
            r, lw, k, v, kk, b, g, s_shift = _pre_call(xt, p, i, j, shift_states, v_first, nb, tm)
            if j == 0:
                v_first = v
            y, new_wkv = _wkv_call(r, lw, k, v, kk, b, wkv_states, new_wkv, j, p, nb, nt)
            new_shift.append(s_shift)
            xt = _attn_ffn_call(xt, y, g, p, i, j, last, bm_out, nb, tm)
        else:
            buf = pool_bufs[j].transpose(1, 0, 2).reshape(POOL_BUF * nb, d)
            xt, s_buf = _pool_ffn_call(xt, buf, p, i, j, last, bm_out, nb, start_pos, tm)
            new_pool.append(s_buf.reshape(POOL_BUF, nb, d).transpose(1, 0, 2))
    y = xt if batch_major else xt.reshape(nt, nb, d).transpose(1, 0, 2)
    return y, new_wkv, jnp.stack(new_shift), jnp.stack(new_pool)


def kernel(x_prompt, x_sample, state_wkv, state_shift, state_pool, norm_mix, norm_ffn, norm_final,
           rwkv_mix, rwkv_w_rkv, rwkv_w_o, rwkv_w0, rwkv_w1, rwkv_w2, rwkv_a0, rwkv_a1, rwkv_a2,
           rwkv_v0, rwkv_v1, rwkv_v2, rwkv_g1, rwkv_g2, rwkv_k_k, rwkv_k_a, rwkv_r_k,
           rwkv_gn_w, rwkv_gn_b, pool_w, pool_scale, ffn_w1, ffn_w2):
    bf = lambda a: a.astype(BF16)
    vec = lambda a: a.reshape(a.shape[0], 1, -1)
    p = {'norm_mix': vec(norm_mix), 'norm_ffn': vec(norm_ffn), 'norm_final': norm_final.reshape(1, -1),
         'mix': rwkv_mix, 'w_rkv': bf(rwkv_w_rkv), 'w_o': bf(rwkv_w_o), 'w0': vec(rwkv_w0),
         'w1': bf(rwkv_w1), 'w2': bf(rwkv_w2), 'a0': vec(rwkv_a0), 'a1': bf(rwkv_a1),
         'a2': bf(rwkv_a2), 'v0': vec(rwkv_v0), 'v1': bf(rwkv_v1), 'v2': bf(rwkv_v2),
         'g1': bf(rwkv_g1), 'g2': bf(rwkv_g2), 'k_k': vec(rwkv_k_k), 'k_a': vec(rwkv_k_a),
         'r_k': vec(rwkv_r_k), 'gn_w': vec(rwkv_gn_w), 'gn_b': vec(rwkv_gn_b), 'pool_w': bf(pool_w),
         'pool_scale': vec(pool_scale), 'ffn_w1': bf(ffn_w1), 'ffn_w2': bf(ffn_w2)}
    dt = x_prompt.dtype
    nb, _, d = x_prompt.shape
    n_rwkv = state_wkv.shape[0]
    n_pool = state_pool.shape[0]
    nh = d // HEAD_DIM
    z_shift = jnp.zeros((n_rwkv, nb, d), dt)
    z_wkv = jnp.zeros((n_rwkv, nb, nh, HEAD_DIM, HEAD_DIM), dt)
    z_pool = jnp.zeros((n_pool, nb, POOL_BUF, d), dt)
    y_p, wkv_p, shift_p, pool_p = _trunk(x_prompt, 0, z_shift, z_wkv, z_pool, p)
    y_s, wkv_s, shift_s, pool_s = _trunk(x_sample, PAST_LEN, state_shift, state_wkv, state_pool, p)
    return (y_p, y_s, wkv_p, shift_p, pool_p, wkv_s, shift_s, pool_s)
```

```python
import functools
import math

import jax
import jax.numpy as jnp
from jax import lax
from jax.experimental import pallas as pl
from jax.experimental.pallas import tpu as pltpu

HEAD_DIM = 64
LANES = 128
SUBLANES = 8
WKV_LANES = 512
WKV_BLOCK_ELEMS = 8 * 128 * 512
POOL_WINDOWS = (2, 4, 8, 16)
POOL_BUF = max(POOL_WINDOWS) - 1
PAST_LEN = 16384
NORM_EPS = 1e-6
GN_EPS = 64e-5
L2_EPS = 1e-12
ROW_BLOCK = 512
FFN_CHUNK = 1024
WKV_CHUNK = 16
RELAYOUT_PITCH = 136
VMEM_LIMIT = 56 * 1024 * 1024

BF16 = jnp.bfloat16
F32 = jnp.float32


def _dot(a, b):
    return jnp.dot(a, b, preferred_element_type=F32)


def _dot_nt(a, b):
    return lax.dot_general(a, b, (((1,), (1,)), ((), ())), preferred_element_type=F32)


def _sigmoid(x):
    return 0.5 * jnp.tanh(0.5 * x) + 0.5


def _rmsnorm(x, g):
    ms = jnp.mean(x * x, axis=-1, keepdims=True)
    return x * lax.rsqrt(ms + NORM_EPS) * g


def _head_ones():
    r = lax.broadcasted_iota(jnp.int32, (LANES, LANES), 0) // HEAD_DIM
    c = lax.broadcasted_iota(jnp.int32, (LANES, LANES), 1) // HEAD_DIM
    return jnp.where(r == c, 1.0, 0.0).astype(BF16)


def _head_sum_bf16(p, ones):
    return _dot(p.astype(BF16), ones)


def _const_spec(shape):
    n = len(shape)
    return pl.BlockSpec(shape, lambda *_: (0,) * n)


def _pick(arr, *idx):
    shape = (None,) * len(idx) + arr.shape[len(idx):]
    at = tuple(idx) + (0,) * (arr.ndim - len(idx))
    return arr, pl.BlockSpec(shape, lambda *_: at)


def _rows_in(x_ref, nb):
    if len(x_ref.shape) == 2:
        return x_ref[...]
    return jnp.swapaxes(x_ref[...], 0, 1).reshape(nb * x_ref.shape[1], x_ref.shape[2])


def _rows_out(o_ref, val, nb):
    if len(o_ref.shape) == 2:
        o_ref[...] = val
    else:
        o_ref[...] = jnp.swapaxes(val.reshape(o_ref.shape[1], nb, val.shape[1]), 0, 1)


def _x_spec(x, tm, nb):
    if x.ndim == 2:
        return pl.BlockSpec((tm, x.shape[1]), lambda i: (i, 0))
    return pl.BlockSpec((nb, tm // nb, x.shape[2]), lambda i: (0, i, 0))


def _slab_lanes(d):
    return WKV_LANES if d % WKV_LANES == 0 else LANES


def _pre_kernel(has_vres, nb, *refs):
    it = iter(refs)
    x_ref = next(it); shift_ref = next(it); gmix_ref = next(it); mix_ref = next(it)
    wr_ref = next(it); wk_ref = next(it); wv_ref = next(it)
    w0_ref = next(it); w1_ref = next(it); w2_ref = next(it)
    a0_ref = next(it); a1_ref = next(it); a2_ref = next(it)
    if has_vres:
        v0_ref = next(it); v1_ref = next(it); v2_ref = next(it); vfirst_ref = next(it)
    g1_ref = next(it); g2_ref = next(it); kk_ref = next(it); ka_ref = next(it)
    r_out = next(it); lw_out = next(it); k_out = next(it); v_out = next(it)
    kk_out = next(it); b_out = next(it); g_out = next(it); shift_out = next(it)
    carry = next(it)

    tm, d = g_out.shape
    nslab, _, slab = r_out.shape

    @pl.when(pl.program_id(0) == 0)
    def _():
        carry[...] = shift_ref[...]

    h = _rmsnorm(_rows_in(x_ref, nb), gmix_ref[...])
    if tm > nb:
        hp = jnp.concatenate([carry[...], h[: tm - nb]], axis=0)
    else:
        hp = carry[...]
    carry[...] = h[tm - nb:]
    shift_out[...] = h[tm - nb:]
    dx = hp - h

    def mixed(i):
        return (h + dx * mix_ref[i:i + 1, :]).astype(BF16)

    xv = mixed(2)
    lora_w = _dot(mixed(3), w1_ref[...])
    lora_a = _dot(mixed(4), a1_ref[...])
    if has_vres:
        lora_v = _dot(xv, v1_ref[...])
    lora_g = _dot(mixed(5), g1_ref[...])
    k = _dot(mixed(1), wk_ref[...])
    wpre = w0_ref[...] + _dot(jnp.tanh(lora_w).astype(BF16), w2_ref[...])
    a = _sigmoid(a0_ref[...] + _dot(lora_a.astype(BF16), a2_ref[...]))
    if has_vres:
        gate = _sigmoid(v0_ref[...] + _dot(lora_v.astype(BF16), v2_ref[...]))
    g = _dot(_sigmoid(lora_g).astype(BF16), g2_ref[...])
    g_out[...] = g
    lw = -math.exp(-0.5) * _sigmoid(wpre)

    ones = _head_ones()
    kk = k * kk_ref[...]
    k2 = k * (1.0 + (a - 1.0) * ka_ref[...])
    kkn = []
    for p in range(d // LANES):
        kkp = kk[:, p * LANES:(p + 1) * LANES]
        kkn.append(kkp * lax.rsqrt(jnp.maximum(_head_sum_bf16(kkp * kkp, ones), L2_EPS * L2_EPS)))
    kkn = jnp.concatenate(kkn, axis=1)
    bb = kkn * a
    v = _dot(xv, wv_ref[...])
    if has_vres:
        vfirst = jnp.concatenate([vfirst_ref[p] for p in range(nslab)], axis=1)
        v = v + (vfirst - v) * gate
    r = _dot(mixed(0), wr_ref[...])
    for p in range(nslab):
        sl = slice(p * slab, (p + 1) * slab)
        r_out[p] = r[:, sl]
        lw_out[p] = lw[:, sl]
        k_out[p] = k2[:, sl]
        v_out[p] = v[:, sl]
        kk_out[p] = kkn[:, sl]
        b_out[p] = bb[:, sl]


def _pre_call(x, p, i, j, shift_states, vfirst, nb, tm):
    d = x.shape[-1]
    rows = x.shape[0] if x.ndim == 2 else x.shape[0] * x.shape[1]
    slab = _slab_lanes(d)
    nslab = d // slab
    has_vres = j > 0
    sm_spec = pl.BlockSpec((nslab, tm, slab), lambda i_: (0, i_, 0))
    ops = [(x, _x_spec(x, tm, nb)), _pick(shift_states, j), _pick(p['norm_mix'], i), _pick(p['mix'], j),
           _pick(p['w_rkv'], j, 0), _pick(p['w_rkv'], j, 1), _pick(p['w_rkv'], j, 2),
           _pick(p['w0'], j), _pick(p['w1'], j), _pick(p['w2'], j),
           _pick(p['a0'], j), _pick(p['a1'], j), _pick(p['a2'], j)]
    if has_vres:
        ops += [_pick(p['v0'], j - 1), _pick(p['v1'], j - 1), _pick(p['v2'], j - 1), (vfirst, sm_spec)]
    ops += [_pick(p['g1'], j), _pick(p['g2'], j), _pick(p['k_k'], j), _pick(p['k_a'], j)]
    sm = jax.ShapeDtypeStruct((nslab, rows, slab), F32)
    out_shape = [sm] * 6 + [jax.ShapeDtypeStruct((rows, d), F32), jax.ShapeDtypeStruct((nb, d), F32)]
    out_specs = [sm_spec] * 6 + [pl.BlockSpec((tm, d), lambda i_: (i_, 0)), _const_spec((nb, d))]
    return pl.pallas_call(
        functools.partial(_pre_kernel, has_vres, nb),
        grid=(rows // tm,),
        in_specs=[o[1] for o in ops],
        out_specs=out_specs,
        out_shape=out_shape,
        scratch_shapes=[pltpu.VMEM((nb, d), F32)],
        compiler_params=pltpu.CompilerParams(
            dimension_semantics=("arbitrary",), vmem_limit_bytes=VMEM_LIMIT),
        name="rwkv_pre",
    )(*[o[0] for o in ops])


_WKV_SLOTS = ("at", "rt", "u", "v", "bh", "kh", "x0", "y0")


def _wkv_kernel(tchunk, nchunks, has_alias, layer, batch_minor, r_ref, lw_ref, k_ref, v_ref, kk_ref, b_ref,
                s0_ref, rk_ref, gnw_ref, gnb_ref, *rest):
    y_ref, sout_ref, s_scr, *work = rest[1:] if has_alias else rest
    if batch_minor:
        t_scr, *work = work
    nseq = SUBLANES
    npl = s_scr.shape[1]
    cp = max(tchunk, 8)
    n = cp * nseq
    nv = tchunk * nseq
    grp = LANES // (2 * cp)
    nslot = len(_WKV_SLOTS)
    scr = [dict(zip(_WKV_SLOTS, work[q * nslot:(q + 1) * nslot])) for q in range(npl)]

    if batch_minor:
        group = pl.program_id(1)

        @pl.when((group == 0) & (pl.program_id(2) == 0))
        def _():
            def relayout(i, carry):
                for q in range(npl):
                    for e in range(SUBLANES):
                        v = i * SUBLANES + e
                        tile = jnp.concatenate([s0_ref[2 * q, v], s0_ref[2 * q + 1, v]], axis=0)
                        t_scr[q, pl.ds(pl.multiple_of(v * RELAYOUT_PITCH, SUBLANES), LANES), :] = tile.T
                return carry
            lax.fori_loop(0, HEAD_DIM // SUBLANES, relayout, 0)

        @pl.when(pl.program_id(2) == 0)
        def _():
            lane = lax.broadcasted_iota(jnp.int32, (HEAD_DIM, LANES), 1)
            for s in range(nseq):
                for q in range(npl):
                    side = t_scr[q, pl.ds(group * nseq + s, HEAD_DIM, stride=RELAYOUT_PITCH), :]
                    s_scr[s, q] = jnp.concatenate([jnp.where(lane < HEAD_DIM, side, 0.0),
                                                   jnp.where(lane >= HEAD_DIM, side, 0.0)], axis=0)
    else:
        @pl.when(pl.program_id(2) == 0)
        def _():
            zero = jnp.zeros((HEAD_DIM, HEAD_DIM), F32)
            for s in range(nseq):
                for q in range(npl):
                    top = jnp.concatenate([s0_ref[s, 2 * q], zero], axis=1)
                    bot = jnp.concatenate([zero, s0_ref[s, 2 * q + 1]], axis=1)
                    s_scr[s, q] = jnp.concatenate([top, bot], axis=0)

    ones = _head_ones()
    hr = lax.broadcasted_iota(jnp.int32, (LANES, LANES), 0) // HEAD_DIM
    hc = lax.broadcasted_iota(jnp.int32, (LANES, LANES), 1) // HEAD_DIM
    same_head16 = jnp.where(hr == hc, 1.0, 0.0).astype(BF16)
    lane_head = lax.broadcasted_iota(jnp.int32, (1, LANES), 1) // HEAD_DIM
    head_f32 = [jnp.where(lane_head == hh, 1.0, 0.0) for hh in range(2)]
    head_bf16 = [m.astype(BF16) for m in head_f32]
    ri = lax.broadcasted_iota(jnp.int32, (n, 2 * n), 0)
    ci = lax.broadcasted_iota(jnp.int32, (n, 2 * n), 1) % n
    same_seq = (ri % nseq) == (ci % nseq)
    tri_incl = jnp.where(same_seq & (ci // nseq <= ri // nseq), 1.0, 0.0)
    tri_strict = jnp.where(same_seq & (ci // nseq < ri // nseq), 1.0, 0.0)[:, :n]
    half = cp // 2 if cp >= 16 else cp
    nh = half * nseq
    if half < cp:
        rh = lax.broadcasted_iota(jnp.int32, (n, nh), 0) % nseq
        ch = lax.broadcasted_iota(jnp.int32, (n, nh), 1) % nseq
        same_seq_half = jnp.where(rh == ch, 1.0, 0.0).astype(BF16)

    def load(ref, c, q):
        val = ref[pl.ds(c * tchunk, tchunk), :, q * LANES:(q + 1) * LANES].reshape(nv, LANES)
        if n > nv:
            val = jnp.concatenate([val, jnp.zeros((n - nv, LANES), F32)], axis=0)
        return val

    def tile_rows(slab, reps):
        return slab if reps == 1 else jnp.concatenate([slab] * reps, axis=0)

    def seq_rows(ref, s):
        return ref[pl.ds(s, cp, stride=nseq), :]

    def phase_a(c, q, st):
        r = load(r_ref, c, q); lw = load(lw_ref, c, q); k = load(k_ref, c, q)
        v = load(v_ref, c, q); kk = load(kk_ref, c, q); b = load(b_ref, c, q)

        acc = jnp.zeros((nseq, LANES), F32)
        cums = []
        for t in range(cp):
            acc = acc + lw[t * nseq:(t + 1) * nseq]
            cums.append(acc)
        cum = jnp.concatenate(cums, axis=0)
        w_inc = jnp.exp(cum)
        w_inv = jnp.exp(-cum)
        w_exc = jnp.concatenate([jnp.ones((nseq, LANES), F32), w_inc[: n - nseq]], axis=0)
        w_tot = w_inc[n - nseq:]
        at = -kk * w_exc
        rt = r * w_inc
        kt = k * w_inv
        bt = b * w_inv
        w_tot_rows = tile_rows(w_tot, cp)
        st.update(r=r, k=k, v=v, at=at, rt=rt, w_tot=w_tot, bh=bt * w_tot_rows, kh=kt * w_tot_rows)

        at16 = at.astype(BF16)
        rt16 = rt.astype(BF16)
        kt16 = kt.astype(BF16)
        kb16 = jnp.concatenate([kt16, bt.astype(BF16)], axis=0)
        gram_a = _dot_nt(jnp.concatenate([at16 * head_bf16[0], at16 * head_bf16[1]], axis=0), kt16)
        gram_r = _dot_nt(jnp.concatenate([rt16 * head_bf16[0], rt16 * head_bf16[1]], axis=0), kb16)
        st["ga"] = jnp.concatenate([gram_a[:n] * tri_strict, gram_a[n:] * tri_strict], axis=1).astype(BF16)
        st["gr"] = jnp.concatenate([gram_r[:n] * tri_incl, gram_r[n:] * tri_incl], axis=1).astype(BF16)
        st["vm"] = jnp.concatenate([v * head_f32[0], v * head_f32[1]], axis=0)
        yield

        prods = []
        for t in range(1, cp):
            j0 = (t // half) * half
            if t > j0:
                prods.append(tile_rows(at[t * nseq:(t + 1) * nseq], t - j0) * bt[j0 * nseq:t * nseq])
        st["coef"] = _head_sum_bf16(jnp.concatenate(prods, axis=0), ones)
        if half < cp:
            lhs = jnp.concatenate([at16[nh:] * head_bf16[0], at16[nh:] * head_bf16[1]], axis=0)
            n21 = _dot_nt(lhs, bt.astype(BF16)[:nh]).astype(BF16) * same_seq_half
            st["n21"] = jnp.concatenate([n21[:nh], n21[nh:]], axis=1)

    def phase_b(c, q, st):
        w = scr[q]
        w["at"][...] = st["at"]
        w["rt"][...] = st["rt"]
        w["v"][...] = st["v"]
        w["bh"][...] = st["bh"]
        w["kh"][...] = st["kh"]
        for s in range(nseq):
            lhs = jnp.concatenate([seq_rows(w["at"], s), seq_rows(w["rt"], s)], axis=0).astype(BF16)
            out = _dot_nt(lhs, s_scr[s, q].astype(BF16) * same_head16)
            w["x0"][pl.ds(s, cp, stride=nseq), :] = out[:cp]
            w["y0"][pl.ds(s, cp, stride=nseq), :] = out[cp:]
        yield

        x = w["x0"][...] + _dot(st["ga"], st["vm"].astype(BF16))
        yield

        coef = st["coef"]
        us = []
        off = 0
        for t in range(cp):
            j0 = (t // half) * half
            if t == half:
                u1 = jnp.concatenate(us, axis=0).astype(BF16)
                x = jnp.concatenate([x[:nh], x[nh:] + _dot(
                    st["n21"], jnp.concatenate([u1 * head_bf16[0], u1 * head_bf16[1]], axis=0))], axis=0)
            u_t = x[t * nseq:(t + 1) * nseq]
            for j in range(j0, t):
                u_t = u_t + coef[off + (j - j0) * nseq:off + (j - j0 + 1) * nseq] * us[j]
            off += (t - j0) * nseq
            us.append(u_t)
        u = jnp.concatenate(us, axis=0)
        w["u"][...] = u
        yield

        vm = st["vm"]
        um = jnp.concatenate([u * head_f32[0], u * head_f32[1]], axis=0)
        vum = jnp.concatenate([vm[:n], um[:n], vm[n:], um[n:]], axis=0)
        st["y"] = w["y0"][...] + _dot(st["gr"], vum.astype(BF16))
        yield

        w_tot = st["w_tot"]
        zp = []
        for s in range(nseq):
            zp += [seq_rows(w["u"], s), seq_rows(w["v"], s)]
        z_t = jnp.concatenate(zp, axis=0).T.astype(BF16)
        for g0 in range(0, nseq, grp):
            cols = []
            for e in range(grp):
                gs = jnp.concatenate([seq_rows(w["bh"], g0 + e), seq_rows(w["kh"], g0 + e)],
                                     axis=0).astype(BF16)
                blk = [gs]
                if e > 0:
                    blk = [jnp.zeros((2 * cp * e, LANES), BF16)] + blk
                if e < grp - 1:
                    blk = blk + [jnp.zeros((2 * cp * (grp - 1 - e), LANES), BF16)]
                cols.append(jnp.concatenate(blk, axis=0))
            lhs = z_t[:, (g0 // grp) * LANES:(g0 // grp + 1) * LANES]
            delta = _dot(lhs, jnp.concatenate(cols, axis=1))
            for e in range(grp):
                s = g0 + e
                dl = delta[:, e * LANES:(e + 1) * LANES]
                s_scr[s, q] = s_scr[s, q] * w_tot[s:s + 1, :] + dl

    def phase_c(c, q, st):
        ql = slice(q * LANES, (q + 1) * LANES)
        yv, rv, kv, vv = st["y"][:nv], st["r"][:nv], st["k"][:nv], st["v"][:nv]
        hs = _head_sum_bf16(jnp.concatenate([yv, rv * kv * rk_ref[:, ql]], axis=0), ones)
        dlt = yv - hs[:nv] * (1.0 / HEAD_DIM)
        bonus = hs[nv:] * vv
        yield
        var = _head_sum_bf16(dlt * dlt, ones) * (1.0 / HEAD_DIM)
        yn = dlt * lax.rsqrt(var + GN_EPS) * gnw_ref[:, ql] + gnb_ref[:, ql]
        y_ref[pl.ds(c * tchunk, tchunk), :, ql] = (yn + bonus).reshape(tchunk, nseq, LANES)

    def advance(gens):
        return [next(g, "done") is None for g in gens]

    def drain(gens):
        while any(advance(gens)):
            pass

    def chunk_group(cs):
        sts = [[{} for _ in range(npl)] for _ in cs]
        ga = [[phase_a(c, q, sts[i][q]) for q in range(npl)] for i, c in enumerate(cs)]
        gb = [[phase_b(c, q, sts[i][q]) for q in range(npl)] for i, c in enumerate(cs)]
        gc = [[phase_c(c, q, sts[i][q]) for q in range(npl)] for i, c in enumerate(cs)]
        drain(ga[0])
        for i in range(len(cs)):
            fill = []
            if i + 1 < len(cs):
                fill.append(ga[i + 1])
            if i > 0:
                fill.append(gc[i - 1])
            while any(advance(gb[i])):
                for f in fill:
                    advance(f)
            for f in fill:
                drain(f)
        drain(gc[-1])

    unroll = 2 if nchunks % 2 == 0 else 1

    def chunk(i, carry):
        chunk_group([i * unroll + e for e in range(unroll)])
        return carry

    lax.fori_loop(0, nchunks // unroll, chunk, 0)

    @pl.when(pl.program_id(2) == pl.num_programs(2) - 1)
    def _():
        out = sout_ref if has_alias else sout_ref.at[layer]
        if not has_alias:
            for other in range(sout_ref.shape[0]):
                if other != layer:
                    sout_ref[other] = jnp.zeros(sout_ref.shape[1:], F32)
        for s in range(nseq):
            for q in range(npl):
                out[s, 2 * q] = s_scr[s, q, :HEAD_DIM, :HEAD_DIM]
                out[s, 2 * q + 1] = s_scr[s, q, HEAD_DIM:, HEAD_DIM:]


def _wkv_call(r, lw, k, v, kk, b, states, new_states, layer, p, nb, nt):
    nslab, _, slab = r.shape
    npl = slab // LANES
    ngroup = nb // SUBLANES
    tchunk = min(WKV_CHUNK, nt)
    tb = min(WKV_BLOCK_ELEMS // (SUBLANES * slab), nt)
    nchunks = tb // tchunk
    assert nt % tb == 0 and tb % tchunk == 0 and nb % SUBLANES == 0
    view = lambda a: a.reshape(nslab, nt, ngroup, SUBLANES, slab)
    hps = 2 * npl
    batch_minor = nb == LANES
    if batch_minor:
        order = lambda f: (lambda p_, g, t: f(g, p_, t))
        grid = (nslab, ngroup, nt // tb)
        states_in = states.transpose(0, 2, 3, 4, 1)
        st_in_spec = pl.BlockSpec((None, hps, HEAD_DIM, HEAD_DIM, nb), lambda p_, g, t: (layer, p_, 0, 0, 0),
                                  pipeline_mode=pl.Buffered(1))
        relayout_scr = [pltpu.VMEM((npl, HEAD_DIM * RELAYOUT_PITCH, LANES), F32)]
    else:
        order = lambda f: f
        grid = (ngroup, nslab, nt // tb)
        states_in = states
        st_in_spec = pl.BlockSpec((None, SUBLANES, hps, HEAD_DIM, HEAD_DIM),
                                  lambda g, p_, t: (layer, g, p_, 0, 0))
        relayout_scr = []
    act_spec = pl.BlockSpec((None, tb, None, SUBLANES, slab), order(lambda g, p_, t: (p_, t, g, 0, 0)))
    has_alias = new_states is not None
    if has_alias:
        st_out_spec = pl.BlockSpec((None, SUBLANES, hps, HEAD_DIM, HEAD_DIM),
                                   order(lambda g, p_, t: (layer, g, p_, 0, 0)))
    else:
        st_out_spec = pl.BlockSpec((states.shape[0], SUBLANES, hps, HEAD_DIM, HEAD_DIM),
                                   order(lambda g, p_, t: (0, g, p_, 0, 0)))
    extra_in = [new_states] if has_alias else []
    extra_spec = [pl.BlockSpec(memory_space=pl.ANY)] if has_alias else []
    par_spec = pl.BlockSpec((None, 1, slab), order(lambda g, p_, t: (layer, 0, p_)))
    n = max(tchunk, 8) * SUBLANES
    y, s_out = pl.pallas_call(
        functools.partial(_wkv_kernel, tchunk, nchunks, has_alias, layer, batch_minor),
        grid=grid,
        in_specs=[act_spec] * 6 + [st_in_spec] + [par_spec] * 3 + extra_spec,
        out_specs=[act_spec, st_out_spec],
        out_shape=[jax.ShapeDtypeStruct((nslab, nt, ngroup, SUBLANES, slab), F32),
                   jax.ShapeDtypeStruct(states.shape, F32)],
        input_output_aliases={10: 1} if has_alias else {},
        scratch_shapes=[pltpu.VMEM((SUBLANES, npl, LANES, LANES), F32)] + relayout_scr
        + [pltpu.VMEM((n, LANES), F32)] * (len(_WKV_SLOTS) * npl),
        compiler_params=pltpu.CompilerParams(
            dimension_semantics=("arbitrary", "arbitrary", "arbitrary"),
            vmem_limit_bytes=VMEM_LIMIT),
        name="wkv7",
    )(view(r), view(lw), view(k), view(v), view(kk), view(b), states_in, p['r_k'], p['gn_w'], p['gn_b'],
      *extra_in)
    return y.reshape(nslab, nt * nb, slab), s_out


def _ffn_stages(x, gffn, w1_ref, w2_ref, out):
    h = _rmsnorm(x, gffn).astype(BF16)
    dff = w1_ref.shape[1]
    step = min(dff, FFN_CHUNK)
    acc = x
    for c in range(dff // step):
        hh = _dot(h, w1_ref[:, c * step:(c + 1) * step])
        yield
        hh = jnp.square(jnp.maximum(hh, 0.0)).astype(BF16)
        acc = acc + _dot(hh, w2_ref[c * step:(c + 1) * step, :])
        yield
    out.append(acc)


def _ffn(x, gffn, w1_ref, w2_ref):
    out = []
    for _ in _ffn_stages(x, gffn, w1_ref, w2_ref, out):
        pass
    return out[0]


def _attn_ffn_kernel(final_norm, nb, x_ref, y_ref, g_ref, wo_ref, gffn_ref, w1_ref, w2_ref, gfin_ref,
                     o_ref):
    nslab = y_ref.shape[0]
    y = jnp.concatenate([y_ref[q] for q in range(nslab)], axis=1)
    x = _rows_in(x_ref, nb) + _dot((y * g_ref[...]).astype(BF16), wo_ref[...])
    out = _ffn(x, gffn_ref[...], w1_ref, w2_ref)
    if final_norm:
        out = _rmsnorm(out, gfin_ref[...])
    _rows_out(o_ref, out, nb)


def _out_x(rows, d, tm, nb, batch_major_out):
    if batch_major_out:
        return (jax.ShapeDtypeStruct((nb, rows // nb, d), F32),
                pl.BlockSpec((nb, tm // nb, d), lambda i: (0, i, 0)))
    return jax.ShapeDtypeStruct((rows, d), F32), pl.BlockSpec((tm, d), lambda i: (i, 0))


def _attn_ffn_call(x, y, g, p, i, j, final_norm, batch_major_out, nb, tm):
    rows, d = g.shape
    nslab, _, slab = y.shape
    out_sds, out_spec = _out_x(rows, d, tm, nb, batch_major_out)
    ops = [(x, _x_spec(x, tm, nb)), (y, pl.BlockSpec((nslab, tm, slab), lambda i_: (0, i_, 0))),
           (g, pl.BlockSpec((tm, d), lambda i_: (i_, 0))), _pick(p['w_o'], j), _pick(p['norm_ffn'], i),
           _pick(p['ffn_w1'], i), _pick(p['ffn_w2'], i), _pick(p['norm_final'])]
    return pl.pallas_call(
        functools.partial(_attn_ffn_kernel, final_norm, nb),
        grid=(rows // tm,),
        in_specs=[o[1] for o in ops],
        out_specs=out_spec,
        out_shape=out_sds,
        compiler_params=pltpu.CompilerParams(
            dimension_semantics=("arbitrary",), vmem_limit_bytes=VMEM_LIMIT),
        name="attn_out_ffn",
    )(*[o[0] for o in ops])


def _pool_mix_stages(x, prev, first_row, nb, start_pos, gmix_ref, pw_ref, ps_ref, out):
    tm, d = x.shape
    gw = d // len(POOL_WINDOWS)
    h = _rmsnorm(x, gmix_ref[...])
    ext = jnp.concatenate([prev, h], axis=0)
    rowi = lax.broadcasted_iota(jnp.int32, (tm, gw), 0) + first_row
    pos = start_pos + jnp.right_shift(rowi, int(math.log2(nb)))
    yield
    ys = []
    for gi, w in enumerate(POOL_WINDOWS):
        sl = slice(gi * gw, (gi + 1) * gw)
        s = ext[:, sl]
        span = 1
        while span < w:
            s = s[span * nb:] + s[: s.shape[0] - span * nb]
            span *= 2
        s = s[s.shape[0] - tm:]
        cnt = jnp.minimum(pos + 1, w).astype(F32)
        diff = (s / cnt - h[:, sl]).astype(BF16)
        ys.append(_dot(diff, pw_ref[gi]))
        yield
    out += [x + jnp.concatenate(ys, axis=1) * ps_ref[...], ext[tm:]]


def _pool_mix(*args, **kwargs):
    out = []
    for _ in _pool_mix_stages(*args, out=out, **kwargs):
        pass
    return out


def _pool_ffn_kernel(final_norm, nb, start_pos, lookahead, *refs):
    if lookahead:
        x_ref, xn_ref, *refs = refs
    else:
        x_ref, *refs = refs
    buf_ref, gmix_ref, pw_ref, ps_ref, gffn_ref, w1_ref, w2_ref, gfin_ref, o_ref, buf_out, carry, *scr = refs
    tm = x_ref.shape[0] if len(x_ref.shape) == 2 else nb * x_ref.shape[1]
    step = pl.program_id(0)
    mix = functools.partial(_pool_mix, nb=nb, start_pos=start_pos, gmix_ref=gmix_ref, pw_ref=pw_ref,
                            ps_ref=ps_ref)

    if lookahead:
        xm, = scr

        @pl.when(step == 0)
        def _():
            x0, tail0 = mix(_rows_in(x_ref, nb), buf_ref[...], 0)
            xm[0] = x0
            carry[...] = tail0

        res, nxt = [], []
        ffn = _ffn_stages(xm[step % 2], gffn_ref[...], w1_ref, w2_ref, res)
        ahead = _pool_mix_stages(_rows_in(xn_ref, nb), carry[...], (step + 1) * tm, nb, start_pos,
                                 gmix_ref, pw_ref, ps_ref, nxt)
        busy = True
        while busy:
            busy = next(ffn, "done") is None
            busy = (next(ahead, "done") is None) or busy
        out = res[0]
        xn, tail = nxt
    else:
        x, tail = mix(_rows_in(x_ref, nb), buf_ref[...], 0)
        buf_out[...] = tail
        out = _ffn(x, gffn_ref[...], w1_ref, w2_ref)
    if final_norm:
        out = _rmsnorm(out, gfin_ref[...])
    _rows_out(o_ref, out, nb)
    if lookahead:
        xm[(step + 1) % 2] = xn
        carry[...] = jnp.where(step < pl.num_programs(0) - 1, tail, carry[...])
        buf_out[...] = carry[...]


def _pool_ffn_call(x, buf, p, i, j, final_norm, batch_major_out, nb, start_pos, tm):
    d = x.shape[-1]
    rows = x.shape[0] if x.ndim == 2 else x.shape[0] * x.shape[1]
    assert nb & (nb - 1) == 0
    nsteps = rows // tm
    lookahead = nsteps > 1
    out_sds, out_spec = _out_x(rows, d, tm, nb, batch_major_out)
    ops = [(x, _x_spec(x, tm, nb))]
    if lookahead:
        ahead = lambda i_: jnp.minimum(i_ + 1, nsteps - 1)
        if x.ndim == 2:
            ops.append((x, pl.BlockSpec((tm, d), lambda i_: (ahead(i_), 0))))
        else:
            ops.append((x, pl.BlockSpec((nb, tm // nb, d), lambda i_: (0, ahead(i_), 0))))
    ops += [(buf, _const_spec(buf.shape)), _pick(p['norm_mix'], i),
            _pick(p['pool_w'], j), _pick(p['pool_scale'], j), _pick(p['norm_ffn'], i),
            _pick(p['ffn_w1'], i), _pick(p['ffn_w2'], i), _pick(p['norm_final'])]
    return pl.pallas_call(
        functools.partial(_pool_ffn_kernel, final_norm, nb, start_pos, lookahead),
        grid=(nsteps,),
        in_specs=[o[1] for o in ops],
        out_specs=[out_spec, _const_spec(buf.shape)],
        out_shape=[out_sds, jax.ShapeDtypeStruct(buf.shape, F32)],
        scratch_shapes=[pltpu.VMEM(buf.shape, F32)] + ([pltpu.VMEM((2, tm, d), F32)] if lookahead else []),
        compiler_params=pltpu.CompilerParams(
            dimension_semantics=("arbitrary",), vmem_limit_bytes=VMEM_LIMIT),
        name="pool_ffn",
    )(*[o[0] for o in ops])


def _trunk(x, start_pos, shift_states, wkv_states, pool_bufs, p):
    nb, nt, d = x.shape
    depth = p['norm_mix'].shape[0]
    rows = nb * nt
    tm = min(ROW_BLOCK, rows)
    assert rows % tm == 0 and tm % nb == 0
    batch_major = nb == SUBLANES
    xt = x if batch_major else x.transpose(1, 0, 2).reshape(rows, d)
    new_shift, new_wkv, new_pool = [], None, []
    v_first = None
    for i in range(depth):
        j = i // 2
        last = i == depth - 1
        bm_out = batch_major and last
        if i % 2 == 0:
            r, lw, k, v, kk, b, g, s_shift = _pre_call(xt, p, i, j, shift_states, v_first, nb, tm)
            if j == 0:
                v_first = v
            y, new_wkv = _wkv_call(r, lw, k, v, kk, b, wkv_states, new_wkv, j, p, nb, nt)
            new_shift.append(s_shift)
            xt = _attn_ffn_call(xt, y, g, p, i, j, last, bm_out, nb, tm)
        else:
            buf = pool_bufs[j].transpose(1, 0, 2).reshape(POOL_BUF * nb, d)
            xt, s_buf = _pool_ffn_call(xt, buf, p, i, j, last, bm_out, nb, start_pos, tm)
            new_pool.append(s_buf.reshape(POOL_BUF, nb, d).transpose(1, 0, 2))
    y = xt if batch_major else xt.reshape(nt, nb, d).transpose(1, 0, 2)
    return y, new_wkv, jnp.stack(new_shift), jnp.stack(new_pool)


def kernel(x_prompt, x_sample, state_wkv, state_shift, state_pool, norm_mix, norm_ffn, norm_final,
           rwkv_mix, rwkv_w_rkv, rwkv_w_o, rwkv_w0, rwkv_w1, rwkv_w2, rwkv_a0, rwkv_a1, rwkv_a2,
           rwkv_v0, rwkv_v1, rwkv_v2, rwkv_g1, rwkv_g2, rwkv_k_k, rwkv_k_a, rwkv_r_k,
           rwkv_gn_w, rwkv_gn_b, pool_w, pool_scale, ffn_w1, ffn_w2):
    bf = lambda a: a.astype(BF16)
    vec = lambda a: a.reshape(a.shape[0], 1, -1)
    p = {'norm_mix': vec(norm_mix), 'norm_ffn': vec(norm_ffn), 'norm_final': norm_final.reshape(1, -1),
         'mix': rwkv_mix, 'w_rkv': bf(rwkv_w_rkv), 'w_o': bf(rwkv_w_o), 'w0': vec(rwkv_w0),
         'w1': bf(rwkv_w1), 'w2': bf(rwkv_w2), 'a0': vec(rwkv_a0), 'a1': bf(rwkv_a1),
         'a2': bf(rwkv_a2), 'v0': vec(rwkv_v0), 'v1': bf(rwkv_v1), 'v2': bf(rwkv_v2),
         'g1': bf(rwkv_g1), 'g2': bf(rwkv_g2), 'k_k': vec(rwkv_k_k), 'k_a': vec(rwkv_k_a),
         'r_k': vec(rwkv_r_k), 'gn_w': vec(rwkv_gn_w), 'gn_b': vec(rwkv_gn_b), 'pool_w': bf(pool_w),
         'pool_scale': vec(pool_scale), 'ffn_w1': bf(ffn_w1), 'ffn_w2': bf(ffn_w2)}
    dt = x_prompt.dtype
    nb, _, d = x_prompt.shape
    n_rwkv = state_wkv.shape[0]
    n_pool = state_pool.shape[0]
    nh = d // HEAD_DIM
    z_shift = jnp.zeros((n_rwkv, nb, d), dt)
    z_wkv = jnp.zeros((n_rwkv, nb, nh, HEAD_DIM, HEAD_DIM), dt)
    z_pool = jnp.zeros((n_pool, nb, POOL_BUF, d), dt)
    y_p, wkv_p, shift_p, pool_p = _trunk(x_prompt, 0, z_shift, z_wkv, z_pool, p)
    y_s, wkv_s, shift_s, pool_s = _trunk(x_sample, PAST_LEN, state_shift, state_wkv, state_pool, p)
    return (y_p, y_s, wkv_p, shift_p, pool_p, wkv_s, shift_s, pool_s)
```

```python
import functools
import math

import jax
import jax.numpy as jnp
from jax import lax
from jax.experimental import pallas as pl
from jax.experimental.pallas import tpu as pltpu

HEAD_DIM = 64
LANES = 128
SUBLANES = 8
WKV_LANES = 512
WKV_BLOCK_ELEMS = 8 * 128 * 512
POOL_WINDOWS = (2, 4, 8, 16)
POOL_BUF = max(POOL_WINDOWS) - 1
PAST_LEN = 16384
NORM_EPS = 1e-6
GN_EPS = 64e-5
L2_EPS = 1e-12
ROW_BLOCK = 512
FFN_CHUNK = 1024
WKV_CHUNK = 16
RELAYOUT_PITCH = 136
VMEM_LIMIT = 56 * 1024 * 1024

BF16 = jnp.bfloat16
F32 = jnp.float32


def _dot(a, b):
    return jnp.dot(a, b, preferred_element_type=F32)


def _dot_nt(a, b):
    return lax.dot_general(a, b, (((1,), (1,)), ((), ())), preferred_element_type=F32)


def _sigmoid(x):
    return 0.5 * jnp.tanh(0.5 * x) + 0.5


def _rmsnorm(x, g):
    ms = jnp.mean(x * x, axis=-1, keepdims=True)
    return x * lax.rsqrt(ms + NORM_EPS) * g


def _head_ones():
    r = lax.broadcasted_iota(jnp.int32, (LANES, LANES), 0) // HEAD_DIM
    c = lax.broadcasted_iota(jnp.int32, (LANES, LANES), 1) // HEAD_DIM
    return jnp.where(r == c, 1.0, 0.0).astype(BF16)


def _head_sum_bf16(p, ones):
    return _dot(p.astype(BF16), ones)


def _const_spec(shape):
    n = len(shape)
    return pl.BlockSpec(shape, lambda *_: (0,) * n)


def _pick(arr, *idx):
    shape = (None,) * len(idx) + arr.shape[len(idx):]
    at = tuple(idx) + (0,) * (arr.ndim - len(idx))
    return arr, pl.BlockSpec(shape, lambda *_: at, pipeline_mode=pl.Buffered(1))


def _rows_in(x_ref, nb):
    if len(x_ref.shape) == 2:
        return x_ref[...]
    return jnp.swapaxes(x_ref[...], 0, 1).reshape(nb * x_ref.shape[1], x_ref.shape[2])


def _rows_out(o_ref, val, nb):
    if len(o_ref.shape) == 2:
        o_ref[...] = val
    else:
        o_ref[...] = jnp.swapaxes(val.reshape(o_ref.shape[1], nb, val.shape[1]), 0, 1)


def _x_spec(x, tm, nb):
    if x.ndim == 2:
        return pl.BlockSpec((tm, x.shape[1]), lambda i: (i, 0))
    return pl.BlockSpec((nb, tm // nb, x.shape[2]), lambda i: (0, i, 0))


def _slab_lanes(d):
    return WKV_LANES if d % WKV_LANES == 0 else LANES


def _pre_kernel(has_vres, nb, *refs):
    it = iter(refs)
    x_ref = next(it); shift_ref = next(it); gmix_ref = next(it); mix_ref = next(it)
    wr_ref = next(it); wk_ref = next(it); wv_ref = next(it)
    w0_ref = next(it); w1_ref = next(it); w2_ref = next(it)
    a0_ref = next(it); a1_ref = next(it); a2_ref = next(it)
    if has_vres:
        v0_ref = next(it); v1_ref = next(it); v2_ref = next(it); vfirst_ref = next(it)
    g1_ref = next(it); g2_ref = next(it); kk_ref = next(it); ka_ref = next(it)
    r_out = next(it); lw_out = next(it); k_out = next(it); v_out = next(it)
    kk_out = next(it); b_out = next(it); g_out = next(it); shift_out = next(it)
    carry = next(it)

    tm, d = g_out.shape
    nslab, _, slab = r_out.shape

    @pl.when(pl.program_id(0) == 0)
    def _():
        carry[...] = shift_ref[...]

    h = _rmsnorm(_rows_in(x_ref, nb), gmix_ref[...])
    if tm > nb:
        hp = jnp.concatenate([carry[...], h[: tm - nb]], axis=0)
    else:
        hp = carry[...]
    carry[...] = h[tm - nb:]
    shift_out[...] = h[tm - nb:]
    dx = hp - h

    def mixed(i):
        return (h + dx * mix_ref[i:i + 1, :]).astype(BF16)

    xv = mixed(2)
    lora_w = _dot(mixed(3), w1_ref[...])
    lora_a = _dot(mixed(4), a1_ref[...])
    if has_vres:
        lora_v = _dot(xv, v1_ref[...])
    lora_g = _dot(mixed(5), g1_ref[...])
    k = _dot(mixed(1), wk_ref[...])
    wpre = w0_ref[...] + _dot(jnp.tanh(lora_w).astype(BF16), w2_ref[...])
    a = _sigmoid(a0_ref[...] + _dot(lora_a.astype(BF16), a2_ref[...]))
    if has_vres:
        gate = _sigmoid(v0_ref[...] + _dot(lora_v.astype(BF16), v2_ref[...]))
    g = _dot(_sigmoid(lora_g).astype(BF16), g2_ref[...])
    g_out[...] = g
    lw = -math.exp(-0.5) * _sigmoid(wpre)

    ones = _head_ones()
    kk = k * kk_ref[...]
    k2 = k * (1.0 + (a - 1.0) * ka_ref[...])
    kkn = []
    for p in range(d // LANES):
        kkp = kk[:, p * LANES:(p + 1) * LANES]
        kkn.append(kkp * lax.rsqrt(jnp.maximum(_head_sum_bf16(kkp * kkp, ones), L2_EPS * L2_EPS)))
    kkn = jnp.concatenate(kkn, axis=1)
    bb = kkn * a
    v = _dot(xv, wv_ref[...])
    if has_vres:
        vfirst = jnp.concatenate([vfirst_ref[p] for p in range(nslab)], axis=1)
        v = v + (vfirst - v) * gate
    r = _dot(mixed(0), wr_ref[...])
    for p in range(nslab):
        sl = slice(p * slab, (p + 1) * slab)
        r_out[p] = r[:, sl]
        lw_out[p] = lw[:, sl]
        k_out[p] = k2[:, sl]
        v_out[p] = v[:, sl]
        kk_out[p] = kkn[:, sl]
        b_out[p] = bb[:, sl]


def _pre_call(x, p, i, j, shift_states, vfirst, nb, tm):
    d = x.shape[-1]
    rows = x.shape[0] if x.ndim == 2 else x.shape[0] * x.shape[1]
    slab = _slab_lanes(d)
    nslab = d // slab
    has_vres = j > 0
    sm_spec = pl.BlockSpec((nslab, tm, slab), lambda i_: (0, i_, 0))
    ops = [(x, _x_spec(x, tm, nb)), _pick(shift_states, j), _pick(p['norm_mix'], i), _pick(p['mix'], j),
           _pick(p['w_rkv'], j, 0), _pick(p['w_rkv'], j, 1), _pick(p['w_rkv'], j, 2),
           _pick(p['w0'], j), _pick(p['w1'], j), _pick(p['w2'], j),
           _pick(p['a0'], j), _pick(p['a1'], j), _pick(p['a2'], j)]
    if has_vres:
        ops += [_pick(p['v0'], j - 1), _pick(p['v1'], j - 1), _pick(p['v2'], j - 1), (vfirst, sm_spec)]
    ops += [_pick(p['g1'], j), _pick(p['g2'], j), _pick(p['k_k'], j), _pick(p['k_a'], j)]
    sm = jax.ShapeDtypeStruct((nslab, rows, slab), F32)
    out_shape = [sm] * 6 + [jax.ShapeDtypeStruct((rows, d), F32), jax.ShapeDtypeStruct((nb, d), F32)]
    out_specs = [sm_spec] * 6 + [pl.BlockSpec((tm, d), lambda i_: (i_, 0)), _const_spec((nb, d))]
    return pl.pallas_call(
        functools.partial(_pre_kernel, has_vres, nb),
        grid=(rows // tm,),
        in_specs=[o[1] for o in ops],
        out_specs=out_specs,
        out_shape=out_shape,
        scratch_shapes=[pltpu.VMEM((nb, d), F32)],
        compiler_params=pltpu.CompilerParams(
            dimension_semantics=("arbitrary",), vmem_limit_bytes=VMEM_LIMIT),
        name="rwkv_pre",
    )(*[o[0] for o in ops])


_WKV_SLOTS = ("at", "rt", "u", "v", "bh", "kh", "x0", "y0")


def _wkv_kernel(tchunk, nchunks, has_alias, layer, batch_minor, r_ref, lw_ref, k_ref, v_ref, kk_ref, b_ref,
                s0_ref, rk_ref, gnw_ref, gnb_ref, *rest):
    y_ref, sout_ref, s_scr, *work = rest[1:] if has_alias else rest
    if batch_minor:
        t_scr, *work = work
    nseq = SUBLANES
    npl = s_scr.shape[1]
    cp = max(tchunk, 8)
    n = cp * nseq
    nv = tchunk * nseq
    grp = LANES // (2 * cp)
    nslot = len(_WKV_SLOTS)
    scr = [dict(zip(_WKV_SLOTS, work[q * nslot:(q + 1) * nslot])) for q in range(npl)]

    if batch_minor:
        group = pl.program_id(1)

        @pl.when((group == 0) & (pl.program_id(2) == 0))
        def _():
            def relayout(i, carry):
                for q in range(npl):
                    for e in range(SUBLANES):
                        v = i * SUBLANES + e
                        tile = jnp.concatenate([s0_ref[2 * q, v], s0_ref[2 * q + 1, v]], axis=0)
                        t_scr[q, pl.ds(pl.multiple_of(v * RELAYOUT_PITCH, SUBLANES), LANES), :] = tile.T
                return carry
            lax.fori_loop(0, HEAD_DIM // SUBLANES, relayout, 0)

        @pl.when(pl.program_id(2) == 0)
        def _():
            lane = lax.broadcasted_iota(jnp.int32, (HEAD_DIM, LANES), 1)
            for s in range(nseq):
                for q in range(npl):
                    side = t_scr[q, pl.ds(group * nseq + s, HEAD_DIM, stride=RELAYOUT_PITCH), :]
                    s_scr[s, q] = jnp.concatenate([jnp.where(lane < HEAD_DIM, side, 0.0),
                                                   jnp.where(lane >= HEAD_DIM, side, 0.0)], axis=0)
    else:
        @pl.when(pl.program_id(2) == 0)
        def _():
            zero = jnp.zeros((HEAD_DIM, HEAD_DIM), F32)
            for s in range(nseq):
                for q in range(npl):
                    top = jnp.concatenate([s0_ref[s, 2 * q], zero], axis=1)
                    bot = jnp.concatenate([zero, s0_ref[s, 2 * q + 1]], axis=1)
                    s_scr[s, q] = jnp.concatenate([top, bot], axis=0)

    ones = _head_ones()
    hr = lax.broadcasted_iota(jnp.int32, (LANES, LANES), 0) // HEAD_DIM
    hc = lax.broadcasted_iota(jnp.int32, (LANES, LANES), 1) // HEAD_DIM
    same_head16 = jnp.where(hr == hc, 1.0, 0.0).astype(BF16)
    lane_head = lax.broadcasted_iota(jnp.int32, (1, LANES), 1) // HEAD_DIM
    head_f32 = [jnp.where(lane_head == hh, 1.0, 0.0) for hh in range(2)]
    head_bf16 = [m.astype(BF16) for m in head_f32]
    ri = lax.broadcasted_iota(jnp.int32, (n, 2 * n), 0)
    ci = lax.broadcasted_iota(jnp.int32, (n, 2 * n), 1) % n
    same_seq = (ri % nseq) == (ci % nseq)
    tri_incl = jnp.where(same_seq & (ci // nseq <= ri // nseq), 1.0, 0.0)
    tri_strict = jnp.where(same_seq & (ci // nseq < ri // nseq), 1.0, 0.0)[:, :n]
    half = cp // 2 if cp >= 16 else cp
    nh = half * nseq
    if half < cp:
        rh = lax.broadcasted_iota(jnp.int32, (n, nh), 0) % nseq
        ch = lax.broadcasted_iota(jnp.int32, (n, nh), 1) % nseq
        same_seq_half = jnp.where(rh == ch, 1.0, 0.0).astype(BF16)

    def load(ref, c, q):
        val = ref[pl.ds(c * tchunk, tchunk), :, q * LANES:(q + 1) * LANES].reshape(nv, LANES)
        if n > nv:
            val = jnp.concatenate([val, jnp.zeros((n - nv, LANES), F32)], axis=0)
        return val

    def tile_rows(slab, reps):
        return slab if reps == 1 else jnp.concatenate([slab] * reps, axis=0)

    def seq_rows(ref, s):
        return ref[pl.ds(s, cp, stride=nseq), :]

    def phase_a(c, q, st):
        r = load(r_ref, c, q); lw = load(lw_ref, c, q); k = load(k_ref, c, q)
        v = load(v_ref, c, q); kk = load(kk_ref, c, q); b = load(b_ref, c, q)

        acc = jnp.zeros((nseq, LANES), F32)
        cums = []
        for t in range(cp):
            acc = acc + lw[t * nseq:(t + 1) * nseq]
            cums.append(acc)
        cum = jnp.concatenate(cums, axis=0)
        w_inc = jnp.exp(cum)
        w_inv = jnp.exp(-cum)
        w_exc = jnp.concatenate([jnp.ones((nseq, LANES), F32), w_inc[: n - nseq]], axis=0)
        w_tot = w_inc[n - nseq:]
        at = -kk * w_exc
        rt = r * w_inc
        kt = k * w_inv
        bt = b * w_inv
        w_tot_rows = tile_rows(w_tot, cp)
        st.update(r=r, k=k, v=v, at=at, rt=rt, w_tot=w_tot, bh=bt * w_tot_rows, kh=kt * w_tot_rows)

        at16 = at.astype(BF16)
        rt16 = rt.astype(BF16)
        kt16 = kt.astype(BF16)
        kb16 = jnp.concatenate([kt16, bt.astype(BF16)], axis=0)
        gram_a = _dot_nt(jnp.concatenate([at16 * head_bf16[0], at16 * head_bf16[1]], axis=0), kt16)
        gram_r = _dot_nt(jnp.concatenate([rt16 * head_bf16[0], rt16 * head_bf16[1]], axis=0), kb16)
        st["ga"] = jnp.concatenate([gram_a[:n] * tri_strict, gram_a[n:] * tri_strict], axis=1).astype(BF16)
        st["gr"] = jnp.concatenate([gram_r[:n] * tri_incl, gram_r[n:] * tri_incl], axis=1).astype(BF16)
        st["vm"] = jnp.concatenate([v * head_f32[0], v * head_f32[1]], axis=0)
        yield

        prods = []
        for t in range(1, cp):
            j0 = (t // half) * half
            if t > j0:
                prods.append(tile_rows(at[t * nseq:(t + 1) * nseq], t - j0) * bt[j0 * nseq:t * nseq])
        st["coef"] = _head_sum_bf16(jnp.concatenate(prods, axis=0), ones)
        if half < cp:
            lhs = jnp.concatenate([at16[nh:] * head_bf16[0], at16[nh:] * head_bf16[1]], axis=0)
            n21 = _dot_nt(lhs, bt.astype(BF16)[:nh]).astype(BF16) * same_seq_half
            st["n21"] = jnp.concatenate([n21[:nh], n21[nh:]], axis=1)

    def phase_b(c, q, st):
        w = scr[q]
        w["at"][...] = st["at"]
        w["rt"][...] = st["rt"]
        w["v"][...] = st["v"]
        w["bh"][...] = st["bh"]
        w["kh"][...] = st["kh"]
        for s in range(nseq):
            lhs = jnp.concatenate([seq_rows(w["at"], s), seq_rows(w["rt"], s)], axis=0).astype(BF16)
            out = _dot_nt(lhs, s_scr[s, q].astype(BF16) * same_head16)
            w["x0"][pl.ds(s, cp, stride=nseq), :] = out[:cp]
            w["y0"][pl.ds(s, cp, stride=nseq), :] = out[cp:]
        yield

        x = w["x0"][...] + _dot(st["ga"], st["vm"].astype(BF16))
        yield

        coef = st["coef"]
        us = []
        off = 0
        for t in range(cp):
            j0 = (t // half) * half
            if t == half:
                u1 = jnp.concatenate(us, axis=0).astype(BF16)
                x = jnp.concatenate([x[:nh], x[nh:] + _dot(
                    st["n21"], jnp.concatenate([u1 * head_bf16[0], u1 * head_bf16[1]], axis=0))], axis=0)
            u_t = x[t * nseq:(t + 1) * nseq]
            for j in range(j0, t):
                u_t = u_t + coef[off + (j - j0) * nseq:off + (j - j0 + 1) * nseq] * us[j]
            off += (t - j0) * nseq
            us.append(u_t)
        u = jnp.concatenate(us, axis=0)
        w["u"][...] = u
        yield

        vm = st["vm"]
        um = jnp.concatenate([u * head_f32[0], u * head_f32[1]], axis=0)
        vum = jnp.concatenate([vm[:n], um[:n], vm[n:], um[n:]], axis=0)
        st["y"] = w["y0"][...] + _dot(st["gr"], vum.astype(BF16))
        yield

        w_tot = st["w_tot"]
        zp = []
        for s in range(nseq):
            zp += [seq_rows(w["u"], s), seq_rows(w["v"], s)]
        z_t = jnp.concatenate(zp, axis=0).T.astype(BF16)
        for g0 in range(0, nseq, grp):
            cols = []
            for e in range(grp):
                gs = jnp.concatenate([seq_rows(w["bh"], g0 + e), seq_rows(w["kh"], g0 + e)],
                                     axis=0).astype(BF16)
                blk = [gs]
                if e > 0:
                    blk = [jnp.zeros((2 * cp * e, LANES), BF16)] + blk
                if e < grp - 1:
                    blk = blk + [jnp.zeros((2 * cp * (grp - 1 - e), LANES), BF16)]
                cols.append(jnp.concatenate(blk, axis=0))
            lhs = z_t[:, (g0 // grp) * LANES:(g0 // grp + 1) * LANES]
            delta = _dot(lhs, jnp.concatenate(cols, axis=1))
            for e in range(grp):
                s = g0 + e
                dl = delta[:, e * LANES:(e + 1) * LANES]
                s_scr[s, q] = s_scr[s, q] * w_tot[s:s + 1, :] + dl

    def phase_c(c, q, st):
        ql = slice(q * LANES, (q + 1) * LANES)
        yv, rv, kv, vv = st["y"][:nv], st["r"][:nv], st["k"][:nv], st["v"][:nv]
        hs = _head_sum_bf16(jnp.concatenate([yv, rv * kv * rk_ref[:, ql]], axis=0), ones)
        dlt = yv - hs[:nv] * (1.0 / HEAD_DIM)
        bonus = hs[nv:] * vv
        yield
        var = _head_sum_bf16(dlt * dlt, ones) * (1.0 / HEAD_DIM)
        yn = dlt * lax.rsqrt(var + GN_EPS) * gnw_ref[:, ql] + gnb_ref[:, ql]
        y_ref[pl.ds(c * tchunk, tchunk), :, ql] = (yn + bonus).reshape(tchunk, nseq, LANES)

    def advance(gens):
        return [next(g, "done") is None for g in gens]

    def drain(gens):
        while any(advance(gens)):
            pass

    def chunk_group(cs):
        sts = [[{} for _ in range(npl)] for _ in cs]
        ga = [[phase_a(c, q, sts[i][q]) for q in range(npl)] for i, c in enumerate(cs)]
        gb = [[phase_b(c, q, sts[i][q]) for q in range(npl)] for i, c in enumerate(cs)]
        gc = [[phase_c(c, q, sts[i][q]) for q in range(npl)] for i, c in enumerate(cs)]
        drain(ga[0])
        for i in range(len(cs)):
            fill = []
            if i + 1 < len(cs):
                fill.append(ga[i + 1])
            if i > 0:
                fill.append(gc[i - 1])
            while any(advance(gb[i])):
                for f in fill:
                    advance(f)
            for f in fill:
                drain(f)
        drain(gc[-1])

    unroll = 2 if nchunks % 2 == 0 else 1

    def chunk(i, carry):
        chunk_group([i * unroll + e for e in range(unroll)])
        return carry

    lax.fori_loop(0, nchunks // unroll, chunk, 0)

    @pl.when(pl.program_id(2) == pl.num_programs(2) - 1)
    def _():
        out = sout_ref if has_alias else sout_ref.at[layer]
        if not has_alias:
            for other in range(sout_ref.shape[0]):
                if other != layer:
                    sout_ref[other] = jnp.zeros(sout_ref.shape[1:], F32)
        for s in range(nseq):
            for q in range(npl):
                out[s, 2 * q] = s_scr[s, q, :HEAD_DIM, :HEAD_DIM]
                out[s, 2 * q + 1] = s_scr[s, q, HEAD_DIM:, HEAD_DIM:]


def _wkv_call(r, lw, k, v, kk, b, states, new_states, layer, p, nb, nt):
    nslab, _, slab = r.shape
    npl = slab // LANES
    ngroup = nb // SUBLANES
    tchunk = min(WKV_CHUNK, nt)
    tb = min(WKV_BLOCK_ELEMS // (SUBLANES * slab), nt)
    nchunks = tb // tchunk
    assert nt % tb == 0 and tb % tchunk == 0 and nb % SUBLANES == 0
    view = lambda a: a.reshape(nslab, nt, ngroup, SUBLANES, slab)
    hps = 2 * npl
    batch_minor = nb == LANES
    if batch_minor:
        order = lambda f: (lambda p_, g, t: f(g, p_, t))
        grid = (nslab, ngroup, nt // tb)
        states_in = states.transpose(0, 2, 3, 4, 1)
        st_in_spec = pl.BlockSpec((None, hps, HEAD_DIM, HEAD_DIM, nb), lambda p_, g, t: (layer, p_, 0, 0, 0),
                                  pipeline_mode=pl.Buffered(1))
        relayout_scr = [pltpu.VMEM((npl, HEAD_DIM * RELAYOUT_PITCH, LANES), F32)]
    else:
        order = lambda f: f
        grid = (ngroup, nslab, nt // tb)
        states_in = states
        st_in_spec = pl.BlockSpec((None, SUBLANES, hps, HEAD_DIM, HEAD_DIM),
                                  lambda g, p_, t: (layer, g, p_, 0, 0))
        relayout_scr = []
    act_spec = pl.BlockSpec((None, tb, None, SUBLANES, slab), order(lambda g, p_, t: (p_, t, g, 0, 0)))
    has_alias = new_states is not None
    if has_alias:
        st_out_spec = pl.BlockSpec((None, SUBLANES, hps, HEAD_DIM, HEAD_DIM),
                                   order(lambda g, p_, t: (layer, g, p_, 0, 0)))
    else:
        st_out_spec = pl.BlockSpec((states.shape[0], SUBLANES, hps, HEAD_DIM, HEAD_DIM),
                                   order(lambda g, p_, t: (0, g, p_, 0, 0)))
    extra_in = [new_states] if has_alias else []
    extra_spec = [pl.BlockSpec(memory_space=pl.ANY)] if has_alias else []
    par_spec = pl.BlockSpec((None, 1, slab), order(lambda g, p_, t: (layer, 0, p_)))
    n = max(tchunk, 8) * SUBLANES
    y, s_out = pl.pallas_call(
        functools.partial(_wkv_kernel, tchunk, nchunks, has_alias, layer, batch_minor),
        grid=grid,
        in_specs=[act_spec] * 6 + [st_in_spec] + [par_spec] * 3 + extra_spec,
        out_specs=[act_spec, st_out_spec],
        out_shape=[jax.ShapeDtypeStruct((nslab, nt, ngroup, SUBLANES, slab), F32),
                   jax.ShapeDtypeStruct(states.shape, F32)],
        input_output_aliases={10: 1} if has_alias else {},
        scratch_shapes=[pltpu.VMEM((SUBLANES, npl, LANES, LANES), F32)] + relayout_scr
        + [pltpu.VMEM((n, LANES), F32)] * (len(_WKV_SLOTS) * npl),
        compiler_params=pltpu.CompilerParams(
            dimension_semantics=("arbitrary", "arbitrary", "arbitrary"),
            vmem_limit_bytes=VMEM_LIMIT),
        name="wkv7",
    )(view(r), view(lw), view(k), view(v), view(kk), view(b), states_in, p['r_k'], p['gn_w'], p['gn_b'],
      *extra_in)
    return y.reshape(nslab, nt * nb, slab), s_out


def _ffn_stages(x, gffn, w1_ref, w2_ref, out):
    h = _rmsnorm(x, gffn).astype(BF16)
    dff = w1_ref.shape[1]
    step = min(dff, FFN_CHUNK)
    acc = x
    for c in range(dff // step):
        hh = _dot(h, w1_ref[:, c * step:(c + 1) * step])
        yield
        hh = jnp.square(jnp.maximum(hh, 0.0)).astype(BF16)
        acc = acc + _dot(hh, w2_ref[c * step:(c + 1) * step, :])
        yield
    out.append(acc)


def _ffn(x, gffn, w1_ref, w2_ref):
    out = []
    for _ in _ffn_stages(x, gffn, w1_ref, w2_ref, out):
        pass
    return out[0]


def _attn_ffn_kernel(final_norm, nb, x_ref, y_ref, g_ref, wo_ref, gffn_ref, w1_ref, w2_ref, gfin_ref,
                     o_ref):
    nslab = y_ref.shape[0]
    y = jnp.concatenate([y_ref[q] for q in range(nslab)], axis=1)
    x = _rows_in(x_ref, nb) + _dot((y * g_ref[...]).astype(BF16), wo_ref[...])
    out = _ffn(x, gffn_ref[...], w1_ref, w2_ref)
    if final_norm:
        out = _rmsnorm(out, gfin_ref[...])
    _rows_out(o_ref, out, nb)


def _out_x(rows, d, tm, nb, batch_major_out):
    if batch_major_out:
        return (jax.ShapeDtypeStruct((nb, rows // nb, d), F32),
                pl.BlockSpec((nb, tm // nb, d), lambda i: (0, i, 0)))
    return jax.ShapeDtypeStruct((rows, d), F32), pl.BlockSpec((tm, d), lambda i: (i, 0))


def _attn_ffn_call(x, y, g, p, i, j, final_norm, batch_major_out, nb, tm):
    rows, d = g.shape
    nslab, _, slab = y.shape
    out_sds, out_spec = _out_x(rows, d, tm, nb, batch_major_out)
    ops = [(x, _x_spec(x, tm, nb)), (y, pl.BlockSpec((nslab, tm, slab), lambda i_: (0, i_, 0))),
           (g, pl.BlockSpec((tm, d), lambda i_: (i_, 0))), _pick(p['w_o'], j), _pick(p['norm_ffn'], i),
           _pick(p['ffn_w1'], i), _pick(p['ffn_w2'], i), _pick(p['norm_final'])]
    return pl.pallas_call(
        functools.partial(_attn_ffn_kernel, final_norm, nb),
        grid=(rows // tm,),
        in_specs=[o[1] for o in ops],
        out_specs=out_spec,
        out_shape=out_sds,
        compiler_params=pltpu.CompilerParams(
            dimension_semantics=("arbitrary",), vmem_limit_bytes=VMEM_LIMIT),
        name="attn_out_ffn",
    )(*[o[0] for o in ops])


def _pool_mix_stages(x, prev, first_row, nb, start_pos, gmix_ref, pw_ref, ps_ref, out):
    tm, d = x.shape
    gw = d // len(POOL_WINDOWS)
    h = _rmsnorm(x, gmix_ref[...])
    ext = jnp.concatenate([prev, h], axis=0)
    rowi = lax.broadcasted_iota(jnp.int32, (tm, gw), 0) + first_row
    pos = start_pos + jnp.right_shift(rowi, int(math.log2(nb)))
    yield
    ys = []
    for gi, w in enumerate(POOL_WINDOWS):
        sl = slice(gi * gw, (gi + 1) * gw)
        s = ext[:, sl]
        span = 1
        while span < w:
            s = s[span * nb:] + s[: s.shape[0] - span * nb]
            span *= 2
        s = s[s.shape[0] - tm:]
        cnt = jnp.minimum(pos + 1, w).astype(F32)
        diff = (s / cnt - h[:, sl]).astype(BF16)
        ys.append(_dot(diff, pw_ref[gi]))
        yield
    out += [x + jnp.concatenate(ys, axis=1) * ps_ref[...], ext[tm:]]


def _pool_mix(*args, **kwargs):
    out = []
    for _ in _pool_mix_stages(*args, out=out, **kwargs):
        pass
    return out


def _pool_ffn_kernel(final_norm, nb, start_pos, lookahead, *refs):
    if lookahead:
        x_ref, xn_ref, *refs = refs
    else:
        x_ref, *refs = refs
    buf_ref, gmix_ref, pw_ref, ps_ref, gffn_ref, w1_ref, w2_ref, gfin_ref, o_ref, buf_out, carry, *scr = refs
    tm = x_ref.shape[0] if len(x_ref.shape) == 2 else nb * x_ref.shape[1]
    step = pl.program_id(0)
    mix = functools.partial(_pool_mix, nb=nb, start_pos=start_pos, gmix_ref=gmix_ref, pw_ref=pw_ref,
                            ps_ref=ps_ref)

    if lookahead:
        xm, = scr

        @pl.when(step == 0)
        def _():
            x0, tail0 = mix(_rows_in(x_ref, nb), buf_ref[...], 0)
            xm[0] = x0
            carry[...] = tail0

        res, nxt = [], []
        ffn = _ffn_stages(xm[step % 2], gffn_ref[...], w1_ref, w2_ref, res)
        ahead = _pool_mix_stages(_rows_in(xn_ref, nb), carry[...], (step + 1) * tm, nb, start_pos,
                                 gmix_ref, pw_ref, ps_ref, nxt)
        busy = True
        while busy:
            busy = next(ffn, "done") is None
            busy = (next(ahead, "done") is None) or busy
        out = res[0]
        xn, tail = nxt
    else:
        x, tail = mix(_rows_in(x_ref, nb), buf_ref[...], 0)
        buf_out[...] = tail
        out = _ffn(x, gffn_ref[...], w1_ref, w2_ref)
    if final_norm:
        out = _rmsnorm(out, gfin_ref[...])
    _rows_out(o_ref, out, nb)
    if lookahead:
        xm[(step + 1) % 2] = xn
        carry[...] = jnp.where(step < pl.num_programs(0) - 1, tail, carry[...])
        buf_out[...] = carry[...]


def _pool_ffn_call(x, buf, p, i, j, final_norm, batch_major_out, nb, start_pos, tm):
    d = x.shape[-1]
    rows = x.shape[0] if x.ndim == 2 else x.shape[0] * x.shape[1]
    assert nb & (nb - 1) == 0
    nsteps = rows // tm
    lookahead = nsteps > 1
    out_sds, out_spec = _out_x(rows, d, tm, nb, batch_major_out)
    ops = [(x, _x_spec(x, tm, nb))]
    if lookahead:
        ahead = lambda i_: jnp.minimum(i_ + 1, nsteps - 1)
        if x.ndim == 2:
            ops.append((x, pl.BlockSpec((tm, d), lambda i_: (ahead(i_), 0))))
        else:
            ops.append((x, pl.BlockSpec((nb, tm // nb, d), lambda i_: (0, ahead(i_), 0))))
    ops += [(buf, _const_spec(buf.shape)), _pick(p['norm_mix'], i),
            _pick(p['pool_w'], j), _pick(p['pool_scale'], j), _pick(p['norm_ffn'], i),
            _pick(p['ffn_w1'], i), _pick(p['ffn_w2'], i), _pick(p['norm_final'])]
    return pl.pallas_call(
        functools.partial(_pool_ffn_kernel, final_norm, nb, start_pos, lookahead),
        grid=(nsteps,),
        in_specs=[o[1] for o in ops],
        out_specs=[out_spec, _const_spec(buf.shape)],
        out_shape=[out_sds, jax.ShapeDtypeStruct(buf.shape, F32)],
        scratch_shapes=[pltpu.VMEM(buf.shape, F32)] + ([pltpu.VMEM((2, tm, d), F32)] if lookahead else []),
        compiler_params=pltpu.CompilerParams(
            dimension_semantics=("arbitrary",), vmem_limit_bytes=VMEM_LIMIT),
        name="pool_ffn",
    )(*[o[0] for o in ops])


def _trunk(x, start_pos, shift_states, wkv_states, pool_bufs, p):
    nb, nt, d = x.shape
    depth = p['norm_mix'].shape[0]
    rows = nb * nt
    tm = min(ROW_BLOCK, rows)
    assert rows % tm == 0 and tm % nb == 0
    batch_major = nb == SUBLANES
    xt = x if batch_major else x.transpose(1, 0, 2).reshape(rows, d)
    new_shift, new_wkv, new_pool = [], None, []
    v_first = None
    for i in range(depth):
        j = i // 2
        last = i == depth - 1
        bm_out = batch_major and last
        if i % 2 == 0:
            r, lw, k, v, kk, b, g, s_shift = _pre_call(xt, p, i, j, shift_states, v_first, nb, tm)
            if j == 0:
                v_first = v
            y, new_wkv = _wkv_call(r, lw, k, v, kk, b, wkv_states, new_wkv, j, p, nb, nt)
            new_shift.append(s_shift)
            xt = _attn_ffn_call(xt, y, g, p, i, j, last, bm_out, nb, tm)
        else:
            buf = pool_bufs[j].transpose(1, 0, 2).reshape(POOL_BUF * nb, d)
            xt, s_buf = _pool_ffn_call(xt, buf, p, i, j, last, bm_out, nb, start_pos, tm)
            new_pool.append(s_buf.reshape(POOL_BUF, nb, d).transpose(1, 0, 2))
    y = xt if batch_major else xt.reshape(nt, nb, d).transpose(1, 0, 2)
    return y, new_wkv, jnp.stack(new_shift), jnp.stack(new_pool)


def kernel(x_prompt, x_sample, state_wkv, state_shift, state_pool, norm_mix, norm_ffn, norm_final,
           rwkv_mix, rwkv_w_rkv, rwkv_w_o, rwkv_w0, rwkv_w1, rwkv_w2, rwkv_a0, rwkv_a1, rwkv_a2,
           rwkv_v0, rwkv_v1, rwkv_v2, rwkv_g1, rwkv_g2, rwkv_k_k, rwkv_k_a, rwkv_r_k,
           rwkv_gn_w, rwkv_gn_b, pool_w, pool_scale, ffn_w1, ffn_w2):
    bf = lambda a: a.astype(BF16)
    vec = lambda a: a.reshape(a.shape[0], 1, -1)
    p = {'norm_mix': vec(norm_mix), 'norm_ffn': vec(norm_ffn), 'norm_final': norm_final.reshape(1, -1),
         'mix': rwkv_mix, 'w_rkv': bf(rwkv_w_rkv), 'w_o': bf(rwkv_w_o), 'w0': vec(rwkv_w0),
         'w1': bf(rwkv_w1), 'w2': bf(rwkv_w2), 'a0': vec(rwkv_a0), 'a1': bf(rwkv_a1),
         'a2': bf(rwkv_a2), 'v0': vec(rwkv_v0), 'v1': bf(rwkv_v1), 'v2': bf(rwkv_v2),
         'g1': bf(rwkv_g1), 'g2': bf(rwkv_g2), 'k_k': vec(rwkv_k_k), 'k_a': vec(rwkv_k_a),
         'r_k': vec(rwkv_r_k), 'gn_w': vec(rwkv_gn_w), 'gn_b': vec(rwkv_gn_b), 'pool_w': bf(pool_w),
         'pool_scale': vec(pool_scale), 'ffn_w1': bf(ffn_w1), 'ffn_w2': bf(ffn_w2)}
    dt = x_prompt.dtype
    nb, _, d = x_prompt.shape
    n_rwkv = state_wkv.shape[0]
    n_pool = state_pool.shape[0]
    nh = d // HEAD_DIM
    z_shift = jnp.zeros((n_rwkv, nb, d), dt)
    z_wkv = jnp.zeros((n_rwkv, nb, nh, HEAD_DIM, HEAD_DIM), dt)
    z_pool = jnp.zeros((n_pool, nb, POOL_BUF, d), dt)
    y_p, wkv_p, shift_p, pool_p = _trunk(x_prompt, 0, z_shift, z_wkv, z_pool, p)
    y_s, wkv_s, shift_s, pool_s = _trunk(x_sample, PAST_LEN, state_shift, state_wkv, state_pool, p)
    return (y_p, y_s, wkv_p, shift_p, pool_p, wkv_s, shift_s, pool_s)
```
